```python
import jax
import jax.numpy as jnp
from jax import lax
import numpy as np

D_MODEL = 2048
BATCH = 1
SEQ = 8192
DEPTH = 1

N_MLA_HEADS = 8
MLA_NOPE_DIM = 128
MLA_ROPE_DIM = 64
MLA_QK_DIM = MLA_NOPE_DIM + MLA_ROPE_DIM
MLA_V_DIM = 128
Q_LORA_RANK = 512
KV_LORA_RANK = 512
N_DIL_HEADS = 8
DIL_HEAD_DIM = 128
DIL_PATTERNS = ((128, 1), (512, 4), (2048, 16))
MIX_WIDTH = N_MLA_HEADS * MLA_V_DIM + N_DIL_HEADS * DIL_HEAD_DIM
IN_PROJ_WIDTH = Q_LORA_RANK + KV_LORA_RANK + MLA_ROPE_DIM + 3 * N_DIL_HEADS * DIL_HEAD_DIM
ROPE_THETA = 10000.0
QUERY_BLOCK = 128
NORM_EPS = 1e-6
NEG_INF = -1e30
N_EXPERT_GROUPS = 8
EXPERTS_PER_GROUP = 8
N_EXPERTS = N_EXPERT_GROUPS * EXPERTS_PER_GROUP
TOP_K_IN_GROUP = 2
EXPERT_FF = 1408
MOE_BLOCK = 64

kernel_name = 'hybrid_mla_dilated_hmoe_block'


def _rms_norm(x, g):
    xf = x.astype(jnp.float32)
    y = xf * lax.rsqrt(jnp.mean(xf * xf, axis=-1, keepdims=True) + NORM_EPS)
    return (y * g.astype(jnp.float32)).astype(x.dtype)


def _rope(x, pos):
    half = x.shape[-1] // 2
    inv_freq = ROPE_THETA ** (-jnp.arange(half, dtype=jnp.float32) / half)
    ang = pos.astype(jnp.float32)[:, None] * inv_freq[None, :]
    cos = jnp.cos(ang)[None, :, None, :]
    sin = jnp.sin(ang)[None, :, None, :]
    xf = x.astype(jnp.float32)
    x1, x2 = xf[..., :half], xf[..., half:]
    return jnp.concatenate([x1 * cos - x2 * sin, x2 * cos + x1 * sin], axis=-1).astype(x.dtype)


def _dense_block_attention(q, k, v):
    B, S, H, dq = q.shape
    dv = v.shape[-1]
    nb = S // QUERY_BLOCK
    scale = dq ** -0.5
    q_blocks = q.reshape(B, nb, QUERY_BLOCK, H, dq).transpose(1, 0, 3, 2, 4)
    k_t = k.transpose(0, 2, 1, 3)
    v_t = v.transpose(0, 2, 1, 3)

    def attend(q_blk):
        s = jnp.einsum('bhqd,bhkd->bhqk', q_blk, k_t, preferred_element_type=jnp.float32) * scale
        p = jax.nn.softmax(s, axis=-1)
        return jnp.einsum('bhqk,bhkd->bhqd', p.astype(v_t.dtype), v_t)

    o = lax.map(attend, q_blocks)
    return o.transpose(1, 0, 3, 2, 4).reshape(B, S, H, dv)


def _mla_mixer(c_q, c_kv, k_rope, g_cq, g_ckv, w_uq, w_ukv, g_qn, g_kn, pos):
    B, S, _ = c_q.shape
    H = N_MLA_HEADS
    q = (_rms_norm(c_q, g_cq) @ w_uq).reshape(B, S, H, MLA_QK_DIM)
    kv = (_rms_norm(c_kv, g_ckv) @ w_ukv).reshape(B, S, H, MLA_NOPE_DIM + MLA_V_DIM)
    k_nope, v = kv[..., :MLA_NOPE_DIM], kv[..., MLA_NOPE_DIM:]
    k_pe = jnp.broadcast_to(k_rope[:, :, None, :], (B, S, H, MLA_ROPE_DIM))
    k = jnp.concatenate([k_nope, k_pe], axis=-1)
    q = _rms_norm(q, g_qn)
    k = _rms_norm(k, g_kn)
    q = jnp.concatenate([q[..., :MLA_NOPE_DIM], _rope(q[..., MLA_NOPE_DIM:], pos)], axis=-1)
    k = jnp.concatenate([k[..., :MLA_NOPE_DIM], _rope(k[..., MLA_NOPE_DIM:], pos)], axis=-1)
    o = _dense_block_attention(q, k, v)
    return o.reshape(B, S, H * MLA_V_DIM)


def _banded_attention(q, k, v, half_w):
    B, G, H, L, d = q.shape
    qb = min(QUERY_BLOCK, L)
    nb = -(-L // qb)
    Lp = nb * qb
    kb_len = qb + 2 * half_w
    lead = [(0, 0)] * 3
    qp = jnp.pad(q, lead + [(0, Lp - L), (0, 0)])
    kp = jnp.pad(k, lead + [(half_w, Lp - L + half_w), (0, 0)])
    vp = jnp.pad(v, lead + [(half_w, Lp - L + half_w), (0, 0)])
    start = jnp.arange(nb) * qb
    kidx = start[:, None] + jnp.arange(kb_len)[None, :]
    kpos = kidx - half_w
    qpos = start[:, None] + jnp.arange(qb)[None, :]
    mask = ((jnp.abs(qpos[:, :, None] - kpos[:, None, :]) <= half_w)
            & (kpos[:, None, :] >= 0) & (kpos[:, None, :] < L))
    kb = jnp.take(kp, kidx, axis=3)
    vb = jnp.take(vp, kidx, axis=3)
    qr = qp.reshape(B, G, H, nb, qb, d)
    s = jnp.einsum('bghnqd,bghnkd->bghnqk', qr, kb, preferred_element_type=jnp.float32) * (d ** -0.5)
    s = jnp.where(mask, s, NEG_INF)
    m = jnp.max(s, axis=-1, keepdims=True)
    p = jnp.exp(s - m)
    den = jnp.sum(p, axis=-1, keepdims=True)
    o = jnp.einsum('bghnqk,bghnkd->bghnqd', p, vb.astype(jnp.float32)) / den
    lse = (m + jnp.log(den))[..., 0]
    o = o.reshape(B, G, H, Lp, d)[..., :L, :]
    lse = lse.reshape(B, G, H, Lp)[..., :L]
    return o, lse


def _strided_window_attention(q, k, v, dil, half_w):
    B, S, H, d = q.shape
    L = S // dil

    def to_sub(t):
        return t.reshape(B, L, dil, H, d).transpose(0, 2, 3, 1, 4)

    o, lse = _banded_attention(to_sub(q), to_sub(k), to_sub(v), half_w)
    o = o.transpose(0, 3, 1, 2, 4).reshape(B, S, H, d)
    lse = lse.transpose(0, 3, 1, 2).reshape(B, S, H)
    return o, lse


def _dilated_mixer(q, k, v, g_qn, g_kn, pos):
    B, S, H, d = q.shape
    q = _rope(_rms_norm(q, g_qn), pos)
    k = _rope(_rms_norm(k, g_kn), pos)
    outs, lses = [], []
    for window, dil in DIL_PATTERNS:
        half_w = window // (2 * dil)
        o, lse = _strided_window_attention(q, k, v, dil, half_w)
        outs.append(o)
        lses.append(lse)
    w = jax.nn.softmax(jnp.stack(lses, axis=0), axis=0)
    o = jnp.sum(w[..., None] * jnp.stack(outs, axis=0), axis=0)
    return o.astype(v.dtype).reshape(B, S, H * d)


def _hier_moe(x, w_group, b_group, w_expert, b_expert, w1, w3, w2):
    B, S, D = x.shape
    T = B * S
    xt = x.reshape(T, D)
    tok = jnp.arange(T)
    coarse = (xt @ w_group).astype(jnp.float32) + b_group.astype(jnp.float32)
    coarse_p = jax.nn.softmax(coarse, axis=-1)
    g = jnp.argmax(coarse, axis=-1)
    p_g = coarse_p[tok, g]
    fine = (xt @ w_expert).astype(jnp.float32) + b_expert.astype(jnp.float32)
    fine_g = fine.reshape(T, N_EXPERT_GROUPS, EXPERTS_PER_GROUP)[tok, g]
    top_v, top_j = lax.top_k(fine_g, TOP_K_IN_GROUP)
    gate = p_g[:, None] * jax.nn.softmax(top_v, axis=-1)
    eid = g[:, None] * EXPERTS_PER_GROUP + top_j

    A = T * TOP_K_IN_GROUP
    flat_e = eid.reshape(A).astype(jnp.int32)
    flat_tok = jnp.repeat(tok, TOP_K_IN_GROUP)
    flat_w = gate.reshape(A)
    order = jnp.argsort(flat_e)
    se, stok, sw = flat_e[order], flat_tok[order], flat_w[order]
    counts = jnp.bincount(flat_e, length=N_EXPERTS)
    padded = (counts + MOE_BLOCK - 1) // MOE_BLOCK * MOE_BLOCK
    starts = jnp.cumsum(counts) - counts
    pend = jnp.cumsum(padded)
    pstarts = pend - padded
    dest = pstarts[se] + (jnp.arange(A) - starts[se])
    n_blocks = -(-A // MOE_BLOCK) + N_EXPERTS
    R = n_blocks * MOE_BLOCK
    buf_tok = jnp.full((R,), T, dtype=jnp.int32).at[dest].set(stok)
    buf_w = jnp.zeros((R,), jnp.float32).at[dest].set(sw)
    block_e = jnp.minimum(jnp.searchsorted(pend, jnp.arange(n_blocks) * MOE_BLOCK, side='right'),
                          N_EXPERTS - 1)
    x_pad = jnp.concatenate([xt, jnp.zeros((1, D), xt.dtype)], axis=0)
    xb = x_pad[buf_tok].reshape(n_blocks, MOE_BLOCK, D)

    def expert_block(args):
        x_blk, e = args
        h = jax.nn.silu(x_blk @ w1[e]) * (x_blk @ w3[e])
        return h @ w2[e]

    yb = lax.map(expert_block, (xb, block_e))
    y = yb.reshape(R, D) * buf_w[:, None].astype(yb.dtype)
    out = jnp.zeros((T + 1, D), y.dtype).at[buf_tok].add(y)[:T]
    return out.astype(x.dtype).reshape(B, S, D)


def setup_inputs(seed: int = 0) -> dict:
    key = jax.random.key(seed)
    ks = jax.random.split(key, 20)
    L = DEPTH
    HD = N_DIL_HEADS * DIL_HEAD_DIM

    def nrm(k, shape, fan_in):
        return jax.random.normal(k, shape, jnp.float32) * (fan_in ** -0.5)

    def gain(k, shape):
        return 1.0 + 0.02 * jax.random.normal(k, shape, jnp.float32)

    return {
        'x': jax.random.normal(ks[0], (BATCH, SEQ, D_MODEL), jnp.float32),
        'norm1_g': gain(ks[1], (L, D_MODEL)),
        'w_in': nrm(ks[2], (L, D_MODEL, IN_PROJ_WIDTH), D_MODEL),
        'g_cq': gain(ks[3], (L, Q_LORA_RANK)),
        'g_ckv': gain(ks[4], (L, KV_LORA_RANK)),
        'w_uq': nrm(ks[5], (L, Q_LORA_RANK, N_MLA_HEADS * MLA_QK_DIM), Q_LORA_RANK),
        'w_ukv': nrm(ks[6], (L, KV_LORA_RANK, N_MLA_HEADS * (MLA_NOPE_DIM + MLA_V_DIM)), KV_LORA_RANK),
        'mla_q_norm_g': gain(ks[7], (L, MLA_QK_DIM)),
        'mla_k_norm_g': gain(ks[8], (L, MLA_QK_DIM)),
        'dil_q_norm_g': gain(ks[9], (L, DIL_HEAD_DIM)),
        'dil_k_norm_g': gain(ks[10], (L, DIL_HEAD_DIM)),
        'w_out': nrm(ks[11], (L, MIX_WIDTH, D_MODEL), MIX_WIDTH),
        'norm2_g': gain(ks[12], (L, D_MODEL)),
        'w_group': nrm(ks[13], (L, D_MODEL, N_EXPERT_GROUPS), D_MODEL),
        'b_group': 0.01 * jax.random.normal(ks[14], (L, N_EXPERT_GROUPS), jnp.float32),
        'w_expert': nrm(ks[15], (L, D_MODEL, N_EXPERTS), D_MODEL),
        'b_expert': 0.01 * jax.random.normal(ks[16], (L, N_EXPERTS), jnp.float32),
        'w1': nrm(ks[17], (L, N_EXPERTS, D_MODEL, EXPERT_FF), D_MODEL),
        'w3': nrm(ks[18], (L, N_EXPERTS, D_MODEL, EXPERT_FF), D_MODEL),
        'w2': nrm(ks[19], (L, N_EXPERTS, EXPERT_FF, D_MODEL), EXPERT_FF),
    }


def reference(x, norm1_g, w_in, g_cq, g_ckv, w_uq, w_ukv, mla_q_norm_g, mla_k_norm_g,
              dil_q_norm_g, dil_k_norm_g, w_out, norm2_g, w_group, b_group, w_expert,
              b_expert, w1, w3, w2):
    B, S, _ = x.shape
    pos = jnp.arange(S)
    HD = N_DIL_HEADS * DIL_HEAD_DIM
    o1 = Q_LORA_RANK
    o2 = o1 + KV_LORA_RANK
    o3 = o2 + MLA_ROPE_DIM
    o4 = o3 + HD
    o5 = o4 + HD
    for l in range(DEPTH):
        h = _rms_norm(x, norm1_g[l])
        z = h @ w_in[l]
        mla_o = _mla_mixer(z[..., :o1], z[..., o1:o2], z[..., o2:o3], g_cq[l], g_ckv[l],
                           w_uq[l], w_ukv[l], mla_q_norm_g[l], mla_k_norm_g[l], pos)
        dq = z[..., o3:o4].reshape(B, S, N_DIL_HEADS, DIL_HEAD_DIM)
        dk = z[..., o4:o5].reshape(B, S, N_DIL_HEADS, DIL_HEAD_DIM)
        dv = z[..., o5:].reshape(B, S, N_DIL_HEADS, DIL_HEAD_DIM)
        dil_o = _dilated_mixer(dq, dk, dv, dil_q_norm_g[l], dil_k_norm_g[l], pos)
        x = x + jnp.concatenate([mla_o, dil_o], axis=-1) @ w_out[l]
        x = x + _hier_moe(_rms_norm(x, norm2_g[l]), w_group[l], b_group[l], w_expert[l],
                          b_expert[l], w1[l], w3[l], w2[l])
    return x
```

```python
import functools

import jax
import jax.numpy as jnp
from jax import lax
from jax.experimental import pallas as pl
from jax.experimental.pallas import tpu as pltpu

D_MODEL = 2048
N_MLA_HEADS = 8
MLA_NOPE_DIM = 128
MLA_ROPE_DIM = 64
MLA_QK_DIM = MLA_NOPE_DIM + MLA_ROPE_DIM
MLA_V_DIM = 128
Q_LORA_RANK = 512
KV_LORA_RANK = 512
N_DIL_HEADS = 8
DIL_HEAD_DIM = 128
DIL_PATTERNS = ((128, 1), (512, 4), (2048, 16))
ROPE_THETA = 10000.0
NORM_EPS = 1e-6
NEG_INF = -1e30
N_EXPERT_GROUPS = 8
EXPERTS_PER_GROUP = 8
N_EXPERTS = N_EXPERT_GROUPS * EXPERTS_PER_GROUP
TOP_K = 2
EXPERT_FF = 1408

LANES = 128
MXU_DIM = 256
HD = N_DIL_HEADS * DIL_HEAD_DIM
MLA_PAD = MXU_DIM
Z_DQ, Z_DK, Z_DV = 0, HD, 2 * HD
Z_CQ = 3 * HD
Z_CKV = Z_CQ + Q_LORA_RANK
Z_KR = Z_CKV + KV_LORA_RANK
Z_WIDTH = Z_KR + 2 * MLA_ROPE_DIM

VMEM_LIMIT = 56 * 1024 * 1024

F32 = jnp.float32
BF16 = jnp.bfloat16


def _cparams(sem, vmem=VMEM_LIMIT):
    return pltpu.CompilerParams(dimension_semantics=sem, vmem_limit_bytes=vmem)


def _dot(a, b):
    return jnp.dot(a, b, preferred_element_type=F32)


def _dot_nt(a, b):
    return lax.dot_general(a, b, (((1,), (1,)), ((), ())), preferred_element_type=F32)


def _inproj_kernel(x_ref, g_ref, w_ref, z_ref):
    x = x_ref[...]
    r = lax.rsqrt(jnp.mean(x * x, axis=-1, keepdims=True) + NORM_EPS)
    h = (x * r * g_ref[...]).astype(BF16)
    z_ref[...] = _dot(h, w_ref[...]).astype(BF16)


def _inproj(x2, g1, w_z, tm=256):
    S = x2.shape[0]
    return pl.pallas_call(
        _inproj_kernel,
        grid=(S // tm,),
        in_specs=[
            pl.BlockSpec((tm, D_MODEL), lambda i: (i, 0)),
            pl.BlockSpec((1, D_MODEL), lambda i: (0, 0)),
            pl.BlockSpec((D_MODEL, Z_WIDTH), lambda i: (0, 0)),
        ],
        out_specs=pl.BlockSpec((tm, Z_WIDTH), lambda i: (i, 0)),
        out_shape=jax.ShapeDtypeStruct((S, Z_WIDTH), BF16),
        compiler_params=_cparams(("parallel",)),
        name="inproj",
    )(x2, g1, w_z)


def _prep_kernel(zdq_ref, zdk_ref, zcq_ref, zckv_ref, zkr_ref,
                 gcq_ref, gckv_ref, wq_ref, wkv_ref, gq_ref, gkn_ref, gkp_ref,
                 gdq_ref, gdk_ref, t64_ref, cos_ref, sin_ref,
                 q_ref, k_ref, v_ref, dq_ref, dk_ref):
    lane = lax.broadcasted_iota(jnp.int32, (1, LANES), 1)
    first_half = lane < MLA_ROPE_DIM

    def rms_rows(c, g):
        c = c.astype(F32)
        r = lax.rsqrt(jnp.mean(c * c, axis=-1, keepdims=True) + NORM_EPS)
        return (c * r * g).astype(BF16)

    cq = rms_rows(zcq_ref[...], gcq_ref[...])
    ckv = rms_rows(zckv_ref[...], gckv_ref[...])
    qe = _dot(cq, wq_ref[...])
    kv = _dot(ckv, wkv_ref[...])
    kr = zkr_ref[...].astype(F32)
    kr_ss = jnp.sum(jnp.where(first_half, kr * kr, 0.0), axis=-1, keepdims=True)
    t64 = t64_ref[...]
    q_scale = MLA_QK_DIM ** -0.5

    def rope64(ext):
        t = ext * t64
        return jnp.where(first_half, t + pltpu.roll(t, MLA_ROPE_DIM, 1), 0.0)

    for h in range(N_MLA_HEADS):
        base = h * MLA_PAD
        qn = qe[:, base:base + LANES]
        qp = qe[:, base + LANES:base + 2 * LANES]
        ss = (jnp.sum(qn * qn, axis=-1, keepdims=True)
              + jnp.sum(jnp.where(first_half, qp * qp, 0.0), axis=-1, keepdims=True))
        r = lax.rsqrt(ss * (1.0 / MLA_QK_DIM) + NORM_EPS) * q_scale
        qn = qn * r * gq_ref[:, :LANES]
        qp = rope64(qp * r * gq_ref[:, LANES:])
        q_ref[:, base:base + LANES] = qn.astype(BF16)
        q_ref[:, base + LANES:base + MLA_PAD] = qp.astype(BF16)

        kn = kv[:, base:base + LANES]
        ss = jnp.sum(kn * kn, axis=-1, keepdims=True) + kr_ss
        r = lax.rsqrt(ss * (1.0 / MLA_QK_DIM) + NORM_EPS)
        kn = kn * r * gkn_ref[...]
        kp = rope64(kr * r * gkp_ref[...])
        k_ref[:, base:base + LANES] = kn.astype(BF16)
        k_ref[:, base + LANES:base + MLA_PAD] = kp.astype(BF16)
        v_ref[:, h * MLA_V_DIM:(h + 1) * MLA_V_DIM] = kv[:, base + LANES:base + 2 * LANES].astype(BF16)

    cos = cos_ref[...]
    sin = sin_ref[...]
    d_scale = DIL_HEAD_DIM ** -0.5

    def dil_head(x, g, scale):
        x = x.astype(F32)
        r = lax.rsqrt(jnp.mean(x * x, axis=-1, keepdims=True) + NORM_EPS)
        y = x * r * g
        return ((y * cos + pltpu.roll(y, DIL_HEAD_DIM // 2, 1) * sin) * scale).astype(BF16)

    for h in range(N_DIL_HEADS):
        sl = slice(h * DIL_HEAD_DIM, (h + 1) * DIL_HEAD_DIM)
        dq_ref[:, sl] = dil_head(zdq_ref[:, sl], gdq_ref[...], d_scale)
        dk_ref[:, sl] = dil_head(zdk_ref[:, sl], gdk_ref[...], 1.0)


def _prep(z, gcq, gckv, wq_ext, wkv, gq_ext, gk_nope, gk_pe, gdq, gdk, t64, cos128, sin128, tm=256):
    S = z.shape[0]
    row = lambda w, j: pl.BlockSpec((tm, w), lambda i, j=j: (i, j))
    full = lambda a: pl.BlockSpec(a.shape, lambda i: (0, 0))
    qk_w = N_MLA_HEADS * MLA_PAD
    return pl.pallas_call(
        _prep_kernel,
        grid=(S // tm,),
        in_specs=[
            row(HD, Z_DQ // HD), row(HD, Z_DK // HD),
            row(Q_LORA_RANK, Z_CQ // Q_LORA_RANK), row(KV_LORA_RANK, Z_CKV // KV_LORA_RANK),
            row(LANES, Z_KR // LANES),
            full(gcq), full(gckv), full(wq_ext), full(wkv), full(gq_ext), full(gk_nope), full(gk_pe),
            full(gdq), full(gdk),
            row(LANES, 0), row(LANES, 0), row(LANES, 0),
        ],
        out_specs=[row(qk_w, 0), row(qk_w, 0), row(HD, 0), row(HD, 0), row(HD, 0)],
        out_shape=[
            jax.ShapeDtypeStruct((S, qk_w), BF16), jax.ShapeDtypeStruct((S, qk_w), BF16),
            jax.ShapeDtypeStruct((S, N_MLA_HEADS * MLA_V_DIM), BF16),
            jax.ShapeDtypeStruct((S, HD), BF16), jax.ShapeDtypeStruct((S, HD), BF16),
        ],
        compiler_params=_cparams(("parallel",)),
        name="qkv_prep",
    )(z, z, z, z, z, gcq, gckv, wq_ext, wkv, gq_ext, gk_nope, gk_pe, gdq, gdk, t64, cos128, sin128)


def _mla_attn_kernel(q_ref, k_ref, v_ref, o_ref, m_ref, l_ref, acc_ref, *, tk):
    S = k_ref.shape[0]
    q = q_ref[...]
    m_ref[...] = jnp.full(m_ref.shape, -jnp.inf, F32)
    l_ref[...] = jnp.zeros(l_ref.shape, F32)
    acc_ref[...] = jnp.zeros(acc_ref.shape, F32)

    def body(c, carry):
        off = pl.multiple_of(c * tk, tk)
        k = k_ref[pl.ds(off, tk), :]
        v = v_ref[pl.ds(off, tk), :]
        s = _dot_nt(q, k)
        m_old = m_ref[...]
        m_new = jnp.maximum(m_old, jnp.max(s, axis=-1, keepdims=True))
        alpha = jnp.exp(m_old - m_new)
        p = jnp.exp(s - m_new)
        l_ref[...] = alpha * l_ref[...] + jnp.sum(p, axis=-1, keepdims=True)
        acc_ref[...] = alpha * acc_ref[...] + _dot(p.astype(BF16), v)
        m_ref[...] = m_new
        return carry

    lax.fori_loop(0, S // tk, body, 0)
    o_ref[...] = (acc_ref[...] / l_ref[...]).astype(o_ref.dtype)


def _mla_attn(q, k, v, tq=256, tk=512):
    S = q.shape[0]
    return pl.pallas_call(
        functools.partial(_mla_attn_kernel, tk=tk),
        grid=(N_MLA_HEADS, S // tq),
        in_specs=[
            pl.BlockSpec((tq, MLA_PAD), lambda h, i: (i, h)),
            pl.BlockSpec((S, MLA_PAD), lambda h, i: (0, h)),
            pl.BlockSpec((S, MLA_V_DIM), lambda h, i: (0, h)),
        ],
        out_specs=pl.BlockSpec((tq, MLA_V_DIM), lambda h, i: (i, h)),
        out_shape=jax.ShapeDtypeStruct((S, N_MLA_HEADS * MLA_V_DIM), BF16),
        scratch_shapes=[pltpu.VMEM((tq, 1), F32), pltpu.VMEM((tq, 1), F32), pltpu.VMEM((tq, MLA_V_DIM), F32)],
        compiler_params=_cparams(("parallel", "parallel")),
        name="mla_attn",
    )(q, k, v)


DIL_QB = 128


def _dil_kernel(*refs, half_w, first, last):
    if first:
        q_ref, k_ref, v_ref = refs[:3]
        outs = refs[3:]
    else:
        q_ref, k_ref, v_ref, po_ref, pl_ref = refs[:5]
        outs = refs[5:]
    o_ref = outs[0]
    L = q_ref.shape[0]
    kw = DIL_QB + 2 * half_w

    def body(b, carry):
        qs = pl.multiple_of(b * DIL_QB, DIL_QB)
        ks = pl.multiple_of(jnp.clip(qs - half_w, 0, L - kw), half_w)
        q = q_ref[pl.ds(qs, DIL_QB), :]
        k = k_ref[pl.ds(ks, kw), :]
        v = v_ref[pl.ds(ks, kw), :]
        s = _dot_nt(q, k)
        qpos = qs + lax.broadcasted_iota(jnp.int32, (DIL_QB, kw), 0)
        kpos = ks + lax.broadcasted_iota(jnp.int32, (DIL_QB, kw), 1)
        s = jnp.where(jnp.abs(qpos - kpos) <= half_w, s, NEG_INF)
        m = jnp.max(s, axis=-1, keepdims=True)
        p = jnp.exp(s - m)
        den = jnp.sum(p, axis=-1, keepdims=True)
        acc = _dot(p.astype(BF16), v)
        if not first:
            lse_prev = pl_ref[pl.ds(qs, DIL_QB), :][:, :1]
            o_prev = po_ref[pl.ds(qs, DIL_QB), :]
            big = jnp.maximum(lse_prev, m)
            a_prev = jnp.exp(lse_prev - big)
            a_new = jnp.exp(m - big)
            acc = a_prev * o_prev + a_new * acc
            den = a_prev + a_new * den
            m = big
        o = acc / den
        o_ref[pl.ds(qs, DIL_QB), :] = o.astype(o_ref.dtype)
        if not last:
            lse = m + jnp.log(den)
            outs[1][pl.ds(qs, DIL_QB), :] = jnp.broadcast_to(lse, (DIL_QB, DIL_HEAD_DIM))
        return carry

    lax.fori_loop(0, L // DIL_QB, body, 0, unroll=2)


def _dil_pattern(dq, dk, z, prev, window, dil, last):
    S = dq.shape[0]
    L = S // dil
    half_w = window // (2 * dil)
    first = prev is None
    H = N_DIL_HEADS
    view = lambda a: a.reshape(L, dil * a.shape[1])
    zcols = Z_WIDTH // DIL_HEAD_DIM
    hd_spec = pl.BlockSpec((L, DIL_HEAD_DIM), lambda r, h: (0, r * H + h))
    v_spec = pl.BlockSpec((L, DIL_HEAD_DIM), lambda r, h: (0, r * zcols + Z_DV // DIL_HEAD_DIM + h))
    args = [view(dq), view(dk), view(z)]
    in_specs = [hd_spec, hd_spec, v_spec]
    if not first:
        args += [view(prev[0]), view(prev[1])]
        in_specs += [hd_spec, hd_spec]
    if last:
        out_shape = [jax.ShapeDtypeStruct((L, dil * HD), BF16)]
        out_specs = [hd_spec]
    else:
        out_shape = [jax.ShapeDtypeStruct((L, dil * HD), F32)] * 2
        out_specs = [hd_spec, hd_spec]
    outs = pl.pallas_call(
        functools.partial(_dil_kernel, half_w=half_w, first=first, last=last),
        grid=(dil, H),
        in_specs=in_specs,
        out_specs=out_specs,
        out_shape=out_shape,
        compiler_params=_cparams(("parallel", "parallel")),
        name=f"dil_attn_d{dil}",
    )(*args)
    return [o.reshape(S, HD) for o in outs]


def _outproj_kernel(a_ref, b_ref, x_ref, wa_ref, wb_ref, g_ref, wr_ref, br_ref,
                    x1_ref, xn_ref, ri_ref, rf_ref):
    x1 = x_ref[...] + _dot(a_ref[...], wa_ref[...]) + _dot(b_ref[...], wb_ref[...])
    x1_ref[...] = x1
    r = lax.rsqrt(jnp.mean(x1 * x1, axis=-1, keepdims=True) + NORM_EPS)
    xn = x1 * r * g_ref[...]
    xn_ref[...] = xn
    logits = jnp.dot(xn, wr_ref[...], preferred_element_type=F32,
                     precision=lax.Precision.HIGHEST) + br_ref[...]
    lane = lax.broadcasted_iota(jnp.int32, logits.shape, 1)
    ninf = -jnp.inf

    def first_argmax(vals, vmax):
        return jnp.min(jnp.where(vals == vmax, lane, LANES), axis=-1, keepdims=True)

    coarse = jnp.where(lane < N_EXPERT_GROUPS, logits, ninf)
    cmax = jnp.max(coarse, axis=-1, keepdims=True)
    g = first_argmax(coarse, cmax)
    p_g = 1.0 / jnp.sum(jnp.exp(coarse - cmax), axis=-1, keepdims=True)
    lo = N_EXPERT_GROUPS + g * EXPERTS_PER_GROUP
    fine = jnp.where((lane >= lo) & (lane < lo + EXPERTS_PER_GROUP), logits, ninf)
    v1 = jnp.max(fine, axis=-1, keepdims=True)
    j1 = first_argmax(fine, v1)
    fine2 = jnp.where(lane == j1, ninf, fine)
    v2 = jnp.max(fine2, axis=-1, keepdims=True)
    j2 = first_argmax(fine2, v2)
    e2 = jnp.exp(v2 - v1)
    w1 = 1.0 / (1.0 + e2)
    w2 = e2 / (1.0 + e2)
    ri_ref[...] = jnp.where(lane == 0, j1 - N_EXPERT_GROUPS, jnp.where(lane == 1, j2 - N_EXPERT_GROUPS, 0))
    rf_ref[...] = jnp.where(lane == 0, p_g * w1, jnp.where(lane == 1, p_g * w2, 0.0))


def _outproj_router(mla_o, dil_o, x2, w_out_bf, g2, w_router, b_router, tm=256):
    S = x2.shape[0]
    half = N_MLA_HEADS * MLA_V_DIM
    row = lambda w: pl.BlockSpec((tm, w), lambda i: (i, 0))
    return pl.pallas_call(
        _outproj_kernel,
        grid=(S // tm,),
        in_specs=[
            row(half), row(HD), row(D_MODEL),
            pl.BlockSpec((half, D_MODEL), lambda i: (0, 0)),
            pl.BlockSpec((HD, D_MODEL), lambda i: (1, 0)),
            pl.BlockSpec((1, D_MODEL), lambda i: (0, 0)),
            pl.BlockSpec((D_MODEL, LANES), lambda i: (0, 0)),
            pl.BlockSpec((1, LANES), lambda i: (0, 0)),
        ],
        out_specs=[row(D_MODEL), row(D_MODEL), row(LANES), row(LANES)],
        out_shape=[
            jax.ShapeDtypeStruct((S, D_MODEL), F32), jax.ShapeDtypeStruct((S, D_MODEL), F32),
            jax.ShapeDtypeStruct((S, LANES), jnp.int32), jax.ShapeDtypeStruct((S, LANES), F32),
        ],
        compiler_params=_cparams(("parallel",)),
        name="outproj_router",
    )(mla_o, dil_o, x2, w_out_bf, w_out_bf, g2, w_router, b_router)


MOE_CAP = 1024
MOE_RT = 256
MOE_FC = 256
MOE_NF = EXPERT_FF // MOE_FC
MOE_TAIL = EXPERT_FF - MOE_NF * MOE_FC
assert MOE_TAIL == LANES


def _moe_kernel(ie_ref, is_ref, in_ref, ord_ref,
                xn_hbm, w1_ref, w3_ref, w2_ref, w1t_ref, w3t_ref, w2t_ref,
                y_hbm, xf_ref, xb_ref, acc_ref, sem_in, sem_out):
    i = pl.program_id(0)
    j = pl.program_id(1)
    n = in_ref[i]
    start = is_ref[i]
    n_sub = (n + MOE_RT - 1) // MOE_RT

    def row_in(r):
        tok = lax.shift_right_logical(ord_ref[start + r], 1)
        return pltpu.make_async_copy(xn_hbm.at[pl.ds(tok, 1), :], xf_ref.at[pl.ds(r, 1), :], sem_in)

    def row_out(r):
        return pltpu.make_async_copy(acc_ref.at[pl.ds(r, 1), :], y_hbm.at[pl.ds(ord_ref[start + r], 1), :], sem_out)

    @pl.when((i == 0) & (j == 0))
    def _():
        xf_ref[...] = jnp.zeros(xf_ref.shape, F32)

    @pl.when((n > 0) & (j == 0))
    def _():
        def issue(r, c):
            row_in(r).start()
            return c
        lax.fori_loop(0, n, issue, 0)

        def drain(r, c):
            row_in(r).wait()
            return c
        lax.fori_loop(0, n, drain, 0)

        def cast(t, c):
            off = pl.multiple_of(t * MOE_RT, MOE_RT)
            xb_ref[pl.ds(off, MOE_RT), :] = xf_ref[pl.ds(off, MOE_RT), :].astype(BF16)
            acc_ref[pl.ds(off, MOE_RT), :] = jnp.zeros((MOE_RT, D_MODEL), F32)
            return c
        lax.fori_loop(0, n_sub, cast, 0)

    def chunk(w1, w3, w2):
        w1 = w1.astype(BF16)
        w3 = w3.astype(BF16)
        w2 = w2.astype(BF16)

        def sub(t, c):
            off = pl.multiple_of(t * MOE_RT, MOE_RT)
            xs = xb_ref[pl.ds(off, MOE_RT), :]
            a = _dot(xs, w1)
            b = _dot(xs, w3)
            hmid = (a * jax.nn.sigmoid(a) * b).astype(BF16)
            acc_ref[pl.ds(off, MOE_RT), :] += _dot(hmid, w2)
            return c
        lax.fori_loop(0, n_sub, sub, 0)

    @pl.when(n > 0)
    def _():
        chunk(w1_ref[...], w3_ref[...], w2_ref[...])

    @pl.when((n > 0) & (j == MOE_NF - 1))
    def _():
        chunk(w1t_ref[...], w3t_ref[...], w2t_ref[...])

        def issue(r, c):
            row_out(r).start()
            return c
        lax.fori_loop(0, n, issue, 0)

        def drain(r, c):
            row_out(r).wait()
            return c
        lax.fori_loop(0, n, drain, 0)


def _moe_experts(xn, w1, w3, w2, item_e, item_start, item_n, order):
    S = xn.shape[0]
    A = S * TOP_K
    n_items = item_e.shape[0]
    tail_blk = (MOE_NF * MOE_FC) // MOE_TAIL

    def jx(i, j, n_ref):
        return jnp.where(n_ref[i] > 0, j, MOE_NF - 1)

    up = pl.BlockSpec((None, D_MODEL, MOE_FC), lambda i, j, ie, is_, n_, o: (ie[i], 0, jx(i, j, n_)))
    down = pl.BlockSpec((None, MOE_FC, D_MODEL), lambda i, j, ie, is_, n_, o: (ie[i], jx(i, j, n_), 0))
    up_t = pl.BlockSpec((None, D_MODEL, MOE_TAIL), lambda i, j, ie, is_, n_, o: (ie[i], 0, tail_blk))
    down_t = pl.BlockSpec((None, MOE_TAIL, D_MODEL), lambda i, j, ie, is_, n_, o: (ie[i], tail_blk, 0))
    grid_spec = pltpu.PrefetchScalarGridSpec(
        num_scalar_prefetch=4,
        grid=(n_items, MOE_NF),
        in_specs=[pl.BlockSpec(memory_space=pl.ANY), up, up, down, up_t, up_t, down_t],
        out_specs=pl.BlockSpec(memory_space=pl.ANY),
        scratch_shapes=[
            pltpu.VMEM((MOE_CAP, D_MODEL), F32),
            pltpu.VMEM((MOE_CAP, D_MODEL), BF16),
            pltpu.VMEM((MOE_CAP, D_MODEL), F32),
            pltpu.SemaphoreType.DMA(()),
            pltpu.SemaphoreType.DMA(()),
        ],
    )
    return pl.pallas_call(
        _moe_kernel,
        grid_spec=grid_spec,
        out_shape=jax.ShapeDtypeStruct((A, D_MODEL), F32),
        compiler_params=_cparams(("arbitrary", "arbitrary")),
        name="moe_experts",
    )(item_e, item_start, item_n, order, xn, w1, w3, w2, w1, w3, w2)


def _moe_items(eid, cap):
    A = eid.size
    flat_e = eid.reshape(A)
    order = jnp.argsort(flat_e, stable=True).astype(jnp.int32)
    experts = jnp.arange(N_EXPERTS + 1, dtype=jnp.int32)
    starts = jnp.sum((flat_e[None, :] < experts[:, None]).astype(jnp.int32), axis=1)
    counts = starts[1:] - starts[:-1]
    per_e = (counts + cap - 1) // cap
    item_end = jnp.cumsum(per_e)
    total = item_end[-1]
    n_items = N_EXPERTS + A // cap
    idx = jnp.arange(n_items, dtype=jnp.int32)
    clamped = jnp.minimum(idx, total - 1)
    e = jnp.minimum(jnp.searchsorted(item_end, clamped, side="right"), N_EXPERTS - 1).astype(jnp.int32)
    local = clamped - (item_end[e] - per_e[e])
    used = idx < total
    item_start = jnp.where(used, starts[e] + local * cap, 0).astype(jnp.int32)
    item_n = jnp.where(used, jnp.clip(counts[e] - local * cap, 0, cap), 0).astype(jnp.int32)
    return e, item_start, item_n, order


def _combine_kernel(x1_ref, y_ref, g_ref, o_ref):
    g = g_ref[...]
    y = y_ref[...]
    o_ref[...] = x1_ref[...] + (g[:, 0:1] * y[:, :D_MODEL] + g[:, 1:2] * y[:, D_MODEL:])


def _combine(x1, y, gates, tm=256):
    S = x1.shape[0]
    return pl.pallas_call(
        _combine_kernel,
        grid=(S // tm,),
        in_specs=[
            pl.BlockSpec((tm, D_MODEL), lambda i: (i, 0)),
            pl.BlockSpec((tm, TOP_K * D_MODEL), lambda i: (i, 0)),
            pl.BlockSpec((tm, LANES), lambda i: (i, 0)),
        ],
        out_specs=pl.BlockSpec((tm, D_MODEL), lambda i: (i, 0)),
        out_shape=jax.ShapeDtypeStruct((S, D_MODEL), F32),
        compiler_params=_cparams(("parallel",)),
        name="moe_combine",
    )(x1, y.reshape(S, TOP_K * D_MODEL), gates)


def _rot_half_cols(w, half):
    return jnp.concatenate([-w[..., half:], w[..., :half]], axis=-1)


def _rope_tables(S):
    pos = jnp.arange(S, dtype=F32)[:, None]

    def cs(half):
        inv = ROPE_THETA ** (-jnp.arange(half, dtype=F32) / half)
        ang = pos * inv[None, :]
        return jnp.cos(ang), jnp.sin(ang)

    c32, s32 = cs(MLA_ROPE_DIM // 2)
    t64 = jnp.concatenate([c32, c32, s32, s32], axis=-1)
    c64, s64 = cs(DIL_HEAD_DIM // 2)
    cos128 = jnp.concatenate([c64, c64], axis=-1)
    sin128 = jnp.concatenate([-s64, s64], axis=-1)
    return t64, cos128, sin128


def kernel(x, norm1_g, w_in, g_cq, g_ckv, w_uq, w_ukv, mla_q_norm_g, mla_k_norm_g, dil_q_norm_g,
           dil_k_norm_g, w_out, norm2_g, w_group, b_group, w_expert, b_expert, w1, w3, w2):
    B, S, D = x.shape
    assert B == 1 and D == D_MODEL and norm1_g.shape[0] == 1
    x2 = x.reshape(S, D)
    half_r = MLA_ROPE_DIM // 2
    o1, o2 = Q_LORA_RANK, Q_LORA_RANK + KV_LORA_RANK
    o3 = o2 + MLA_ROPE_DIM

    wi = w_in[0]
    w_kr = wi[:, o2:o3]
    w_z = jnp.concatenate([wi[:, o3:], wi[:, :o2], w_kr, _rot_half_cols(w_kr, half_r)], axis=-1).astype(BF16)

    wq = w_uq[0].reshape(Q_LORA_RANK, N_MLA_HEADS, MLA_QK_DIM)
    wq_pe = wq[..., MLA_NOPE_DIM:]
    wq_ext = jnp.concatenate([wq, _rot_half_cols(wq_pe, half_r)], axis=-1)
    wq_ext = wq_ext.reshape(Q_LORA_RANK, N_MLA_HEADS * MLA_PAD).astype(BF16)
    wkv = w_ukv[0].astype(BF16)

    def ext_gain(g):
        pe = g[MLA_NOPE_DIM:]
        return jnp.concatenate([g, pe[half_r:], pe[:half_r]])[None, :]

    gq_ext = ext_gain(mla_q_norm_g[0])
    gk_ext = ext_gain(mla_k_norm_g[0])
    t64, cos128, sin128 = _rope_tables(S)

    z = _inproj(x2, norm1_g, w_z)
    q, k, v, dq, dk = _prep(z, g_cq, g_ckv, wq_ext, wkv, gq_ext, gk_ext[:, :LANES], gk_ext[:, LANES:],
                            dil_q_norm_g, dil_k_norm_g, t64, cos128, sin128)
    mla_o = _mla_attn(q, k, v)

    state = None
    for p, (window, dil) in enumerate(DIL_PATTERNS):
        state = _dil_pattern(dq, dk, z, state, window, dil, last=(p == len(DIL_PATTERNS) - 1))
    dil_o = state[0]

    pad = LANES - N_EXPERT_GROUPS - N_EXPERTS
    w_router = jnp.concatenate([w_group[0], w_expert[0], jnp.zeros((D, pad), F32)], axis=-1)
    b_router = jnp.concatenate([b_group[0], b_expert[0], jnp.zeros((pad,), F32)])[None, :]
    x1, xn, route_i, route_f = _outproj_router(mla_o, dil_o, x2, w_out[0].astype(BF16), norm2_g,
                                               w_router, b_router)

    item_e, item_start, item_n, order = _moe_items(route_i[:, :TOP_K], MOE_CAP)
    y = _moe_experts(xn, w1[0], w3[0], w2[0], item_e, item_start, item_n, order)
    out = _combine(x1, y, route_f)
    return out.reshape(B, S, D)
```

```python
import functools
import math

import jax
import jax.numpy as jnp
from jax import lax
from jax.experimental import pallas as pl
from jax.experimental.pallas import tpu as pltpu

D_MODEL = 2048
N_MLA_HEADS = 8
MLA_NOPE_DIM = 128
MLA_ROPE_DIM = 64
MLA_QK_DIM = MLA_NOPE_DIM + MLA_ROPE_DIM
MLA_V_DIM = 128
Q_LORA_RANK = 512
KV_LORA_RANK = 512
N_DIL_HEADS = 8
DIL_HEAD_DIM = 128
DIL_PATTERNS = ((128, 1), (512, 4), (2048, 16))
ROPE_THETA = 10000.0
NORM_EPS = 1e-6
NEG_INF = -1e30
N_EXPERT_GROUPS = 8
EXPERTS_PER_GROUP = 8
N_EXPERTS = N_EXPERT_GROUPS * EXPERTS_PER_GROUP
TOP_K = 2
EXPERT_FF = 1408

LANES = 128
BF16_ROWS = 16
MXU_DIM = 256
HD = N_DIL_HEADS * DIL_HEAD_DIM
MLA_PAD = MXU_DIM
Z_DQ, Z_DK, Z_DV = 0, HD, 2 * HD
Z_CQ = 3 * HD
Z_CKV = Z_CQ + Q_LORA_RANK
Z_KR = Z_CKV + KV_LORA_RANK
Z_WIDTH = Z_KR + 2 * MLA_ROPE_DIM

ATT_TK = 512
ATT_TQ = 512
VT_ROWS = MLA_V_DIM + BF16_ROWS
PERM_TILE = 256
LOG2E = math.log2(math.e)

VMEM_LIMIT = 56 * 1024 * 1024

F32 = jnp.float32
BF16 = jnp.bfloat16


def _cparams(sem, vmem=VMEM_LIMIT):
    return pltpu.CompilerParams(dimension_semantics=sem, vmem_limit_bytes=vmem)


def _dot(a, b):
    return jnp.dot(a, b, preferred_element_type=F32)


def _dot_nt(a, b):
    return lax.dot_general(a, b, (((1,), (1,)), ((), ())), preferred_element_type=F32)


def _inproj_kernel(x_ref, g_ref, w_ref, z_ref):
    x = x_ref[...]
    r = lax.rsqrt(jnp.mean(x * x, axis=-1, keepdims=True) + NORM_EPS)
    h = (x * r * g_ref[...]).astype(BF16)
    z_ref[...] = _dot(h, w_ref[...]).astype(BF16)


def _inproj(x2, g1, w_z, tm=256):
    S = x2.shape[0]
    return pl.pallas_call(
        _inproj_kernel,
        grid=(S // tm,),
        in_specs=[
            pl.BlockSpec((tm, D_MODEL), lambda i: (i, 0)),
            pl.BlockSpec((1, D_MODEL), lambda i: (0, 0)),
            pl.BlockSpec((D_MODEL, Z_WIDTH), lambda i: (0, 0)),
        ],
        out_specs=pl.BlockSpec((tm, Z_WIDTH), lambda i: (i, 0)),
        out_shape=jax.ShapeDtypeStruct((S, Z_WIDTH), BF16),
        compiler_params=_cparams(("parallel",)),
        name="inproj",
    )(x2, g1, w_z)


def _prep_kernel(zdq_ref, zdk_ref, zdv_ref, zcq_ref, zckv_ref, zkr_ref,
                 gcq_ref, gckv_ref, wq_ref, wkv_ref, gq_ref, gkn_ref, gkp_ref,
                 gdq_ref, gdk_ref, t64_ref, cos_ref, sin_ref, p4_ref, p16_ref,
                 q_ref, k_ref, vt_ref, dq_ref, dk_ref,
                 dq4_ref, dk4_ref, dv4_ref, dq16_ref, dk16_ref, dv16_ref):
    tm = zcq_ref.shape[0]
    lane = lax.broadcasted_iota(jnp.int32, (1, LANES), 1)
    first_half = lane < MLA_ROPE_DIM

    def rms_rows(c, g):
        c = c.astype(F32)
        r = lax.rsqrt(jnp.mean(c * c, axis=-1, keepdims=True) + NORM_EPS)
        return (c * r * g).astype(BF16)

    cq = rms_rows(zcq_ref[...], gcq_ref[...])
    ckv = rms_rows(zckv_ref[...], gckv_ref[...])
    qe = _dot(cq, wq_ref[...])
    kv = _dot(ckv, wkv_ref[...])
    kr = zkr_ref[...].astype(F32)
    kr_ss = jnp.sum(jnp.where(first_half, kr * kr, 0.0), axis=-1, keepdims=True)
    t64 = t64_ref[...]
    q_scale = MLA_QK_DIM ** -0.5 * LOG2E

    def rope64(ext):
        t = ext * t64
        return jnp.where(first_half, t + pltpu.roll(t, MLA_ROPE_DIM, 1), 0.0)

    ones_rows = (lax.broadcasted_iota(jnp.int32, (BF16_ROWS, tm), 0) == 0).astype(BF16)
    for h in range(N_MLA_HEADS):
        base = h * MLA_PAD
        qn = qe[:, base:base + LANES]
        qp = qe[:, base + LANES:base + 2 * LANES]
        ss = (jnp.sum(qn * qn, axis=-1, keepdims=True)
              + jnp.sum(jnp.where(first_half, qp * qp, 0.0), axis=-1, keepdims=True))
        r = lax.rsqrt(ss * (1.0 / MLA_QK_DIM) + NORM_EPS) * q_scale
        q_ref[:, base:base + LANES] = (qn * r * gq_ref[:, :LANES]).astype(BF16)
        q_ref[:, base + LANES:base + MLA_PAD] = rope64(qp * r * gq_ref[:, LANES:]).astype(BF16)

        kn = kv[:, base:base + LANES]
        ss = jnp.sum(kn * kn, axis=-1, keepdims=True) + kr_ss
        r = lax.rsqrt(ss * (1.0 / MLA_QK_DIM) + NORM_EPS)
        k_ref[:, base:base + LANES] = (kn * r * gkn_ref[...]).astype(BF16)
        k_ref[:, base + LANES:base + MLA_PAD] = rope64(kr * r * gkp_ref[...]).astype(BF16)
        vt = kv[:, base + LANES:base + 2 * LANES].T.astype(BF16)
        vt_ref[h, :MLA_V_DIM, :] = vt
        vt_ref[h, MLA_V_DIM:, :] = ones_rows

    cos = cos_ref[...]
    sin = sin_ref[...]
    d_scale = DIL_HEAD_DIM ** -0.5

    def dil_head(x, g, scale):
        x = x.astype(F32)
        r = lax.rsqrt(jnp.mean(x * x, axis=-1, keepdims=True) + NORM_EPS)
        y = x * r * g
        return ((y * cos + pltpu.roll(y, DIL_HEAD_DIM // 2, 1) * sin) * scale).astype(BF16)

    for h in range(N_DIL_HEADS):
        sl = slice(h * DIL_HEAD_DIM, (h + 1) * DIL_HEAD_DIM)
        dq_ref[:, sl] = dil_head(zdq_ref[:, sl], gdq_ref[...], d_scale)
        dk_ref[:, sl] = dil_head(zdk_ref[:, sl], gdk_ref[...], 1.0)

    for src, d4, d16 in ((dq_ref, dq4_ref, dq16_ref), (dk_ref, dk4_ref, dk16_ref), (zdv_ref, dv4_ref, dv16_ref)):
        for sub in range(tm // PERM_TILE):
            xs = src[sub * PERM_TILE:(sub + 1) * PERM_TILE, :]
            for dil, p_ref, dst in ((4, p4_ref, d4), (16, p16_ref, d16)):
                n = PERM_TILE // dil
                xp = _dot(p_ref[...], xs).astype(BF16)
                for r in range(dil):
                    dst[r, sub * n:(sub + 1) * n, :] = xp[r * n:(r + 1) * n, :]


def _prep(z, gcq, gckv, wq_ext, wkv, gq_ext, gk_nope, gk_pe, gdq, gdk, t64, cos128, sin128, p4, p16):
    S = z.shape[0]
    tm = ATT_TK
    row = lambda w, j: pl.BlockSpec((tm, w), lambda i, j=j: (i, j))
    full = lambda a: pl.BlockSpec(a.shape, lambda i: (0, 0))
    res = lambda dil: pl.BlockSpec((dil, tm // dil, HD), lambda i: (0, i, 0))
    qk_w = N_MLA_HEADS * MLA_PAD
    res_shape = lambda dil: jax.ShapeDtypeStruct((dil, S // dil, HD), BF16)
    return pl.pallas_call(
        _prep_kernel,
        grid=(S // tm,),
        in_specs=[
            row(HD, Z_DQ // HD), row(HD, Z_DK // HD), row(HD, Z_DV // HD),
            row(Q_LORA_RANK, Z_CQ // Q_LORA_RANK), row(KV_LORA_RANK, Z_CKV // KV_LORA_RANK),
            row(LANES, Z_KR // LANES),
            full(gcq), full(gckv), full(wq_ext), full(wkv), full(gq_ext), full(gk_nope), full(gk_pe),
            full(gdq), full(gdk),
            row(LANES, 0), row(LANES, 0), row(LANES, 0), full(p4), full(p16),
        ],
        out_specs=[row(qk_w, 0), row(qk_w, 0),
                   pl.BlockSpec((N_MLA_HEADS, None, VT_ROWS, tm), lambda i: (0, i, 0, 0)),
                   row(HD, 0), row(HD, 0),
                   res(4), res(4), res(4), res(16), res(16), res(16)],
        out_shape=[
            jax.ShapeDtypeStruct((S, qk_w), BF16), jax.ShapeDtypeStruct((S, qk_w), BF16),
            jax.ShapeDtypeStruct((N_MLA_HEADS, S // tm, VT_ROWS, tm), BF16),
            jax.ShapeDtypeStruct((S, HD), BF16), jax.ShapeDtypeStruct((S, HD), BF16),
            res_shape(4), res_shape(4), res_shape(4), res_shape(16), res_shape(16), res_shape(16),
        ],
        compiler_params=_cparams(("parallel",)),
        name="qkv_prep",
    )(z, z, z, z, z, z, gcq, gckv, wq_ext, wkv, gq_ext, gk_nope, gk_pe, gdq, gdk, t64, cos128, sin128, p4, p16)


def _mla_attn_kernel(q_ref, k_ref, vt_ref, o_ref, m_ref, acc_ref, s_ref):
    n_chunks = vt_ref.shape[0]
    q = q_ref[...]

    def scores(c):
        off = pl.multiple_of(c * ATT_TK, ATT_TK)
        return _dot_nt(k_ref[pl.ds(off, ATT_TK), :], q)

    m_ref[...] = jnp.full(m_ref.shape, -jnp.inf, F32)
    acc_ref[...] = jnp.zeros(acc_ref.shape, F32)

    def fold(c, slot):
        s = s_ref[slot]
        m_old = m_ref[...]
        m_new = jnp.maximum(m_old, jnp.max(s, axis=0, keepdims=True))
        alpha = jnp.exp2(m_old - m_new)
        p = jnp.exp2(s - m_new).astype(BF16)
        acc_ref[...] = alpha * acc_ref[...] + _dot(vt_ref[c], p)
        m_ref[...] = m_new

    s_ref[0] = scores(0)

    def body(t, carry):
        c = 2 * t
        s_ref[1] = scores(c + 1)
        fold(c, 0)
        s_ref[0] = scores(jnp.minimum(c + 2, n_chunks - 1))
        fold(c + 1, 1)
        return carry

    lax.fori_loop(0, n_chunks // 2, body, 0)
    acc = acc_ref[...]
    o_t = acc[:MLA_V_DIM, :] / acc[MLA_V_DIM:MLA_V_DIM + 1, :]
    o_ref[...] = o_t.T.astype(o_ref.dtype)


def _mla_attn(q, k, vt):
    S = q.shape[0]
    n_chunks = vt.shape[1]
    return pl.pallas_call(
        _mla_attn_kernel,
        grid=(N_MLA_HEADS, S // ATT_TQ),
        in_specs=[
            pl.BlockSpec((ATT_TQ, MLA_PAD), lambda h, i: (i, h)),
            pl.BlockSpec((S, MLA_PAD), lambda h, i: (0, h)),
            pl.BlockSpec((None, n_chunks, VT_ROWS, ATT_TK), lambda h, i: (h, 0, 0, 0)),
        ],
        out_specs=pl.BlockSpec((ATT_TQ, MLA_V_DIM), lambda h, i: (i, h)),
        out_shape=jax.ShapeDtypeStruct((S, N_MLA_HEADS * MLA_V_DIM), BF16),
        scratch_shapes=[pltpu.VMEM((1, ATT_TQ), F32), pltpu.VMEM((VT_ROWS, ATT_TQ), F32),
                        pltpu.VMEM((2, ATT_TK, ATT_TQ), F32)],
        compiler_params=_cparams(("parallel", "parallel")),
        name="mla_attn",
    )(q, k, vt)


DIL_QB = 128


def _dil_kernel(q_ref, k_ref, v_ref, o_ref, lse_ref, *, half_w, hb, gb):
    L = q_ref.shape[0]
    kw = DIL_QB + 2 * half_w
    g_id = pl.program_id(1)
    lane = lax.broadcasted_iota(jnp.int32, (DIL_QB, LANES), 1)

    @pl.when(g_id == 0)
    def _():
        lse_ref[...] = jnp.zeros(lse_ref.shape, F32)

    def body(step, carry):
        qs_l, ks_l, qb, kb, vb = [], [], [], [], []
        for j in range(gb):
            qs = pl.multiple_of((step * gb + j) * DIL_QB, DIL_QB)
            ks = pl.multiple_of(jnp.clip(qs - half_w, 0, L - kw), half_w)
            qs_l.append(qs)
            ks_l.append(ks)
            for h in range(hb):
                sl = slice(h * DIL_HEAD_DIM, (h + 1) * DIL_HEAD_DIM)
                qb.append(q_ref[pl.ds(qs, DIL_QB), sl])
                kb.append(k_ref[pl.ds(ks, kw), sl])
                vb.append(v_ref[pl.ds(ks, kw), sl])
        q = jnp.stack(qb)
        k = jnp.stack(kb)
        v = jnp.stack(vb)
        s = jnp.einsum("gqd,gkd->gqk", q, k, preferred_element_type=F32)
        rel = (lax.broadcasted_iota(jnp.int32, (DIL_QB, kw), 0)
               - lax.broadcasted_iota(jnp.int32, (DIL_QB, kw), 1))
        bias = []
        for j in range(gb):
            mask = jnp.abs(rel + (qs_l[j] - ks_l[j])) <= half_w
            bias += [jnp.where(mask, 0.0, NEG_INF)] * hb
        s = s + jnp.stack(bias)
        m = jnp.max(s, axis=-1, keepdims=True)
        p = jnp.exp(s - m)
        den = jnp.sum(p, axis=-1, keepdims=True)
        o = jnp.einsum("gqk,gkd->gqd", p.astype(BF16), v, preferred_element_type=F32) / den
        lse = m + jnp.log(den)
        for j in range(gb):
            tile = lse_ref[pl.ds(qs_l[j], DIL_QB), :]
            for h in range(hb):
                sl = slice(h * DIL_HEAD_DIM, (h + 1) * DIL_HEAD_DIM)
                o_ref[pl.ds(qs_l[j], DIL_QB), sl] = o[j * hb + h].astype(o_ref.dtype)
                tile = jnp.where(lane == g_id * hb + h, lse[j * hb + h], tile)
            lse_ref[pl.ds(qs_l[j], DIL_QB), :] = tile
        return carry

    lax.fori_loop(0, L // (DIL_QB * gb), body, 0)


def _dil_pattern(dq, dk, dv, v_col0, window, dil, hb, gb):
    L = dq.shape[1]
    half_w = window // (2 * dil)
    w = hb * DIL_HEAD_DIM
    spec = lambda c0: pl.BlockSpec((None, L, w), lambda r, g, c0=c0: (r, 0, c0 + g))
    return pl.pallas_call(
        functools.partial(_dil_kernel, half_w=half_w, hb=hb, gb=gb),
        grid=(dil, N_DIL_HEADS // hb),
        in_specs=[spec(0), spec(0), spec(v_col0 // w)],
        out_specs=[spec(0), pl.BlockSpec((None, L, LANES), lambda r, g: (r, 0, 0))],
        out_shape=[jax.ShapeDtypeStruct((dil, L, HD), BF16), jax.ShapeDtypeStruct((dil, L, LANES), F32)],
        compiler_params=_cparams(("parallel", "arbitrary")),
        name=f"dil_attn_d{dil}",
    )(dq, dk, dv)


def _outproj_kernel(a_ref, o1_ref, l1_ref, o4_ref, l4_ref, o16_ref, l16_ref, p4t_ref, p16t_ref,
                    x_ref, wa_ref, wb_ref, g_ref, wr_ref, wrh_ref, br_ref,
                    x1_ref, xn_ref, ri_ref, rf_ref):
    tm = x_ref.shape[0]

    def to_token_order(o_ref, l_ref, pt_ref):
        pt = pt_ref[...]
        o = _dot(pt, o_ref[...].reshape(tm, HD))
        lse = l_ref[...].reshape(tm, LANES)
        hi = lse.astype(BF16)
        rem = lse - hi.astype(F32)
        mid = rem.astype(BF16)
        lo = (rem - mid.astype(F32)).astype(BF16)
        return o, _dot(pt, hi) + _dot(pt, mid) + _dot(pt, lo)

    o1 = o1_ref[...].astype(F32)
    l1 = l1_ref[...]
    o4, l4 = to_token_order(o4_ref, l4_ref, p4t_ref)
    o16, l16 = to_token_order(o16_ref, l16_ref, p16t_ref)
    big = jnp.maximum(jnp.maximum(l1, l4), l16)
    e1 = jnp.exp(l1 - big)
    e4 = jnp.exp(l4 - big)
    e16 = jnp.exp(l16 - big)
    inv = 1.0 / (e1 + e4 + e16)
    w1, w4, w16 = e1 * inv, e4 * inv, e16 * inv
    slabs = []
    for h in range(N_DIL_HEADS):
        sl = slice(h * DIL_HEAD_DIM, (h + 1) * DIL_HEAD_DIM)
        slabs.append(w1[:, h:h + 1] * o1[:, sl] + w4[:, h:h + 1] * o4[:, sl] + w16[:, h:h + 1] * o16[:, sl])
    dil_o = jnp.concatenate(slabs, axis=-1).astype(BF16)

    x1 = x_ref[...] + _dot(a_ref[...], wa_ref[...]) + _dot(dil_o, wb_ref[...])
    x1_ref[...] = x1
    r = lax.rsqrt(jnp.mean(x1 * x1, axis=-1, keepdims=True) + NORM_EPS)
    xn = x1 * r * g_ref[...]
    xn_ref[...] = xn
    xh = xn.astype(BF16)
    xl = (xn - xh.astype(F32)).astype(BF16)
    two = _dot(xh, wr_ref[...])
    logits = two[:, :LANES] + two[:, LANES:] + _dot(xl, wrh_ref[...]) + br_ref[...]
    lane = lax.broadcasted_iota(jnp.int32, logits.shape, 1)
    ninf = -jnp.inf

    def first_argmax(vals, vmax):
        return jnp.min(jnp.where(vals == vmax, lane, LANES), axis=-1, keepdims=True)

    coarse = jnp.where(lane < N_EXPERT_GROUPS, logits, ninf)
    cmax = jnp.max(coarse, axis=-1, keepdims=True)
    g = first_argmax(coarse, cmax)
    p_g = 1.0 / jnp.sum(jnp.exp(coarse - cmax), axis=-1, keepdims=True)
    lo_lane = N_EXPERT_GROUPS + g * EXPERTS_PER_GROUP
    fine = jnp.where((lane >= lo_lane) & (lane < lo_lane + EXPERTS_PER_GROUP), logits, ninf)
    v1 = jnp.max(fine, axis=-1, keepdims=True)
    j1 = first_argmax(fine, v1)
    fine2 = jnp.where(lane == j1, ninf, fine)
    v2 = jnp.max(fine2, axis=-1, keepdims=True)
    j2 = first_argmax(fine2, v2)
    e2 = jnp.exp(v2 - v1)
    g1 = 1.0 / (1.0 + e2)
    g2 = e2 / (1.0 + e2)
    ri_ref[...] = jnp.where(lane == 0, j1 - N_EXPERT_GROUPS, jnp.where(lane == 1, j2 - N_EXPERT_GROUPS, 0))
    rf_ref[...] = jnp.where(lane == 0, p_g * g1, jnp.where(lane == 1, p_g * g2, 0.0))


def _outproj_router(mla_o, dil1, dil4, dil16, p4t, p16t, x2, w_out_bf, g2, wr_two, wr_hi, b_router):
    S = x2.shape[0]
    tm = PERM_TILE
    half = N_MLA_HEADS * MLA_V_DIM
    row = lambda w: pl.BlockSpec((tm, w), lambda i: (i, 0))
    res = lambda dil, w: pl.BlockSpec((dil, tm // dil, w), lambda i: (0, i, 0))
    const = lambda a: pl.BlockSpec(a.shape, lambda i: (0, 0))
    return pl.pallas_call(
        _outproj_kernel,
        grid=(S // tm,),
        in_specs=[
            row(half),
            pl.BlockSpec((None, tm, HD), lambda i: (0, i, 0)), pl.BlockSpec((None, tm, LANES), lambda i: (0, i, 0)),
            res(4, HD), res(4, LANES), res(16, HD), res(16, LANES), const(p4t), const(p16t),
            row(D_MODEL),
            pl.BlockSpec((half, D_MODEL), lambda i: (0, 0)),
            pl.BlockSpec((HD, D_MODEL), lambda i: (1, 0)),
            const(g2), const(wr_two), const(wr_hi), const(b_router),
        ],
        out_specs=[row(D_MODEL), row(D_MODEL), row(LANES), row(LANES)],
        out_shape=[
            jax.ShapeDtypeStruct((S, D_MODEL), F32), jax.ShapeDtypeStruct((S, D_MODEL), F32),
            jax.ShapeDtypeStruct((S, LANES), jnp.int32), jax.ShapeDtypeStruct((S, LANES), F32),
        ],
        compiler_params=_cparams(("parallel",)),
        name="outproj_router",
    )(mla_o, dil1[0], dil1[1], dil4[0], dil4[1], dil16[0], dil16[1], p4t, p16t,
      x2, w_out_bf, w_out_bf, g2, wr_two, wr_hi, b_router)


MOE_CAP = 512
MOE_RT = 256
MOE_RH = 128
MOE_FC = 256
MOE_NF = EXPERT_FF // MOE_FC
MOE_TAIL = EXPERT_FF - MOE_NF * MOE_FC
assert MOE_TAIL == LANES


def _moe_kernel(ie_ref, is_ref, in_ref, ord_ref,
                xn_hbm, w1_ref, w3_ref, w2_ref, w1t_ref, w3t_ref, w2t_ref,
                y_hbm, xf_ref, xb_ref, acc_ref, sem_in, sem_out):
    i = pl.program_id(0)
    j = pl.program_id(1)
    n_items = pl.num_programs(0)
    n = in_ref[i]
    slot = lax.rem(i, 2)

    def row_in(item, sl, r):
        tok = lax.shift_right_logical(ord_ref[is_ref[item] + r], 1)
        return pltpu.make_async_copy(xn_hbm.at[pl.ds(tok, 1), :], xf_ref.at[sl, pl.ds(r, 1), :], sem_in.at[sl])

    def row_out(item, r):
        dst = ord_ref[is_ref[item] + r]
        return pltpu.make_async_copy(acc_ref.at[pl.ds(r, 1), :], y_hbm.at[pl.ds(dst, 1), :], sem_out)

    def for_rows(count, fn):
        def step(r, c):
            fn(r)
            return c
        lax.fori_loop(0, count, step, 0)

    @pl.when((i == 0) & (j == 0))
    def _():
        xf_ref[...] = jnp.zeros(xf_ref.shape, F32)
        for_rows(n, lambda r: row_in(0, 0, r).start())

    @pl.when((n > 0) & (j == 0))
    def _():
        for_rows(n, lambda r: row_in(i, slot, r).wait())

        @pl.when(i > 0)
        def _():
            for_rows(in_ref[i - 1], lambda r: row_out(i - 1, r).wait())

        for t in range(MOE_CAP // MOE_RH):
            @pl.when(t * MOE_RH < n)
            def _():
                sl = pl.ds(t * MOE_RH, MOE_RH)
                xb_ref[sl, :] = xf_ref[slot, sl, :].astype(BF16)
                acc_ref[sl, :] = jnp.zeros((MOE_RH, D_MODEL), F32)

    @pl.when((n > 0) & (j == 1) & (i + 1 < n_items))
    def _():
        nxt = jnp.minimum(i + 1, n_items - 1)
        for_rows(in_ref[nxt], lambda r: row_in(nxt, 1 - slot, r).start())

    def chunk(w1, w3, w2):
        w1 = w1.astype(BF16)
        w3 = w3.astype(BF16)
        w2 = w2.astype(BF16)

        def rows(off, size):
            xs = xb_ref[pl.ds(off, size), :]
            a = _dot(xs, w1)
            b = _dot(xs, w3)
            hmid = (a * jax.nn.sigmoid(a) * b).astype(BF16)
            acc_ref[pl.ds(off, size), :] += _dot(hmid, w2)

        n_full = n // MOE_RT

        def full(t, c):
            rows(pl.multiple_of(t * MOE_RT, MOE_RT), MOE_RT)
            return c
        lax.fori_loop(0, n_full, full, 0)
        rem = n - n_full * MOE_RT
        base = pl.multiple_of(n_full * MOE_RT, MOE_RT)

        @pl.when(rem > 0)
        def _():
            rows(base, MOE_RH)

        @pl.when(rem > MOE_RH)
        def _():
            rows(base + MOE_RH, MOE_RH)

    @pl.when(n > 0)
    def _():
        chunk(w1_ref[...], w3_ref[...], w2_ref[...])

    @pl.when((n > 0) & (j == MOE_NF - 1))
    def _():
        chunk(w1t_ref[...], w3t_ref[...], w2t_ref[...])
        for_rows(n, lambda r: row_out(i, r).start())
        nxt = jnp.minimum(i + 1, n_items - 1)

        @pl.when((i == n_items - 1) | (in_ref[nxt] == 0))
        def _():
            for_rows(n, lambda r: row_out(i, r).wait())


def _moe_experts(xn, w1, w3, w2, item_e, item_start, item_n, order):
    S = xn.shape[0]
    A = S * TOP_K
    n_items = item_e.shape[0]
    tail_blk = (MOE_NF * MOE_FC) // MOE_TAIL

    def jx(i, j, n_ref):
        return jnp.where(n_ref[i] > 0, j, MOE_NF - 1)

    up = pl.BlockSpec((None, D_MODEL, MOE_FC), lambda i, j, ie, is_, n_, o: (ie[i], 0, jx(i, j, n_)))
    down = pl.BlockSpec((None, MOE_FC, D_MODEL), lambda i, j, ie, is_, n_, o: (ie[i], jx(i, j, n_), 0))
    up_t = pl.BlockSpec((None, D_MODEL, MOE_TAIL), lambda i, j, ie, is_, n_, o: (ie[i], 0, tail_blk))
    down_t = pl.BlockSpec((None, MOE_TAIL, D_MODEL), lambda i, j, ie, is_, n_, o: (ie[i], tail_blk, 0))
    grid_spec = pltpu.PrefetchScalarGridSpec(
        num_scalar_prefetch=4,
        grid=(n_items, MOE_NF),
        in_specs=[pl.BlockSpec(memory_space=pl.ANY), up, up, down, up_t, up_t, down_t],
        out_specs=pl.BlockSpec(memory_space=pl.ANY),
        scratch_shapes=[
            pltpu.VMEM((2, MOE_CAP, D_MODEL), F32),
            pltpu.VMEM((MOE_CAP, D_MODEL), BF16),
            pltpu.VMEM((MOE_CAP, D_MODEL), F32),
            pltpu.SemaphoreType.DMA((2,)),
            pltpu.SemaphoreType.DMA(()),
        ],
    )
    return pl.pallas_call(
        _moe_kernel,
        grid_spec=grid_spec,
        out_shape=jax.ShapeDtypeStruct((A, D_MODEL), F32),
        compiler_params=_cparams(("arbitrary", "arbitrary")),
        name="moe_experts",
    )(item_e, item_start, item_n, order, xn, w1, w3, w2, w1, w3, w2)


def _moe_items(eid, cap):
    A = eid.size
    flat_e = eid.reshape(A)
    order = jnp.argsort(flat_e, stable=True).astype(jnp.int32)
    experts = jnp.arange(N_EXPERTS + 1, dtype=jnp.int32)
    starts = jnp.sum((flat_e[None, :] < experts[:, None]).astype(jnp.int32), axis=1)
    counts = starts[1:] - starts[:-1]
    per_e = (counts + cap - 1) // cap
    item_end = jnp.cumsum(per_e)
    total = item_end[-1]
    n_items = N_EXPERTS + A // cap
    idx = jnp.arange(n_items, dtype=jnp.int32)
    clamped = jnp.minimum(idx, total - 1)
    e = jnp.minimum(jnp.searchsorted(item_end, clamped, side="right"), N_EXPERTS - 1).astype(jnp.int32)
    local = clamped - (item_end[e] - per_e[e])
    used = idx < total
    item_start = jnp.where(used, starts[e] + local * cap, 0).astype(jnp.int32)
    item_n = jnp.where(used, jnp.clip(counts[e] - local * cap, 0, cap), 0).astype(jnp.int32)
    return e, item_start, item_n, order


def _combine_kernel(x1_ref, y_ref, g_ref, o_ref):
    g = g_ref[...]
    y = y_ref[...]
    o_ref[...] = x1_ref[...] + (g[:, 0:1] * y[:, :D_MODEL] + g[:, 1:2] * y[:, D_MODEL:])


def _combine(x1, y, gates, tm=256):
    S = x1.shape[0]
    return pl.pallas_call(
        _combine_kernel,
        grid=(S // tm,),
        in_specs=[
            pl.BlockSpec((tm, D_MODEL), lambda i: (i, 0)),
            pl.BlockSpec((tm, TOP_K * D_MODEL), lambda i: (i, 0)),
            pl.BlockSpec((tm, LANES), lambda i: (i, 0)),
        ],
        out_specs=pl.BlockSpec((tm, D_MODEL), lambda i: (i, 0)),
        out_shape=jax.ShapeDtypeStruct((S, D_MODEL), F32),
        compiler_params=_cparams(("parallel",)),
        name="moe_combine",
    )(x1, y.reshape(S, TOP_K * D_MODEL), gates)


def _rot_half_cols(w, half):
    return jnp.concatenate([-w[..., half:], w[..., :half]], axis=-1)


def _rope_tables(S):
    pos = jnp.arange(S, dtype=F32)[:, None]

    def cs(half):
        inv = ROPE_THETA ** (-jnp.arange(half, dtype=F32) / half)
        ang = pos * inv[None, :]
        return jnp.cos(ang), jnp.sin(ang)

    c32, s32 = cs(MLA_ROPE_DIM // 2)
    t64 = jnp.concatenate([c32, c32, s32, s32], axis=-1)
    c64, s64 = cs(DIL_HEAD_DIM // 2)
    cos128 = jnp.concatenate([c64, c64], axis=-1)
    sin128 = jnp.concatenate([-s64, s64], axis=-1)
    return t64, cos128, sin128


def _residue_perm(dil):
    n = PERM_TILE // dil
    rows = jnp.arange(PERM_TILE)
    src = (rows % n) * dil + rows // n
    return (src[:, None] == jnp.arange(PERM_TILE)[None, :]).astype(BF16)


def kernel(x, norm1_g, w_in, g_cq, g_ckv, w_uq, w_ukv, mla_q_norm_g, mla_k_norm_g, dil_q_norm_g,
           dil_k_norm_g, w_out, norm2_g, w_group, b_group, w_expert, b_expert, w1, w3, w2):
    B, S, D = x.shape
    assert B == 1 and D == D_MODEL and norm1_g.shape[0] == 1
    x2 = x.reshape(S, D)
    half_r = MLA_ROPE_DIM // 2
    o1, o2 = Q_LORA_RANK, Q_LORA_RANK + KV_LORA_RANK
    o3 = o2 + MLA_ROPE_DIM

    wi = w_in[0]
    w_kr = wi[:, o2:o3]
    w_z = jnp.concatenate([wi[:, o3:], wi[:, :o2], w_kr, _rot_half_cols(w_kr, half_r)], axis=-1).astype(BF16)

    wq = w_uq[0].reshape(Q_LORA_RANK, N_MLA_HEADS, MLA_QK_DIM)
    wq_pe = wq[..., MLA_NOPE_DIM:]
    wq_ext = jnp.concatenate([wq, _rot_half_cols(wq_pe, half_r)], axis=-1)
    wq_ext = wq_ext.reshape(Q_LORA_RANK, N_MLA_HEADS * MLA_PAD).astype(BF16)
    wkv = w_ukv[0].astype(BF16)

    def ext_gain(g):
        pe = g[MLA_NOPE_DIM:]
        return jnp.concatenate([g, pe[half_r:], pe[:half_r]])[None, :]

    gq_ext = ext_gain(mla_q_norm_g[0])
    gk_ext = ext_gain(mla_k_norm_g[0])
    t64, cos128, sin128 = _rope_tables(S)
    p4, p16 = _residue_perm(4), _residue_perm(16)

    z = _inproj(x2, norm1_g, w_z)
    q, k, vt, dq, dk, dq4, dk4, dv4, dq16, dk16, dv16 = _prep(
        z, g_cq, g_ckv, wq_ext, wkv, gq_ext, gk_ext[:, :LANES], gk_ext[:, LANES:],
        dil_q_norm_g, dil_k_norm_g, t64, cos128, sin128, p4, p16)
    mla_o = _mla_attn(q, k, vt)

    (w_1, d_1), (w_4, d_4), (w_16, d_16) = DIL_PATTERNS
    dil1 = _dil_pattern(dq[None], dk[None], z[None], Z_DV, w_1, d_1, hb=2, gb=4)
    dil4 = _dil_pattern(dq4, dk4, dv4, 0, w_4, d_4, hb=8, gb=1)
    dil16 = _dil_pattern(dq16, dk16, dv16, 0, w_16, d_16, hb=8, gb=1)

    pad = LANES - N_EXPERT_GROUPS - N_EXPERTS
    w_router = jnp.concatenate([w_group[0], w_expert[0], jnp.zeros((D, pad), F32)], axis=-1)
    b_router = jnp.concatenate([b_group[0], b_expert[0], jnp.zeros((pad,), F32)])[None, :]
    wr_hi = w_router.astype(BF16)
    wr_lo = (w_router - wr_hi.astype(F32)).astype(BF16)
    x1, xn, route_i, route_f = _outproj_router(
        mla_o, dil1, dil4, dil16, p4.T, p16.T, x2, w_out[0].astype(BF16), norm2_g,
        jnp.concatenate([wr_hi, wr_lo], axis=-1), wr_hi, b_router)

    item_e, item_start, item_n, order = _moe_items(route_i[:, :TOP_K], MOE_CAP)
    y = _moe_experts(xn, w1[0], w3[0], w2[0], item_e, item_start, item_n, order)
    out = _combine(x1, y, route_f)
    return out.reshape(B, S, D)
```

```python
import functools
import math

import jax
import jax.numpy as jnp
from jax import lax
from jax.experimental import pallas as pl
from jax.experimental.pallas import tpu as pltpu

D_MODEL = 2048
N_MLA_HEADS = 8
MLA_NOPE_DIM = 128
MLA_ROPE_DIM = 64
MLA_QK_DIM = MLA_NOPE_DIM + MLA_ROPE_DIM
MLA_V_DIM = 128
Q_LORA_RANK = 512
KV_LORA_RANK = 512
N_DIL_HEADS = 8
DIL_HEAD_DIM = 128
DIL_PATTERNS = ((128, 1), (512, 4), (2048, 16))
ROPE_THETA = 10000.0
NORM_EPS = 1e-6
NEG_INF = -1e30
N_EXPERT_GROUPS = 8
EXPERTS_PER_GROUP = 8
N_EXPERTS = N_EXPERT_GROUPS * EXPERTS_PER_GROUP
TOP_K = 2
EXPERT_FF = 1408

LANES = 128
BF16_ROWS = 16
MXU_DIM = 256
HD = N_DIL_HEADS * DIL_HEAD_DIM
MLA_PAD = MXU_DIM
Z_DQ, Z_DK, Z_DV = 0, HD, 2 * HD
Z_CQ = 3 * HD
Z_CKV = Z_CQ + Q_LORA_RANK
Z_KR = Z_CKV + KV_LORA_RANK
Z_WIDTH = Z_KR + 2 * MLA_ROPE_DIM

ATT_TK = 512
ATT_TQ = 1024
VT_ROWS = MLA_V_DIM + BF16_ROWS
PERM_TILE = 256
LOG2E = math.log2(math.e)

VMEM_LIMIT = 56 * 1024 * 1024

F32 = jnp.float32
BF16 = jnp.bfloat16


def _cparams(sem, vmem=VMEM_LIMIT):
    return pltpu.CompilerParams(dimension_semantics=sem, vmem_limit_bytes=vmem)


def _dot(a, b):
    return jnp.dot(a, b, preferred_element_type=F32)


def _dot_nt(a, b):
    return lax.dot_general(a, b, (((1,), (1,)), ((), ())), preferred_element_type=F32)


def _inproj_kernel(x_ref, g_ref, w_ref, z_ref):
    x = x_ref[...]
    r = lax.rsqrt(jnp.mean(x * x, axis=-1, keepdims=True) + NORM_EPS)
    h = (x * r * g_ref[...]).astype(BF16)
    z_ref[...] = _dot(h, w_ref[...]).astype(BF16)


def _inproj(x2, g1, w_z, tm=256):
    S = x2.shape[0]
    return pl.pallas_call(
        _inproj_kernel,
        grid=(S // tm,),
        in_specs=[
            pl.BlockSpec((tm, D_MODEL), lambda i: (i, 0)),
            pl.BlockSpec((1, D_MODEL), lambda i: (0, 0)),
            pl.BlockSpec((D_MODEL, Z_WIDTH), lambda i: (0, 0)),
        ],
        out_specs=pl.BlockSpec((tm, Z_WIDTH), lambda i: (i, 0)),
        out_shape=jax.ShapeDtypeStruct((S, Z_WIDTH), BF16),
        compiler_params=_cparams(("parallel",)),
        name="inproj",
    )(x2, g1, w_z)


def _prep_kernel(zdq_ref, zdk_ref, zdv_ref, zcq_ref, zckv_ref, zkr_ref,
                 gcq_ref, gckv_ref, wq_ref, wkv_ref, gq_ref, gkn_ref, gkp_ref,
                 gdq_ref, gdk_ref, t64_ref, cos_ref, sin_ref, p4_ref, p16_ref,
                 q_ref, k_ref, vt_ref, dq_ref, dk_ref,
                 dq4_ref, dk4_ref, dv4_ref, dq16_ref, dk16_ref, dv16_ref):
    tm = zcq_ref.shape[0]
    lane = lax.broadcasted_iota(jnp.int32, (1, LANES), 1)
    first_half = lane < MLA_ROPE_DIM

    def rms_rows(c, g):
        c = c.astype(F32)
        r = lax.rsqrt(jnp.mean(c * c, axis=-1, keepdims=True) + NORM_EPS)
        return (c * r * g).astype(BF16)

    cq = rms_rows(zcq_ref[...], gcq_ref[...])
    ckv = rms_rows(zckv_ref[...], gckv_ref[...])
    qe = _dot(cq, wq_ref[...])
    kv = _dot(ckv, wkv_ref[...])
    kr = zkr_ref[...].astype(F32)
    kr_ss = jnp.sum(jnp.where(first_half, kr * kr, 0.0), axis=-1, keepdims=True)
    t64 = t64_ref[...]
    q_scale = MLA_QK_DIM ** -0.5 * LOG2E

    def rope64(ext):
        t = ext * t64
        return jnp.where(first_half, t + pltpu.roll(t, MLA_ROPE_DIM, 1), 0.0)

    ones_rows = (lax.broadcasted_iota(jnp.int32, (BF16_ROWS, tm), 0) == 0).astype(BF16)
    for h in range(N_MLA_HEADS):
        base = h * MLA_PAD
        qn = qe[:, base:base + LANES]
        qp = qe[:, base + LANES:base + 2 * LANES]
        ss = (jnp.sum(qn * qn, axis=-1, keepdims=True)
              + jnp.sum(jnp.where(first_half, qp * qp, 0.0), axis=-1, keepdims=True))
        r = lax.rsqrt(ss * (1.0 / MLA_QK_DIM) + NORM_EPS) * q_scale
        q_ref[:, base:base + LANES] = (qn * r * gq_ref[:, :LANES]).astype(BF16)
        q_ref[:, base + LANES:base + MLA_PAD] = rope64(qp * r * gq_ref[:, LANES:]).astype(BF16)

        kn = kv[:, base:base + LANES]
        ss = jnp.sum(kn * kn, axis=-1, keepdims=True) + kr_ss
        r = lax.rsqrt(ss * (1.0 / MLA_QK_DIM) + NORM_EPS)
        k_ref[:, base:base + LANES] = (kn * r * gkn_ref[...]).astype(BF16)
        k_ref[:, base + LANES:base + MLA_PAD] = rope64(kr * r * gkp_ref[...]).astype(BF16)
        vt = kv[:, base + LANES:base + 2 * LANES].T.astype(BF16)
        vt_ref[h, :MLA_V_DIM, :] = vt
        vt_ref[h, MLA_V_DIM:, :] = ones_rows

    cos = cos_ref[...]
    sin = sin_ref[...]
    d_scale = DIL_HEAD_DIM ** -0.5

    def dil_head(x, g, scale):
        x = x.astype(F32)
        r = lax.rsqrt(jnp.mean(x * x, axis=-1, keepdims=True) + NORM_EPS)
        y = x * r * g
        return ((y * cos + pltpu.roll(y, DIL_HEAD_DIM // 2, 1) * sin) * scale).astype(BF16)

    for h in range(N_DIL_HEADS):
        sl = slice(h * DIL_HEAD_DIM, (h + 1) * DIL_HEAD_DIM)
        dq_ref[:, sl] = dil_head(zdq_ref[:, sl], gdq_ref[...], d_scale)
        dk_ref[:, sl] = dil_head(zdk_ref[:, sl], gdk_ref[...], 1.0)

    for src, d4, d16 in ((dq_ref, dq4_ref, dq16_ref), (dk_ref, dk4_ref, dk16_ref), (zdv_ref, dv4_ref, dv16_ref)):
        for sub in range(tm // PERM_TILE):
            xs = src[sub * PERM_TILE:(sub + 1) * PERM_TILE, :]
            for dil, p_ref, dst in ((4, p4_ref, d4), (16, p16_ref, d16)):
                n = PERM_TILE // dil
                xp = _dot(p_ref[...], xs).astype(BF16)
                for r in range(dil):
                    dst[r, sub * n:(sub + 1) * n, :] = xp[r * n:(r + 1) * n, :]


def _prep(z, gcq, gckv, wq_ext, wkv, gq_ext, gk_nope, gk_pe, gdq, gdk, t64, cos128, sin128, p4, p16):
    S = z.shape[0]
    tm = ATT_TK
    row = lambda w, j: pl.BlockSpec((tm, w), lambda i, j=j: (i, j))
    full = lambda a: pl.BlockSpec(a.shape, lambda i: (0, 0))
    res = lambda dil: pl.BlockSpec((dil, tm // dil, HD), lambda i: (0, i, 0))
    qk_w = N_MLA_HEADS * MLA_PAD
    res_shape = lambda dil: jax.ShapeDtypeStruct((dil, S // dil, HD), BF16)
    return pl.pallas_call(
        _prep_kernel,
        grid=(S // tm,),
        in_specs=[
            row(HD, Z_DQ // HD), row(HD, Z_DK // HD), row(HD, Z_DV // HD),
            row(Q_LORA_RANK, Z_CQ // Q_LORA_RANK), row(KV_LORA_RANK, Z_CKV // KV_LORA_RANK),
            row(LANES, Z_KR // LANES),
            full(gcq), full(gckv), full(wq_ext), full(wkv), full(gq_ext), full(gk_nope), full(gk_pe),
            full(gdq), full(gdk),
            row(LANES, 0), row(LANES, 0), row(LANES, 0), full(p4), full(p16),
        ],
        out_specs=[row(qk_w, 0), row(qk_w, 0),
                   pl.BlockSpec((N_MLA_HEADS, None, VT_ROWS, tm), lambda i: (0, i, 0, 0)),
                   row(HD, 0), row(HD, 0),
                   res(4), res(4), res(4), res(16), res(16), res(16)],
        out_shape=[
            jax.ShapeDtypeStruct((S, qk_w), BF16), jax.ShapeDtypeStruct((S, qk_w), BF16),
            jax.ShapeDtypeStruct((N_MLA_HEADS, S // tm, VT_ROWS, tm), BF16),
            jax.ShapeDtypeStruct((S, HD), BF16), jax.ShapeDtypeStruct((S, HD), BF16),
            res_shape(4), res_shape(4), res_shape(4), res_shape(16), res_shape(16), res_shape(16),
        ],
        compiler_params=_cparams(("parallel",)),
        name="qkv_prep",
    )(z, z, z, z, z, z, gcq, gckv, wq_ext, wkv, gq_ext, gk_nope, gk_pe, gdq, gdk, t64, cos128, sin128, p4, p16)


def _mla_attn_kernel(q_ref, k_ref, vt_ref, o_ref, m_ref, acc_ref, s_ref):
    n_chunks = vt_ref.shape[0]
    q = q_ref[...]

    def scores(c):
        off = pl.multiple_of(c * ATT_TK, ATT_TK)
        return _dot_nt(k_ref[pl.ds(off, ATT_TK), :], q)

    m_ref[...] = jnp.full(m_ref.shape, -jnp.inf, F32)
    acc_ref[...] = jnp.zeros(acc_ref.shape, F32)

    def fold(c, slot):
        s = s_ref[slot]
        m_old = m_ref[...]
        m_new = jnp.maximum(m_old, jnp.max(s, axis=0, keepdims=True))
        alpha = jnp.exp2(m_old - m_new)
        p = jnp.exp2(s - m_new).astype(BF16)
        acc_ref[...] = alpha * acc_ref[...] + _dot(vt_ref[c], p)
        m_ref[...] = m_new

    s_ref[0] = scores(0)

    def body(t, carry):
        c = 2 * t
        s_ref[1] = scores(c + 1)
        fold(c, 0)
        s_ref[0] = scores(jnp.minimum(c + 2, n_chunks - 1))
        fold(c + 1, 1)
        return carry

    lax.fori_loop(0, n_chunks // 2, body, 0)
    acc = acc_ref[...]
    o_t = acc[:MLA_V_DIM, :] / acc[MLA_V_DIM:MLA_V_DIM + 1, :]
    o_ref[...] = o_t.T.astype(o_ref.dtype)


def _mla_attn(q, k, vt):
    S = q.shape[0]
    n_chunks = vt.shape[1]
    return pl.pallas_call(
        _mla_attn_kernel,
        grid=(N_MLA_HEADS, S // ATT_TQ),
        in_specs=[
            pl.BlockSpec((ATT_TQ, MLA_PAD), lambda h, i: (i, h)),
            pl.BlockSpec((S, MLA_PAD), lambda h, i: (0, h)),
            pl.BlockSpec((None, n_chunks, VT_ROWS, ATT_TK), lambda h, i: (h, 0, 0, 0)),
        ],
        out_specs=pl.BlockSpec((ATT_TQ, MLA_V_DIM), lambda h, i: (i, h)),
        out_shape=jax.ShapeDtypeStruct((S, N_MLA_HEADS * MLA_V_DIM), BF16),
        scratch_shapes=[pltpu.VMEM((1, ATT_TQ), F32), pltpu.VMEM((VT_ROWS, ATT_TQ), F32),
                        pltpu.VMEM((2, ATT_TK, ATT_TQ), F32)],
        compiler_params=_cparams(("parallel", "parallel")),
        name="mla_attn",
    )(q, k, vt)


DIL_QB = 128


def _dil_kernel(q_ref, k_ref, v_ref, o_ref, lse_ref, *, half_w, hb, gb):
    L = q_ref.shape[0]
    kw = DIL_QB + 2 * half_w
    g_id = pl.program_id(1)
    lane = lax.broadcasted_iota(jnp.int32, (DIL_QB, LANES), 1)

    @pl.when(g_id == 0)
    def _():
        lse_ref[...] = jnp.zeros(lse_ref.shape, F32)

    def body(step, carry):
        qs_l, ks_l, qb, kb, vb = [], [], [], [], []
        for j in range(gb):
            qs = pl.multiple_of((step * gb + j) * DIL_QB, DIL_QB)
            ks = pl.multiple_of(jnp.clip(qs - half_w, 0, L - kw), half_w)
            qs_l.append(qs)
            ks_l.append(ks)
            for h in range(hb):
                sl = slice(h * DIL_HEAD_DIM, (h + 1) * DIL_HEAD_DIM)
                qb.append(q_ref[pl.ds(qs, DIL_QB), sl])
                kb.append(k_ref[pl.ds(ks, kw), sl])
                vb.append(v_ref[pl.ds(ks, kw), sl])
        q = jnp.stack(qb)
        k = jnp.stack(kb)
        v = jnp.stack(vb)
        s = jnp.einsum("gqd,gkd->gqk", q, k, preferred_element_type=F32)
        rel = (lax.broadcasted_iota(jnp.int32, (DIL_QB, kw), 0)
               - lax.broadcasted_iota(jnp.int32, (DIL_QB, kw), 1))
        bias = []
        for j in range(gb):
            mask = jnp.abs(rel + (qs_l[j] - ks_l[j])) <= half_w
            bias += [jnp.where(mask, 0.0, NEG_INF)] * hb
        s = s + jnp.stack(bias)
        m = jnp.max(s, axis=-1, keepdims=True)
        p = jnp.exp(s - m)
        den = jnp.sum(p, axis=-1, keepdims=True)
        o = jnp.einsum("gqk,gkd->gqd", p.astype(BF16), v, preferred_element_type=F32) / den
        lse = m + jnp.log(den)
        for j in range(gb):
            tile = lse_ref[pl.ds(qs_l[j], DIL_QB), :]
            for h in range(hb):
                sl = slice(h * DIL_HEAD_DIM, (h + 1) * DIL_HEAD_DIM)
                o_ref[pl.ds(qs_l[j], DIL_QB), sl] = o[j * hb + h].astype(o_ref.dtype)
                tile = jnp.where(lane == g_id * hb + h, lse[j * hb + h], tile)
            lse_ref[pl.ds(qs_l[j], DIL_QB), :] = tile
        return carry

    lax.fori_loop(0, L // (DIL_QB * gb), body, 0)


def _dil_pattern(dq, dk, dv, v_col0, window, dil, hb, gb):
    L = dq.shape[1]
    half_w = window // (2 * dil)
    w = hb * DIL_HEAD_DIM
    spec = lambda c0: pl.BlockSpec((None, L, w), lambda r, g, c0=c0: (r, 0, c0 + g))
    return pl.pallas_call(
        functools.partial(_dil_kernel, half_w=half_w, hb=hb, gb=gb),
        grid=(dil, N_DIL_HEADS // hb),
        in_specs=[spec(0), spec(0), spec(v_col0 // w)],
        out_specs=[spec(0), pl.BlockSpec((None, L, LANES), lambda r, g: (r, 0, 0))],
        out_shape=[jax.ShapeDtypeStruct((dil, L, HD), BF16), jax.ShapeDtypeStruct((dil, L, LANES), F32)],
        compiler_params=_cparams(("parallel", "arbitrary")),
        name=f"dil_attn_d{dil}",
    )(dq, dk, dv)


def _outproj_kernel(a_ref, o1_ref, l1_ref, o4_ref, l4_ref, o16_ref, l16_ref, p4t_ref, p16t_ref,
                    x_ref, wa_ref, wb_ref, g_ref, wr_ref, wrh_ref, br_ref,
                    x1_ref, xn_ref, ri_ref, rf_ref):
    tm = x_ref.shape[0]

    def to_token_order(o_ref, l_ref, pt_ref):
        pt = pt_ref[...]
        o = _dot(pt, o_ref[...].reshape(tm, HD))
        lse = l_ref[...].reshape(tm, LANES)
        hi = lse.astype(BF16)
        rem = lse - hi.astype(F32)
        mid = rem.astype(BF16)
        lo = (rem - mid.astype(F32)).astype(BF16)
        return o, _dot(pt, hi) + _dot(pt, mid) + _dot(pt, lo)

    o1 = o1_ref[...].astype(F32)
    l1 = l1_ref[...]
    o4, l4 = to_token_order(o4_ref, l4_ref, p4t_ref)
    o16, l16 = to_token_order(o16_ref, l16_ref, p16t_ref)
    big = jnp.maximum(jnp.maximum(l1, l4), l16)
    e1 = jnp.exp(l1 - big)
    e4 = jnp.exp(l4 - big)
    e16 = jnp.exp(l16 - big)
    inv = 1.0 / (e1 + e4 + e16)
    w1, w4, w16 = e1 * inv, e4 * inv, e16 * inv
    slabs = []
    for h in range(N_DIL_HEADS):
        sl = slice(h * DIL_HEAD_DIM, (h + 1) * DIL_HEAD_DIM)
        slabs.append(w1[:, h:h + 1] * o1[:, sl] + w4[:, h:h + 1] * o4[:, sl] + w16[:, h:h + 1] * o16[:, sl])
    dil_o = jnp.concatenate(slabs, axis=-1).astype(BF16)

    x1 = x_ref[...] + _dot(a_ref[...], wa_ref[...]) + _dot(dil_o, wb_ref[...])
    x1_ref[...] = x1
    r = lax.rsqrt(jnp.mean(x1 * x1, axis=-1, keepdims=True) + NORM_EPS)
    xn = x1 * r * g_ref[...]
    xn_ref[...] = xn
    xh = xn.astype(BF16)
    xl = (xn - xh.astype(F32)).astype(BF16)
    two = _dot(xh, wr_ref[...])
    logits = two[:, :LANES] + two[:, LANES:] + _dot(xl, wrh_ref[...]) + br_ref[...]
    lane = lax.broadcasted_iota(jnp.int32, logits.shape, 1)
    ninf = -jnp.inf

    def first_argmax(vals, vmax):
        return jnp.min(jnp.where(vals == vmax, lane, LANES), axis=-1, keepdims=True)

    coarse = jnp.where(lane < N_EXPERT_GROUPS, logits, ninf)
    cmax = jnp.max(coarse, axis=-1, keepdims=True)
    g = first_argmax(coarse, cmax)
    p_g = 1.0 / jnp.sum(jnp.exp(coarse - cmax), axis=-1, keepdims=True)
    lo_lane = N_EXPERT_GROUPS + g * EXPERTS_PER_GROUP
    fine = jnp.where((lane >= lo_lane) & (lane < lo_lane + EXPERTS_PER_GROUP), logits, ninf)
    v1 = jnp.max(fine, axis=-1, keepdims=True)
    j1 = first_argmax(fine, v1)
    fine2 = jnp.where(lane == j1, ninf, fine)
    v2 = jnp.max(fine2, axis=-1, keepdims=True)
    j2 = first_argmax(fine2, v2)
    e2 = jnp.exp(v2 - v1)
    g1 = 1.0 / (1.0 + e2)
    g2 = e2 / (1.0 + e2)
    ri_ref[...] = jnp.where(lane == 0, j1 - N_EXPERT_GROUPS, jnp.where(lane == 1, j2 - N_EXPERT_GROUPS, 0))
    rf_ref[...] = jnp.where(lane == 0, p_g * g1, jnp.where(lane == 1, p_g * g2, 0.0))


def _outproj_router(mla_o, dil1, dil4, dil16, p4t, p16t, x2, w_out_bf, g2, wr_two, wr_hi, b_router):
    S = x2.shape[0]
    tm = PERM_TILE
    half = N_MLA_HEADS * MLA_V_DIM
    row = lambda w: pl.BlockSpec((tm, w), lambda i: (i, 0))
    res = lambda dil, w: pl.BlockSpec((dil, tm // dil, w), lambda i: (0, i, 0))
    const = lambda a: pl.BlockSpec(a.shape, lambda i: (0, 0))
    return pl.pallas_call(
        _outproj_kernel,
        grid=(S // tm,),
        in_specs=[
            row(half),
            pl.BlockSpec((None, tm, HD), lambda i: (0, i, 0)), pl.BlockSpec((None, tm, LANES), lambda i: (0, i, 0)),
            res(4, HD), res(4, LANES), res(16, HD), res(16, LANES), const(p4t), const(p16t),
            row(D_MODEL),
            pl.BlockSpec((half, D_MODEL), lambda i: (0, 0)),
            pl.BlockSpec((HD, D_MODEL), lambda i: (1, 0)),
            const(g2), const(wr_two), const(wr_hi), const(b_router),
        ],
        out_specs=[row(D_MODEL), row(D_MODEL), row(LANES), row(LANES)],
        out_shape=[
            jax.ShapeDtypeStruct((S, D_MODEL), F32), jax.ShapeDtypeStruct((S, D_MODEL), F32),
            jax.ShapeDtypeStruct((S, LANES), jnp.int32), jax.ShapeDtypeStruct((S, LANES), F32),
        ],
        compiler_params=_cparams(("parallel",)),
        name="outproj_router",
    )(mla_o, dil1[0], dil1[1], dil4[0], dil4[1], dil16[0], dil16[1], p4t, p16t,
      x2, w_out_bf, w_out_bf, g2, wr_two, wr_hi, b_router)


MOE_CAP = 512
MOE_RT = 256
MOE_RH = 128
MOE_KC = 256
MOE_NK = D_MODEL // MOE_KC
MOE_FB = 768
MOE_NB = 2
MOE_FB_LAST = EXPERT_FF - (MOE_NB - 1) * MOE_FB
MOE_STEPS = MOE_NK + MOE_NB
DMA_UNROLL = 8
assert 0 < MOE_FB_LAST <= MOE_FB and MOE_FB % LANES == 0 and MOE_FB_LAST % LANES == 0


def _moe_kernel(ie_ref, is_ref, in_ref, ord_ref,
                xn_hbm, w1_ref, w3_ref, w2_ref,
                y_hbm, xf_ref, xb_ref, wcat_ref, ab_ref, h_ref, acc_ref, sem_in, sem_out):
    i = pl.program_id(0)
    j = pl.program_id(1)
    n_items = pl.num_programs(0)
    n_tokens = xn_hbm.shape[0]
    n = in_ref[i]
    slot = lax.rem(i, 2)

    def row_in(item, sl, r):
        tok = lax.shift_right_logical(ord_ref[is_ref[item] + r], 1)
        return pltpu.make_async_copy(xn_hbm.at[pl.ds(tok, 1), :], xf_ref.at[sl, pl.ds(r, 1), :], sem_in.at[sl])

    def row_out(item, r):
        a = ord_ref[is_ref[item] + r]
        dst = (a & 1) * n_tokens + lax.shift_right_logical(a, 1)
        return pltpu.make_async_copy(acc_ref.at[pl.ds(r, 1), :], y_hbm.at[pl.ds(dst, 1), :], sem_out)

    def for_rows(count, fn):
        groups = lax.div(count, DMA_UNROLL)

        def group(g, c):
            for u in range(DMA_UNROLL):
                fn(g * DMA_UNROLL + u)
            return c
        lax.fori_loop(0, groups, group, 0)

        def single(r, c):
            fn(r)
            return c
        lax.fori_loop(groups * DMA_UNROLL, count, single, 0)

    def wait_rows(count, rows_desc):
        groups = lax.div(count, DMA_UNROLL)

        def group(g, c):
            rows_desc(pl.multiple_of(g * DMA_UNROLL, DMA_UNROLL), DMA_UNROLL).wait()
            return c
        lax.fori_loop(0, groups, group, 0)

        def single(r, c):
            rows_desc(r, 1).wait()
            return c
        lax.fori_loop(groups * DMA_UNROLL, count, single, 0)

    def rows_in(sl, r0, size):
        return pltpu.make_async_copy(xn_hbm.at[pl.ds(0, size), :], xf_ref.at[sl, pl.ds(r0, size), :], sem_in.at[sl])

    def rows_out(r0, size):
        return pltpu.make_async_copy(acc_ref.at[pl.ds(r0, size), :], y_hbm.at[pl.ds(0, size), :], sem_out)

    def for_row_tiles(fn):
        n_full = lax.div(n, MOE_RT)

        def full(t, c):
            fn(pl.multiple_of(t * MOE_RT, MOE_RT), MOE_RT)
            return c
        lax.fori_loop(0, n_full, full, 0)
        rem = n - n_full * MOE_RT
        base = pl.multiple_of(n_full * MOE_RT, MOE_RT)

        @pl.when(rem > 0)
        def _():
            fn(base, MOE_RH)

        @pl.when(rem > MOE_RH)
        def _():
            fn(base + MOE_RH, MOE_RH)

    @pl.when((i == 0) & (j == 0))
    def _():
        xf_ref[...] = jnp.zeros(xf_ref.shape, F32)
        for_rows(n, lambda r: row_in(0, 0, r).start())

    @pl.when((n > 0) & (j == 0))
    def _():
        wait_rows(n, functools.partial(rows_in, slot))

        @pl.when(i > 0)
        def _():
            wait_rows(in_ref[i - 1], rows_out)

        for t in range(MOE_CAP // MOE_RH):
            @pl.when(t * MOE_RH < n)
            def _():
                sl = pl.ds(t * MOE_RH, MOE_RH)
                for kc in range(MOE_NK):
                    xb_ref[kc, sl, :] = xf_ref[slot, sl, kc * MOE_KC:(kc + 1) * MOE_KC].astype(BF16)
                ab_ref[sl, :] = jnp.zeros((MOE_RH, 2 * EXPERT_FF), F32)

    @pl.when((n > 0) & (j == 1) & (i + 1 < n_items))
    def _():
        nxt = jnp.minimum(i + 1, n_items - 1)
        for_rows(in_ref[nxt], lambda r: row_in(nxt, 1 - slot, r).start())

    @pl.when((n > 0) & (j < MOE_NK))
    def _():
        wcat_ref[:, :EXPERT_FF] = w1_ref[...].astype(BF16)
        wcat_ref[:, EXPERT_FF:] = w3_ref[...].astype(BF16)
        kc = jnp.minimum(j, MOE_NK - 1)

        def up(off, size):
            ab_ref[pl.ds(off, size), :] += _dot(xb_ref[kc, pl.ds(off, size), :], wcat_ref[...])
        for_row_tiles(up)

    @pl.when((n > 0) & (j == MOE_NK))
    def _():
        def act(off, size):
            a = ab_ref[pl.ds(off, size), :EXPERT_FF]
            b = ab_ref[pl.ds(off, size), EXPERT_FF:]
            h_ref[pl.ds(off, size), :] = (a * jax.nn.sigmoid(a) * b).astype(BF16)
        for_row_tiles(act)
        w2 = w2_ref[...].astype(BF16)

        def down(off, size):
            acc_ref[pl.ds(off, size), :] = _dot(h_ref[pl.ds(off, size), :MOE_FB], w2)
        for_row_tiles(down)

    @pl.when((n > 0) & (j == MOE_NK + 1))
    def _():
        w2 = w2_ref[:MOE_FB_LAST, :].astype(BF16)

        def down(off, size):
            acc_ref[pl.ds(off, size), :] += _dot(h_ref[pl.ds(off, size), MOE_FB:], w2)
        for_row_tiles(down)
        for_rows(n, lambda r: row_out(i, r).start())
        nxt = jnp.minimum(i + 1, n_items - 1)

        @pl.when((i == n_items - 1) | (in_ref[nxt] == 0))
        def _():
            wait_rows(n, rows_out)


def _moe_experts(xn, w1, w3, w2, item_e, item_start, item_n, order):
    S = xn.shape[0]
    n_items = item_e.shape[0]

    def up_idx(i, j, ie, is_, n_, o):
        return ie[i], jnp.where(n_[i] > 0, jnp.minimum(j, MOE_NK - 1), MOE_NK - 1), 0

    def down_idx(i, j, ie, is_, n_, o):
        return ie[i], jnp.where(n_[i] > 0, jnp.maximum(j - MOE_NK, 0), MOE_NB - 1), 0

    up = pl.BlockSpec((None, MOE_KC, EXPERT_FF), up_idx)
    down = pl.BlockSpec((None, MOE_FB, D_MODEL), down_idx)
    grid_spec = pltpu.PrefetchScalarGridSpec(
        num_scalar_prefetch=4,
        grid=(n_items, MOE_STEPS),
        in_specs=[pl.BlockSpec(memory_space=pl.ANY), up, up, down],
        out_specs=pl.BlockSpec(memory_space=pl.ANY),
        scratch_shapes=[
            pltpu.VMEM((2, MOE_CAP, D_MODEL), F32),
            pltpu.VMEM((MOE_NK, MOE_CAP, MOE_KC), BF16),
            pltpu.VMEM((MOE_KC, 2 * EXPERT_FF), BF16),
            pltpu.VMEM((MOE_CAP, 2 * EXPERT_FF), F32),
            pltpu.VMEM((MOE_CAP, EXPERT_FF), BF16),
            pltpu.VMEM((MOE_CAP, D_MODEL), F32),
            pltpu.SemaphoreType.DMA((2,)),
            pltpu.SemaphoreType.DMA(()),
        ],
    )
    return pl.pallas_call(
        _moe_kernel,
        grid_spec=grid_spec,
        out_shape=jax.ShapeDtypeStruct((TOP_K * S, D_MODEL), F32),
        compiler_params=_cparams(("arbitrary", "arbitrary")),
        name="moe_experts",
    )(item_e, item_start, item_n, order, xn, w1, w3, w2)


def _moe_items(eid, cap):
    A = eid.size
    flat_e = eid.reshape(A)
    order = jnp.argsort(flat_e, stable=True).astype(jnp.int32)
    experts = jnp.arange(N_EXPERTS + 1, dtype=jnp.int32)
    starts = jnp.sum((flat_e[None, :] < experts[:, None]).astype(jnp.int32), axis=1)
    counts = starts[1:] - starts[:-1]
    per_e = (counts + cap - 1) // cap
    item_end = jnp.cumsum(per_e)
    total = item_end[-1]
    n_items = N_EXPERTS + A // cap
    idx = jnp.arange(n_items, dtype=jnp.int32)
    clamped = jnp.minimum(idx, total - 1)
    e = jnp.minimum(jnp.searchsorted(item_end, clamped, side="right"), N_EXPERTS - 1).astype(jnp.int32)
    local = clamped - (item_end[e] - per_e[e])
    used = idx < total
    item_start = jnp.where(used, starts[e] + local * cap, 0).astype(jnp.int32)
    item_n = jnp.where(used, jnp.clip(counts[e] - local * cap, 0, cap), 0).astype(jnp.int32)
    return e, item_start, item_n, order


def _combine_kernel(x1_ref, y0_ref, y1_ref, g_ref, o_ref):
    g = g_ref[...]
    o_ref[...] = x1_ref[...] + (g[:, 0:1] * y0_ref[...] + g[:, 1:2] * y1_ref[...])


def _combine(x1, y, gates, tm=256):
    S = x1.shape[0]
    nb = S // tm
    return pl.pallas_call(
        _combine_kernel,
        grid=(nb,),
        in_specs=[
            pl.BlockSpec((tm, D_MODEL), lambda i: (i, 0)),
            pl.BlockSpec((tm, D_MODEL), lambda i: (i, 0)),
            pl.BlockSpec((tm, D_MODEL), lambda i: (nb + i, 0)),
            pl.BlockSpec((tm, LANES), lambda i: (i, 0)),
        ],
        out_specs=pl.BlockSpec((tm, D_MODEL), lambda i: (i, 0)),
        out_shape=jax.ShapeDtypeStruct((S, D_MODEL), F32),
        compiler_params=_cparams(("parallel",)),
        name="moe_combine",
    )(x1, y, y, gates)


def _rot_half_cols(w, half):
    return jnp.concatenate([-w[..., half:], w[..., :half]], axis=-1)


def _rope_tables(S):
    pos = jnp.arange(S, dtype=F32)[:, None]

    def cs(half):
        inv = ROPE_THETA ** (-jnp.arange(half, dtype=F32) / half)
        ang = pos * inv[None, :]
        return jnp.cos(ang), jnp.sin(ang)

    c32, s32 = cs(MLA_ROPE_DIM // 2)
    t64 = jnp.concatenate([c32, c32, s32, s32], axis=-1)
    c64, s64 = cs(DIL_HEAD_DIM // 2)
    cos128 = jnp.concatenate([c64, c64], axis=-1)
    sin128 = jnp.concatenate([-s64, s64], axis=-1)
    return t64, cos128, sin128


def _residue_perm(dil):
    n = PERM_TILE // dil
    rows = jnp.arange(PERM_TILE)
    src = (rows % n) * dil + rows // n
    return (src[:, None] == jnp.arange(PERM_TILE)[None, :]).astype(BF16)


def kernel(x, norm1_g, w_in, g_cq, g_ckv, w_uq, w_ukv, mla_q_norm_g, mla_k_norm_g, dil_q_norm_g,
           dil_k_norm_g, w_out, norm2_g, w_group, b_group, w_expert, b_expert, w1, w3, w2):
    B, S, D = x.shape
    assert B == 1 and D == D_MODEL and norm1_g.shape[0] == 1
    x2 = x.reshape(S, D)
    half_r = MLA_ROPE_DIM // 2
    o1, o2 = Q_LORA_RANK, Q_LORA_RANK + KV_LORA_RANK
    o3 = o2 + MLA_ROPE_DIM

    wi = w_in[0]
    w_kr = wi[:, o2:o3]
    w_z = jnp.concatenate([wi[:, o3:], wi[:, :o2], w_kr, _rot_half_cols(w_kr, half_r)], axis=-1).astype(BF16)

    wq = w_uq[0].reshape(Q_LORA_RANK, N_MLA_HEADS, MLA_QK_DIM)
    wq_pe = wq[..., MLA_NOPE_DIM:]
    wq_ext = jnp.concatenate([wq, _rot_half_cols(wq_pe, half_r)], axis=-1)
    wq_ext = wq_ext.reshape(Q_LORA_RANK, N_MLA_HEADS * MLA_PAD).astype(BF16)
    wkv = w_ukv[0].astype(BF16)

    def ext_gain(g):
        pe = g[MLA_NOPE_DIM:]
        return jnp.concatenate([g, pe[half_r:], pe[:half_r]])[None, :]

    gq_ext = ext_gain(mla_q_norm_g[0])
    gk_ext = ext_gain(mla_k_norm_g[0])
    t64, cos128, sin128 = _rope_tables(S)
    p4, p16 = _residue_perm(4), _residue_perm(16)

    z = _inproj(x2, norm1_g, w_z)
    q, k, vt, dq, dk, dq4, dk4, dv4, dq16, dk16, dv16 = _prep(
        z, g_cq, g_ckv, wq_ext, wkv, gq_ext, gk_ext[:, :LANES], gk_ext[:, LANES:],
        dil_q_norm_g, dil_k_norm_g, t64, cos128, sin128, p4, p16)
    mla_o = _mla_attn(q, k, vt)

    (w_1, d_1), (w_4, d_4), (w_16, d_16) = DIL_PATTERNS
    dil1 = _dil_pattern(dq[None], dk[None], z[None], Z_DV, w_1, d_1, hb=2, gb=4)
    dil4 = _dil_pattern(dq4, dk4, dv4, 0, w_4, d_4, hb=8, gb=1)
    dil16 = _dil_pattern(dq16, dk16, dv16, 0, w_16, d_16, hb=8, gb=1)

    pad = LANES - N_EXPERT_GROUPS - N_EXPERTS
    w_router = jnp.concatenate([w_group[0], w_expert[0], jnp.zeros((D, pad), F32)], axis=-1)
    b_router = jnp.concatenate([b_group[0], b_expert[0], jnp.zeros((pad,), F32)])[None, :]
    wr_hi = w_router.astype(BF16)
    wr_lo = (w_router - wr_hi.astype(F32)).astype(BF16)
    x1, xn, route_i, route_f = _outproj_router(
        mla_o, dil1, dil4, dil16, p4.T, p16.T, x2, w_out[0].astype(BF16), norm2_g,
        jnp.concatenate([wr_hi, wr_lo], axis=-1), wr_hi, b_router)

    item_e, item_start, item_n, order = _moe_items(route_i[:, :TOP_K], MOE_CAP)
    y = _moe_experts(xn, w1[0], w3[0], w2[0], item_e, item_start, item_n, order)
    out = _combine(x1, y, route_f)
    return out.reshape(B, S, D)
```

```python
import functools
import math

import jax
import jax.numpy as jnp
import numpy as np
from jax import lax
from jax.experimental import pallas as pl
from jax.experimental.pallas import tpu as pltpu

D_MODEL = 2048
N_MLA_HEADS = 8
MLA_NOPE_DIM = 128
MLA_ROPE_DIM = 64
MLA_QK_DIM = MLA_NOPE_DIM + MLA_ROPE_DIM
MLA_V_DIM = 128
Q_LORA_RANK = 512
KV_LORA_RANK = 512
N_DIL_HEADS = 8
DIL_HEAD_DIM = 128
DIL_PATTERNS = ((128, 1), (512, 4), (2048, 16))
ROPE_THETA = 10000.0
NORM_EPS = 1e-6
NEG_INF = -1e30
N_EXPERT_GROUPS = 8
EXPERTS_PER_GROUP = 8
N_EXPERTS = N_EXPERT_GROUPS * EXPERTS_PER_GROUP
TOP_K = 2
EXPERT_FF = 1408

LANES = 128
BF16_ROWS = 16
MXU_DIM = 256
HD = N_DIL_HEADS * DIL_HEAD_DIM
MLA_PAD = MXU_DIM
Z_DQ, Z_DK, Z_DV = 0, HD, 2 * HD
Z_CQ = 3 * HD
Z_CKV = Z_CQ + Q_LORA_RANK
Z_KR = Z_CKV + KV_LORA_RANK
Z_WIDTH = Z_KR + 2 * MLA_ROPE_DIM

ATT_TK = 512
ATT_TQ = 1024
ATT_UNROLL = 4
VT_ROWS = MLA_V_DIM + BF16_ROWS
PERM_TILE = 256
LOG2E = math.log2(math.e)

VMEM_LIMIT = 56 * 1024 * 1024

F32 = jnp.float32
BF16 = jnp.bfloat16


def _cparams(sem, vmem=VMEM_LIMIT):
    return pltpu.CompilerParams(dimension_semantics=sem, vmem_limit_bytes=vmem)


def _dot(a, b):
    return jnp.dot(a, b, preferred_element_type=F32)


def _dot_nt(a, b):
    return lax.dot_general(a, b, (((1,), (1,)), ((), ())), preferred_element_type=F32)


def _inproj_kernel(x_ref, g_ref, w_ref, z_ref):
    x = x_ref[...]
    r = lax.rsqrt(jnp.mean(x * x, axis=-1, keepdims=True) + NORM_EPS)
    h = (x * r * g_ref[...]).astype(BF16)
    z_ref[...] = _dot(h, w_ref[...]).astype(BF16)


def _inproj(x2, g1, w_z, tm=256):
    S = x2.shape[0]
    return pl.pallas_call(
        _inproj_kernel,
        grid=(S // tm,),
        in_specs=[
            pl.BlockSpec((tm, D_MODEL), lambda i: (i, 0)),
            pl.BlockSpec((1, D_MODEL), lambda i: (0, 0)),
            pl.BlockSpec((D_MODEL, Z_WIDTH), lambda i: (0, 0)),
        ],
        out_specs=pl.BlockSpec((tm, Z_WIDTH), lambda i: (i, 0)),
        out_shape=jax.ShapeDtypeStruct((S, Z_WIDTH), BF16),
        compiler_params=_cparams(("parallel",)),
        name="inproj",
    )(x2, g1, w_z)


def _prep_kernel(zdq_ref, zdk_ref, zdv_ref, zcq_ref, zckv_ref, zkr_ref,
                 gcq_ref, gckv_ref, wq_ref, wkv_ref, gq_ref, gkn_ref, gkp_ref,
                 gdq_ref, gdk_ref, t64_ref, cos_ref, sin_ref, p4_ref, p16_ref,
                 q_ref, k_ref, vt_ref, dq_ref, dk_ref,
                 dq4_ref, dk4_ref, dv4_ref, dq16_ref, dk16_ref, dv16_ref):
    tm = zcq_ref.shape[0]
    lane = lax.broadcasted_iota(jnp.int32, (1, LANES), 1)
    first_half = lane < MLA_ROPE_DIM

    def rms_rows(c, g):
        c = c.astype(F32)
        r = lax.rsqrt(jnp.mean(c * c, axis=-1, keepdims=True) + NORM_EPS)
        return (c * r * g).astype(BF16)

    cq = rms_rows(zcq_ref[...], gcq_ref[...])
    ckv = rms_rows(zckv_ref[...], gckv_ref[...])
    qe = _dot(cq, wq_ref[...])
    kv = _dot(ckv, wkv_ref[...])
    kr = zkr_ref[...].astype(F32)
    kr_ss = jnp.sum(jnp.where(first_half, kr * kr, 0.0), axis=-1, keepdims=True)
    t64 = t64_ref[...]
    q_scale = MLA_QK_DIM ** -0.5 * LOG2E

    def rope64(ext):
        t = ext * t64
        return jnp.where(first_half, t + pltpu.roll(t, MLA_ROPE_DIM, 1), 0.0)

    ones_rows = (lax.broadcasted_iota(jnp.int32, (BF16_ROWS, tm), 0) == 0).astype(BF16)
    eye = (lax.broadcasted_iota(jnp.int32, (MLA_V_DIM, MLA_V_DIM), 0)
           == lax.broadcasted_iota(jnp.int32, (MLA_V_DIM, MLA_V_DIM), 1)).astype(BF16)
    for h in range(N_MLA_HEADS):
        base = h * MLA_PAD
        qn = qe[:, base:base + LANES]
        qp = qe[:, base + LANES:base + 2 * LANES]
        ss = (jnp.sum(qn * qn, axis=-1, keepdims=True)
              + jnp.sum(jnp.where(first_half, qp * qp, 0.0), axis=-1, keepdims=True))
        r = lax.rsqrt(ss * (1.0 / MLA_QK_DIM) + NORM_EPS) * q_scale
        q_ref[:, base:base + LANES] = (qn * r * gq_ref[:, :LANES]).astype(BF16)
        q_ref[:, base + LANES:base + MLA_PAD] = rope64(qp * r * gq_ref[:, LANES:]).astype(BF16)

        kn = kv[:, base:base + LANES]
        ss = jnp.sum(kn * kn, axis=-1, keepdims=True) + kr_ss
        r = lax.rsqrt(ss * (1.0 / MLA_QK_DIM) + NORM_EPS)
        k_ref[:, base:base + LANES] = (kn * r * gkn_ref[...]).astype(BF16)
        k_ref[:, base + LANES:base + MLA_PAD] = rope64(kr * r * gkp_ref[...]).astype(BF16)
        v_h = kv[:, base + LANES:base + 2 * LANES].astype(BF16)
        vt_ref[h, :MLA_V_DIM, :] = _dot_nt(eye, v_h).astype(BF16)
        vt_ref[h, MLA_V_DIM:, :] = ones_rows

    cos = cos_ref[...]
    sin = sin_ref[...]
    d_scale = DIL_HEAD_DIM ** -0.5

    def dil_head(x, g, scale):
        x = x.astype(F32)
        r = lax.rsqrt(jnp.mean(x * x, axis=-1, keepdims=True) + NORM_EPS)
        y = x * r * g
        return ((y * cos + pltpu.roll(y, DIL_HEAD_DIM // 2, 1) * sin) * scale).astype(BF16)

    for h in range(N_DIL_HEADS):
        sl = slice(h * DIL_HEAD_DIM, (h + 1) * DIL_HEAD_DIM)
        dq_ref[:, sl] = dil_head(zdq_ref[:, sl], gdq_ref[...], d_scale)
        dk_ref[:, sl] = dil_head(zdk_ref[:, sl], gdk_ref[...], 1.0)

    for src, d4, d16 in ((dq_ref, dq4_ref, dq16_ref), (dk_ref, dk4_ref, dk16_ref), (zdv_ref, dv4_ref, dv16_ref)):
        for sub in range(tm // PERM_TILE):
            xs = src[sub * PERM_TILE:(sub + 1) * PERM_TILE, :]
            for dil, p_ref, dst in ((4, p4_ref, d4), (16, p16_ref, d16)):
                n = PERM_TILE // dil
                xp = _dot(p_ref[...], xs).astype(BF16)
                for r in range(dil):
                    dst[r, sub * n:(sub + 1) * n, :] = xp[r * n:(r + 1) * n, :]


def _prep(z, gcq, gckv, wq_ext, wkv, gq_ext, gk_nope, gk_pe, gdq, gdk, t64, cos128, sin128, p4, p16):
    S = z.shape[0]
    tm = ATT_TK
    row = lambda w, j: pl.BlockSpec((tm, w), lambda i, j=j: (i, j))
    full = lambda a: pl.BlockSpec(a.shape, lambda i: (0, 0))
    res = lambda dil: pl.BlockSpec((dil, tm // dil, HD), lambda i: (0, i, 0))
    qk_w = N_MLA_HEADS * MLA_PAD
    res_shape = lambda dil: jax.ShapeDtypeStruct((dil, S // dil, HD), BF16)
    return pl.pallas_call(
        _prep_kernel,
        grid=(S // tm,),
        in_specs=[
            row(HD, Z_DQ // HD), row(HD, Z_DK // HD), row(HD, Z_DV // HD),
            row(Q_LORA_RANK, Z_CQ // Q_LORA_RANK), row(KV_LORA_RANK, Z_CKV // KV_LORA_RANK),
            row(LANES, Z_KR // LANES),
            full(gcq), full(gckv), full(wq_ext), full(wkv), full(gq_ext), full(gk_nope), full(gk_pe),
            full(gdq), full(gdk),
            row(LANES, 0), row(LANES, 0), row(LANES, 0), full(p4), full(p16),
        ],
        out_specs=[row(qk_w, 0), row(qk_w, 0),
                   pl.BlockSpec((N_MLA_HEADS, None, VT_ROWS, tm), lambda i: (0, i, 0, 0)),
                   row(HD, 0), row(HD, 0),
                   res(4), res(4), res(4), res(16), res(16), res(16)],
        out_shape=[
            jax.ShapeDtypeStruct((S, qk_w), BF16), jax.ShapeDtypeStruct((S, qk_w), BF16),
            jax.ShapeDtypeStruct((N_MLA_HEADS, S // tm, VT_ROWS, tm), BF16),
            jax.ShapeDtypeStruct((S, HD), BF16), jax.ShapeDtypeStruct((S, HD), BF16),
            res_shape(4), res_shape(4), res_shape(4), res_shape(16), res_shape(16), res_shape(16),
        ],
        compiler_params=_cparams(("parallel",)),
        name="qkv_prep",
    )(z, z, z, z, z, z, gcq, gckv, wq_ext, wkv, gq_ext, gk_nope, gk_pe, gdq, gdk, t64, cos128, sin128, p4, p16)


def _mla_attn_kernel(q_ref, k_ref, vt_ref, o_ref, m_ref, acc_ref, s_ref):
    n_chunks = vt_ref.shape[0]
    q = q_ref[...]

    def scores(c):
        off = pl.multiple_of(c * ATT_TK, ATT_TK)
        return _dot_nt(k_ref[pl.ds(off, ATT_TK), :], q)

    m_ref[...] = jnp.full(m_ref.shape, -jnp.inf, F32)
    acc_ref[...] = jnp.zeros(acc_ref.shape, F32)

    def fold(c, slot):
        s = s_ref[slot]
        m_old = m_ref[...]
        m_new = jnp.maximum(m_old, jnp.max(s, axis=0, keepdims=True))
        alpha = jnp.exp2(m_old - m_new)
        p = jnp.exp2(s - m_new).astype(BF16)
        acc_ref[...] = alpha * acc_ref[...] + _dot(vt_ref[c], p)
        m_ref[...] = m_new

    s_ref[0] = scores(0)

    def body(t, carry):
        for u in range(ATT_UNROLL):
            c = ATT_UNROLL * t + u
            s_ref[(u + 1) % 2] = scores(jnp.minimum(c + 1, n_chunks - 1))
            fold(c, u % 2)
        return carry

    lax.fori_loop(0, n_chunks // ATT_UNROLL, body, 0)
    acc = acc_ref[...]
    o_t = acc[:MLA_V_DIM, :] / acc[MLA_V_DIM:MLA_V_DIM + 1, :]
    o_ref[...] = o_t.T.astype(o_ref.dtype)


def _mla_attn(q, k, vt):
    S = q.shape[0]
    n_chunks = vt.shape[1]
    return pl.pallas_call(
        _mla_attn_kernel,
        grid=(N_MLA_HEADS, S // ATT_TQ),
        in_specs=[
            pl.BlockSpec((ATT_TQ, MLA_PAD), lambda h, i: (i, h)),
            pl.BlockSpec((S, MLA_PAD), lambda h, i: (0, h)),
            pl.BlockSpec((None, n_chunks, VT_ROWS, ATT_TK), lambda h, i: (h, 0, 0, 0)),
        ],
        out_specs=pl.BlockSpec((ATT_TQ, MLA_V_DIM), lambda h, i: (i, h)),
        out_shape=jax.ShapeDtypeStruct((S, N_MLA_HEADS * MLA_V_DIM), BF16),
        scratch_shapes=[pltpu.VMEM((1, ATT_TQ), F32), pltpu.VMEM((VT_ROWS, ATT_TQ), F32),
                        pltpu.VMEM((2, ATT_TK, ATT_TQ), F32)],
        compiler_params=_cparams(("parallel", "parallel")),
        name="mla_attn",
    )(q, k, vt)


DIL_QB = 128


def _dil_kernel(q_ref, k_ref, v_ref, o_ref, lse_ref, *, half_w, hb, gb):
    L = q_ref.shape[0]
    kw = DIL_QB + 2 * half_w
    g_id = pl.program_id(1)
    lane = lax.broadcasted_iota(jnp.int32, (DIL_QB, LANES), 1)

    @pl.when(g_id == 0)
    def _():
        lse_ref[...] = jnp.zeros(lse_ref.shape, F32)

    def body(step, carry):
        qs_l, ks_l, qb, kb, vb = [], [], [], [], []
        for j in range(gb):
            qs = pl.multiple_of((step * gb + j) * DIL_QB, DIL_QB)
            ks = pl.multiple_of(jnp.clip(qs - half_w, 0, L - kw), half_w)
            qs_l.append(qs)
            ks_l.append(ks)
            for h in range(hb):
                sl = slice(h * DIL_HEAD_DIM, (h + 1) * DIL_HEAD_DIM)
                qb.append(q_ref[pl.ds(qs, DIL_QB), sl])
                kb.append(k_ref[pl.ds(ks, kw), sl])
                vb.append(v_ref[pl.ds(ks, kw), sl])
        q = jnp.stack(qb)
        k = jnp.stack(kb)
        v = jnp.stack(vb)
        s = jnp.einsum("gqd,gkd->gqk", q, k, preferred_element_type=F32)
        rel = (lax.broadcasted_iota(jnp.int32, (DIL_QB, kw), 0)
               - lax.broadcasted_iota(jnp.int32, (DIL_QB, kw), 1))
        bias = []
        for j in range(gb):
            mask = jnp.abs(rel + (qs_l[j] - ks_l[j])) <= half_w
            bias += [jnp.where(mask, 0.0, NEG_INF)] * hb
        s = s + jnp.stack(bias)
        m = jnp.max(s, axis=-1, keepdims=True)
        p = jnp.exp(s - m)
        den = jnp.sum(p, axis=-1, keepdims=True)
        o = jnp.einsum("gqk,gkd->gqd", p.astype(BF16), v, preferred_element_type=F32) / den
        lse = m + jnp.log(den)
        for j in range(gb):
            tile = lse_ref[pl.ds(qs_l[j], DIL_QB), :]
            for h in range(hb):
                sl = slice(h * DIL_HEAD_DIM, (h + 1) * DIL_HEAD_DIM)
                o_ref[pl.ds(qs_l[j], DIL_QB), sl] = o[j * hb + h].astype(o_ref.dtype)
                tile = jnp.where(lane == g_id * hb + h, lse[j * hb + h], tile)
            lse_ref[pl.ds(qs_l[j], DIL_QB), :] = tile
        return carry

    lax.fori_loop(0, L // (DIL_QB * gb), body, 0)


def _dil_pattern(dq, dk, dv, v_col0, window, dil, hb, gb):
    L = dq.shape[1]
    half_w = window // (2 * dil)
    w = hb * DIL_HEAD_DIM
    spec = lambda c0: pl.BlockSpec((None, L, w), lambda r, g, c0=c0: (r, 0, c0 + g))
    return pl.pallas_call(
        functools.partial(_dil_kernel, half_w=half_w, hb=hb, gb=gb),
        grid=(dil, N_DIL_HEADS // hb),
        in_specs=[spec(0), spec(0), spec(v_col0 // w)],
        out_specs=[spec(0), pl.BlockSpec((None, L, LANES), lambda r, g: (r, 0, 0))],
        out_shape=[jax.ShapeDtypeStruct((dil, L, HD), BF16), jax.ShapeDtypeStruct((dil, L, LANES), F32)],
        compiler_params=_cparams(("parallel", "arbitrary")),
        name=f"dil_attn_d{dil}",
    )(dq, dk, dv)


def _outproj_kernel(a_ref, o1_ref, l1_ref, o4_ref, l4_ref, o16_ref, l16_ref, p4t_ref, p16t_ref,
                    x_ref, wa_ref, wb_ref, g_ref, wr_ref, wrh_ref, br_ref,
                    x1_ref, xn_ref, ri_ref, rf_ref):
    tm = x_ref.shape[0]

    def to_token_order(o_ref, l_ref, pt_ref):
        pt = pt_ref[...]
        o = _dot(pt, o_ref[...].reshape(tm, HD))
        lse = l_ref[...].reshape(tm, LANES)
        hi = lse.astype(BF16)
        rem = lse - hi.astype(F32)
        mid = rem.astype(BF16)
        lo = (rem - mid.astype(F32)).astype(BF16)
        return o, _dot(pt, hi) + _dot(pt, mid) + _dot(pt, lo)

    o1 = o1_ref[...].astype(F32)
    l1 = l1_ref[...]
    o4, l4 = to_token_order(o4_ref, l4_ref, p4t_ref)
    o16, l16 = to_token_order(o16_ref, l16_ref, p16t_ref)
    big = jnp.maximum(jnp.maximum(l1, l4), l16)
    e1 = jnp.exp(l1 - big)
    e4 = jnp.exp(l4 - big)
    e16 = jnp.exp(l16 - big)
    inv = 1.0 / (e1 + e4 + e16)
    w1, w4, w16 = e1 * inv, e4 * inv, e16 * inv
    slabs = []
    for h in range(N_DIL_HEADS):
        sl = slice(h * DIL_HEAD_DIM, (h + 1) * DIL_HEAD_DIM)
        slabs.append(w1[:, h:h + 1] * o1[:, sl] + w4[:, h:h + 1] * o4[:, sl] + w16[:, h:h + 1] * o16[:, sl])
    dil_o = jnp.concatenate(slabs, axis=-1).astype(BF16)

    x1 = x_ref[...] + _dot(a_ref[...], wa_ref[...]) + _dot(dil_o, wb_ref[...])
    x1_ref[...] = x1
    r = lax.rsqrt(jnp.mean(x1 * x1, axis=-1, keepdims=True) + NORM_EPS)
    xn = x1 * r * g_ref[...]
    xn_ref[...] = xn
    xh = xn.astype(BF16)
    xl = (xn - xh.astype(F32)).astype(BF16)
    two = _dot(xh, wr_ref[...])
    logits = two[:, :LANES] + two[:, LANES:] + _dot(xl, wrh_ref[...]) + br_ref[...]
    lane = lax.broadcasted_iota(jnp.int32, logits.shape, 1)
    ninf = -jnp.inf

    def first_argmax(vals, vmax):
        return jnp.min(jnp.where(vals == vmax, lane, LANES), axis=-1, keepdims=True)

    coarse = jnp.where(lane < N_EXPERT_GROUPS, logits, ninf)
    cmax = jnp.max(coarse, axis=-1, keepdims=True)
    g = first_argmax(coarse, cmax)
    p_g = 1.0 / jnp.sum(jnp.exp(coarse - cmax), axis=-1, keepdims=True)
    lo_lane = N_EXPERT_GROUPS + g * EXPERTS_PER_GROUP
    fine = jnp.where((lane >= lo_lane) & (lane < lo_lane + EXPERTS_PER_GROUP), logits, ninf)
    v1 = jnp.max(fine, axis=-1, keepdims=True)
    j1 = first_argmax(fine, v1)
    fine2 = jnp.where(lane == j1, ninf, fine)
    v2 = jnp.max(fine2, axis=-1, keepdims=True)
    j2 = first_argmax(fine2, v2)
    e2 = jnp.exp(v2 - v1)
    g1 = 1.0 / (1.0 + e2)
    g2 = e2 / (1.0 + e2)
    ri_ref[...] = jnp.where(lane == 0, j1 - N_EXPERT_GROUPS, jnp.where(lane == 1, j2 - N_EXPERT_GROUPS, 0))
    rf_ref[...] = jnp.where(lane == 0, p_g * g1, jnp.where(lane == 1, p_g * g2, 0.0))


def _outproj_router(mla_o, dil1, dil4, dil16, p4t, p16t, x2, w_out_bf, g2, wr_two, wr_hi, b_router):
    S = x2.shape[0]
    tm = PERM_TILE
    half = N_MLA_HEADS * MLA_V_DIM
    row = lambda w: pl.BlockSpec((tm, w), lambda i: (i, 0))
    res = lambda dil, w: pl.BlockSpec((dil, tm // dil, w), lambda i: (0, i, 0))
    const = lambda a: pl.BlockSpec(a.shape, lambda i: (0, 0))
    return pl.pallas_call(
        _outproj_kernel,
        grid=(S // tm,),
        in_specs=[
            row(half),
            pl.BlockSpec((None, tm, HD), lambda i: (0, i, 0)), pl.BlockSpec((None, tm, LANES), lambda i: (0, i, 0)),
            res(4, HD), res(4, LANES), res(16, HD), res(16, LANES), const(p4t), const(p16t),
            row(D_MODEL),
            pl.BlockSpec((half, D_MODEL), lambda i: (0, 0)),
            pl.BlockSpec((HD, D_MODEL), lambda i: (1, 0)),
            const(g2), const(wr_two), const(wr_hi), const(b_router),
        ],
        out_specs=[row(D_MODEL), row(D_MODEL), row(LANES), row(LANES)],
        out_shape=[
            jax.ShapeDtypeStruct((S, D_MODEL), F32), jax.ShapeDtypeStruct((S, D_MODEL), F32),
            jax.ShapeDtypeStruct((S, LANES), jnp.int32), jax.ShapeDtypeStruct((S, LANES), F32),
        ],
        compiler_params=_cparams(("parallel",)),
        name="outproj_router",
    )(mla_o, dil1[0], dil1[1], dil4[0], dil4[1], dil16[0], dil16[1], p4t, p16t,
      x2, w_out_bf, w_out_bf, g2, wr_two, wr_hi, b_router)


MOE_CAP = 512
MOE_RH = 128
MOE_KC = 256
MOE_NK = D_MODEL // MOE_KC
MOE_FB = 768
MOE_NB = 2
MOE_FB_LAST = EXPERT_FF - (MOE_NB - 1) * MOE_FB
MOE_STEPS = MOE_NK + MOE_NB
DMA_UNROLL = 8
assert 0 < MOE_FB_LAST <= MOE_FB and MOE_FB % LANES == 0 and MOE_FB_LAST % LANES == 0


def _moe_kernel(ie_ref, is_ref, in_ref, ord_ref,
                xn_hbm, w1_ref, w3_ref, w2_ref,
                y_hbm, xf_ref, xb_ref, wcat_ref, ab_ref, h_ref, acc_ref, sem_in, sem_out):
    i = pl.program_id(0)
    j = pl.program_id(1)
    n_items = pl.num_programs(0)
    n_tokens = xn_hbm.shape[0]
    n = in_ref[i]
    slot = lax.rem(i, 2)

    def row_in(item, sl, r):
        tok = lax.shift_right_logical(ord_ref[is_ref[item] + r], 1)
        return pltpu.make_async_copy(xn_hbm.at[pl.ds(tok, 1), :], xf_ref.at[sl, pl.ds(r, 1), :], sem_in.at[sl])

    def row_out(item, r):
        a = ord_ref[is_ref[item] + r]
        dst = (a & 1) * n_tokens + lax.shift_right_logical(a, 1)
        return pltpu.make_async_copy(acc_ref.at[pl.ds(r, 1), :], y_hbm.at[pl.ds(dst, 1), :], sem_out)

    def for_rows(count, fn):
        groups = lax.div(count, DMA_UNROLL)

        def group(g, c):
            for u in range(DMA_UNROLL):
                fn(g * DMA_UNROLL + u)
            return c
        lax.fori_loop(0, groups, group, 0)

        def single(r, c):
            fn(r)
            return c
        lax.fori_loop(groups * DMA_UNROLL, count, single, 0)

    def wait_rows(count, rows_desc):
        groups = lax.div(count, DMA_UNROLL)

        def group(g, c):
            rows_desc(pl.multiple_of(g * DMA_UNROLL, DMA_UNROLL), DMA_UNROLL).wait()
            return c
        lax.fori_loop(0, groups, group, 0)

        def single(r, c):
            rows_desc(r, 1).wait()
            return c
        lax.fori_loop(groups * DMA_UNROLL, count, single, 0)

    def rows_in(sl, r0, size):
        return pltpu.make_async_copy(xn_hbm.at[pl.ds(0, size), :], xf_ref.at[sl, pl.ds(r0, size), :], sem_in.at[sl])

    def rows_out(r0, size):
        return pltpu.make_async_copy(acc_ref.at[pl.ds(r0, size), :], y_hbm.at[pl.ds(0, size), :], sem_out)

    def for_row_tiles(fn):
        tiles = lax.div(n + (MOE_RH - 1), MOE_RH)
        for k in range(1, MOE_CAP // MOE_RH + 1):
            @pl.when(tiles == k)
            def _():
                fn(0, k * MOE_RH)

    @pl.when((i == 0) & (j == 0))
    def _():
        xf_ref[...] = jnp.zeros(xf_ref.shape, F32)
        for_rows(n, lambda r: row_in(0, 0, r).start())

    @pl.when((n > 0) & (j == 0))
    def _():
        wait_rows(n, functools.partial(rows_in, slot))

        @pl.when(i > 0)
        def _():
            wait_rows(in_ref[i - 1], rows_out)

        for t in range(MOE_CAP // MOE_RH):
            @pl.when(t * MOE_RH < n)
            def _():
                sl = pl.ds(t * MOE_RH, MOE_RH)
                for kc in range(MOE_NK):
                    xb_ref[kc, sl, :] = xf_ref[slot, sl, kc * MOE_KC:(kc + 1) * MOE_KC].astype(BF16)
                ab_ref[sl, :] = jnp.zeros((MOE_RH, 2 * EXPERT_FF), F32)

    @pl.when((n > 0) & (j == 1) & (i + 1 < n_items))
    def _():
        nxt = jnp.minimum(i + 1, n_items - 1)
        for_rows(in_ref[nxt], lambda r: row_in(nxt, 1 - slot, r).start())

    @pl.when((n > 0) & (j < MOE_NK))
    def _():
        wcat_ref[:, :EXPERT_FF] = w1_ref[...].astype(BF16)
        wcat_ref[:, EXPERT_FF:] = w3_ref[...].astype(BF16)
        kc = jnp.minimum(j, MOE_NK - 1)

        def up(off, size):
            ab_ref[pl.ds(off, size), :] += _dot(xb_ref[kc, pl.ds(off, size), :], wcat_ref[...])
        for_row_tiles(up)

    @pl.when((n > 0) & (j == MOE_NK))
    def _():
        def act(off, size):
            a = ab_ref[pl.ds(off, size), :EXPERT_FF]
            b = ab_ref[pl.ds(off, size), EXPERT_FF:]
            half_a = 0.5 * a
            h_ref[pl.ds(off, size), :] = ((half_a + half_a * jnp.tanh(half_a)) * b).astype(BF16)
        for_row_tiles(act)
        w2 = w2_ref[...].astype(BF16)

        def down(off, size):
            acc_ref[pl.ds(off, size), :] = _dot(h_ref[pl.ds(off, size), :MOE_FB], w2)
        for_row_tiles(down)

    @pl.when((n > 0) & (j == MOE_NK + 1))
    def _():
        w2 = w2_ref[:MOE_FB_LAST, :].astype(BF16)

        def down(off, size):
            acc_ref[pl.ds(off, size), :] += _dot(h_ref[pl.ds(off, size), MOE_FB:], w2)
        for_row_tiles(down)
        for_rows(n, lambda r: row_out(i, r).start())
        nxt = jnp.minimum(i + 1, n_items - 1)

        @pl.when((i == n_items - 1) | (in_ref[nxt] == 0))
        def _():
            wait_rows(n, rows_out)


def _moe_experts(xn, w1, w3, w2, item_e, item_start, item_n, order):
    S = xn.shape[0]
    n_items = item_e.shape[0]

    def up_idx(i, j, ie, is_, n_, o):
        return ie[i], jnp.where(n_[i] > 0, jnp.minimum(j, MOE_NK - 1), MOE_NK - 1), 0

    def down_idx(i, j, ie, is_, n_, o):
        return ie[i], jnp.where(n_[i] > 0, jnp.maximum(j - MOE_NK, 0), MOE_NB - 1), 0

    up = pl.BlockSpec((None, MOE_KC, EXPERT_FF), up_idx)
    down = pl.BlockSpec((None, MOE_FB, D_MODEL), down_idx)
    grid_spec = pltpu.PrefetchScalarGridSpec(
        num_scalar_prefetch=4,
        grid=(n_items, MOE_STEPS),
        in_specs=[pl.BlockSpec(memory_space=pl.ANY), up, up, down],
        out_specs=pl.BlockSpec(memory_space=pl.ANY),
        scratch_shapes=[
            pltpu.VMEM((2, MOE_CAP, D_MODEL), F32),
            pltpu.VMEM((MOE_NK, MOE_CAP, MOE_KC), BF16),
            pltpu.VMEM((MOE_KC, 2 * EXPERT_FF), BF16),
            pltpu.VMEM((MOE_CAP, 2 * EXPERT_FF), F32),
            pltpu.VMEM((MOE_CAP, EXPERT_FF), BF16),
            pltpu.VMEM((MOE_CAP, D_MODEL), F32),
            pltpu.SemaphoreType.DMA((2,)),
            pltpu.SemaphoreType.DMA(()),
        ],
    )
    return pl.pallas_call(
        _moe_kernel,
        grid_spec=grid_spec,
        out_shape=jax.ShapeDtypeStruct((TOP_K * S, D_MODEL), F32),
        compiler_params=_cparams(("arbitrary", "arbitrary")),
        name="moe_experts",
    )(item_e, item_start, item_n, order, xn, w1, w3, w2)


def _moe_items(eid, cap):
    A = eid.size
    flat_e = eid.reshape(A)
    order = jnp.argsort(flat_e, stable=True).astype(jnp.int32)
    experts = jnp.arange(N_EXPERTS + 1, dtype=jnp.int32)
    starts = jnp.sum((flat_e[None, :] < experts[:, None]).astype(jnp.int32), axis=1)
    counts = starts[1:] - starts[:-1]
    per_e = (counts + cap - 1) // cap
    item_end = jnp.cumsum(per_e)
    total = item_end[-1]
    n_items = N_EXPERTS + A // cap
    idx = jnp.arange(n_items, dtype=jnp.int32)
    clamped = jnp.minimum(idx, total - 1)
    e = jnp.minimum(jnp.searchsorted(item_end, clamped, side="right"), N_EXPERTS - 1).astype(jnp.int32)
    local = clamped - (item_end[e] - per_e[e])
    used = idx < total
    item_start = jnp.where(used, starts[e] + local * cap, 0).astype(jnp.int32)
    item_n = jnp.where(used, jnp.clip(counts[e] - local * cap, 0, cap), 0).astype(jnp.int32)
    return e, item_start, item_n, order


def _combine_kernel(x1_ref, y0_ref, y1_ref, g_ref, o_ref):
    g = g_ref[...]
    o_ref[...] = x1_ref[...] + (g[:, 0:1] * y0_ref[...] + g[:, 1:2] * y1_ref[...])


def _combine(x1, y, gates, tm=256):
    S = x1.shape[0]
    nb = S // tm
    return pl.pallas_call(
        _combine_kernel,
        grid=(nb,),
        in_specs=[
            pl.BlockSpec((tm, D_MODEL), lambda i: (i, 0)),
            pl.BlockSpec((tm, D_MODEL), lambda i: (i, 0)),
            pl.BlockSpec((tm, D_MODEL), lambda i: (nb + i, 0)),
            pl.BlockSpec((tm, LANES), lambda i: (i, 0)),
        ],
        out_specs=pl.BlockSpec((tm, D_MODEL), lambda i: (i, 0)),
        out_shape=jax.ShapeDtypeStruct((S, D_MODEL), F32),
        compiler_params=_cparams(("parallel",)),
        name="moe_combine",
    )(x1, y, y, gates)


def _rot_half_cols(w, half):
    return jnp.concatenate([-w[..., half:], w[..., :half]], axis=-1)


def _rope_tables(S):
    pos = np.arange(S, dtype=np.float32)[:, None]

    def cs(half):
        inv = (np.float32(ROPE_THETA) ** (-np.arange(half, dtype=np.float32) / np.float32(half))).astype(np.float32)
        ang = (pos * inv[None, :]).astype(np.float64)
        return np.cos(ang).astype(np.float32), np.sin(ang).astype(np.float32)

    c32, s32 = cs(MLA_ROPE_DIM // 2)
    t64 = np.concatenate([c32, c32, s32, s32], axis=-1)
    c64, s64 = cs(DIL_HEAD_DIM // 2)
    cos128 = np.concatenate([c64, c64], axis=-1)
    sin128 = np.concatenate([-s64, s64], axis=-1)
    return jnp.asarray(t64), jnp.asarray(cos128), jnp.asarray(sin128)


def _residue_perm(dil):
    n = PERM_TILE // dil
    rows = np.arange(PERM_TILE)
    src = (rows % n) * dil + rows // n
    return jnp.asarray(src[:, None] == np.arange(PERM_TILE)[None, :], dtype=BF16)


def kernel(x, norm1_g, w_in, g_cq, g_ckv, w_uq, w_ukv, mla_q_norm_g, mla_k_norm_g, dil_q_norm_g,
           dil_k_norm_g, w_out, norm2_g, w_group, b_group, w_expert, b_expert, w1, w3, w2):
    B, S, D = x.shape
    assert B == 1 and D == D_MODEL and norm1_g.shape[0] == 1
    x2 = x.reshape(S, D)
    half_r = MLA_ROPE_DIM // 2
    o1, o2 = Q_LORA_RANK, Q_LORA_RANK + KV_LORA_RANK
    o3 = o2 + MLA_ROPE_DIM

    wi = w_in[0]
    w_kr = wi[:, o2:o3]
    w_z = jnp.concatenate([wi[:, o3:], wi[:, :o2], w_kr, _rot_half_cols(w_kr, half_r)], axis=-1).astype(BF16)

    wq = w_uq[0].reshape(Q_LORA_RANK, N_MLA_HEADS, MLA_QK_DIM)
    wq_pe = wq[..., MLA_NOPE_DIM:]
    wq_ext = jnp.concatenate([wq, _rot_half_cols(wq_pe, half_r)], axis=-1)
    wq_ext = wq_ext.reshape(Q_LORA_RANK, N_MLA_HEADS * MLA_PAD).astype(BF16)
    wkv = w_ukv[0].astype(BF16)

    def ext_gain(g):
        pe = g[MLA_NOPE_DIM:]
        return jnp.concatenate([g, pe[half_r:], pe[:half_r]])[None, :]

    gq_ext = ext_gain(mla_q_norm_g[0])
    gk_ext = ext_gain(mla_k_norm_g[0])
    t64, cos128, sin128 = _rope_tables(S)
    p4, p16 = _residue_perm(4), _residue_perm(16)

    z = _inproj(x2, norm1_g, w_z)
    q, k, vt, dq, dk, dq4, dk4, dv4, dq16, dk16, dv16 = _prep(
        z, g_cq, g_ckv, wq_ext, wkv, gq_ext, gk_ext[:, :LANES], gk_ext[:, LANES:],
        dil_q_norm_g, dil_k_norm_g, t64, cos128, sin128, p4, p16)
    mla_o = _mla_attn(q, k, vt)

    (w_1, d_1), (w_4, d_4), (w_16, d_16) = DIL_PATTERNS
    dil1 = _dil_pattern(dq[None], dk[None], z[None], Z_DV, w_1, d_1, hb=2, gb=4)
    dil4 = _dil_pattern(dq4, dk4, dv4, 0, w_4, d_4, hb=8, gb=1)
    dil16 = _dil_pattern(dq16, dk16, dv16, 0, w_16, d_16, hb=8, gb=1)

    pad = LANES - N_EXPERT_GROUPS - N_EXPERTS
    w_router = jnp.concatenate([w_group[0], w_expert[0], jnp.zeros((D, pad), F32)], axis=-1)
    b_router = jnp.concatenate([b_group[0], b_expert[0], jnp.zeros((pad,), F32)])[None, :]
    wr_hi = w_router.astype(BF16)
    wr_lo = (w_router - wr_hi.astype(F32)).astype(BF16)
    x1, xn, route_i, route_f = _outproj_router(
        mla_o, dil1, dil4, dil16, p4.T, p16.T, x2, w_out[0].astype(BF16), norm2_g,
        jnp.concatenate([wr_hi, wr_lo], axis=-1), wr_hi, b_router)

    item_e, item_start, item_n, order = _moe_items(route_i[:, :TOP_K], MOE_CAP)
    y = _moe_experts(xn, w1[0], w3[0], w2[0], item_e, item_start, item_n, order)
    out = _combine(x1, y, route_f)
    return out.reshape(B, S, D)
```

```python
import functools
import math

import jax
import jax.numpy as jnp
import numpy as np
from jax import lax
from jax.experimental import pallas as pl
from jax.experimental.pallas import tpu as pltpu

D_MODEL = 2048
N_MLA_HEADS = 8
MLA_NOPE_DIM = 128
MLA_ROPE_DIM = 64
MLA_QK_DIM = MLA_NOPE_DIM + MLA_ROPE_DIM
MLA_V_DIM = 128
Q_LORA_RANK = 512
KV_LORA_RANK = 512
N_DIL_HEADS = 8
DIL_HEAD_DIM = 128
DIL_PATTERNS = ((128, 1), (512, 4), (2048, 16))
ROPE_THETA = 10000.0
NORM_EPS = 1e-6
NEG_INF = -1e30
N_EXPERT_GROUPS = 8
EXPERTS_PER_GROUP = 8
N_EXPERTS = N_EXPERT_GROUPS * EXPERTS_PER_GROUP
TOP_K = 2
EXPERT_FF = 1408

LANES = 128
BF16_ROWS = 16
MXU_DIM = 256
HD = N_DIL_HEADS * DIL_HEAD_DIM
MLA_PAD = MXU_DIM
Z_DQ, Z_DK, Z_DV = 0, HD, 2 * HD
Z_CQ = 3 * HD
Z_CKV = Z_CQ + Q_LORA_RANK
Z_KR = Z_CKV + KV_LORA_RANK
Z_WIDTH = Z_KR + 2 * MLA_ROPE_DIM

ATT_TK = 512
ATT_TQ = 1024
ATT_UNROLL = 4
VT_ROWS = MLA_V_DIM + BF16_ROWS
PERM_TILE = 256
LOG2E = math.log2(math.e)

VMEM_LIMIT = 56 * 1024 * 1024

F32 = jnp.float32
BF16 = jnp.bfloat16


def _cparams(sem, vmem=VMEM_LIMIT):
    return pltpu.CompilerParams(dimension_semantics=sem, vmem_limit_bytes=vmem)


def _dot(a, b):
    return jnp.dot(a, b, preferred_element_type=F32)


def _dot_nt(a, b):
    return lax.dot_general(a, b, (((1,), (1,)), ((), ())), preferred_element_type=F32)


def _inproj_kernel(x_ref, g_ref, w_ref, z_ref):
    x = x_ref[...]
    r = lax.rsqrt(jnp.mean(x * x, axis=-1, keepdims=True) + NORM_EPS)
    h = (x * r * g_ref[...]).astype(BF16)
    z_ref[...] = _dot(h, w_ref[...]).astype(BF16)


def _inproj(x2, g1, w_z, tm=256):
    S = x2.shape[0]
    return pl.pallas_call(
        _inproj_kernel,
        grid=(S // tm,),
        in_specs=[
            pl.BlockSpec((tm, D_MODEL), lambda i: (i, 0)),
            pl.BlockSpec((1, D_MODEL), lambda i: (0, 0)),
            pl.BlockSpec((D_MODEL, Z_WIDTH), lambda i: (0, 0)),
        ],
        out_specs=pl.BlockSpec((tm, Z_WIDTH), lambda i: (i, 0)),
        out_shape=jax.ShapeDtypeStruct((S, Z_WIDTH), BF16),
        compiler_params=_cparams(("parallel",)),
        name="inproj",
    )(x2, g1, w_z)


def _prep_kernel(zdq_ref, zdk_ref, zdv_ref, zcq_ref, zckv_ref, zkr_ref,
                 gcq_ref, gckv_ref, wq_ref, wkv_ref, gq_ref, gkn_ref, gkp_ref,
                 gdq_ref, gdk_ref, t64_ref, cos_ref, sin_ref, p4_ref, p16_ref,
                 q_ref, k_ref, vt_ref, dq_ref, dk_ref,
                 dq4_ref, dk4_ref, dv4_ref, dq16_ref, dk16_ref, dv16_ref):
    tm = zcq_ref.shape[0]
    lane = lax.broadcasted_iota(jnp.int32, (1, LANES), 1)
    first_half = lane < MLA_ROPE_DIM

    def rms_rows(c, g):
        c = c.astype(F32)
        r = lax.rsqrt(jnp.mean(c * c, axis=-1, keepdims=True) + NORM_EPS)
        return (c * r * g).astype(BF16)

    cq = rms_rows(zcq_ref[...], gcq_ref[...])
    ckv = rms_rows(zckv_ref[...], gckv_ref[...])
    qe = _dot(cq, wq_ref[...])
    kv = _dot(ckv, wkv_ref[...])
    kr = zkr_ref[...].astype(F32)
    kr_ss = jnp.sum(jnp.where(first_half, kr * kr, 0.0), axis=-1, keepdims=True)
    t64 = t64_ref[...]
    q_scale = MLA_QK_DIM ** -0.5 * LOG2E

    def rope64(ext):
        t = ext * t64
        return jnp.where(first_half, t + pltpu.roll(t, MLA_ROPE_DIM, 1), 0.0)

    ones_rows = (lax.broadcasted_iota(jnp.int32, (BF16_ROWS, tm), 0) == 0).astype(BF16)
    eye = (lax.broadcasted_iota(jnp.int32, (MLA_V_DIM, MLA_V_DIM), 0)
           == lax.broadcasted_iota(jnp.int32, (MLA_V_DIM, MLA_V_DIM), 1)).astype(BF16)
    for h in range(N_MLA_HEADS):
        base = h * MLA_PAD
        qn = qe[:, base:base + LANES]
        qp = qe[:, base + LANES:base + 2 * LANES]
        ss = (jnp.sum(qn * qn, axis=-1, keepdims=True)
              + jnp.sum(jnp.where(first_half, qp * qp, 0.0), axis=-1, keepdims=True))
        r = lax.rsqrt(ss * (1.0 / MLA_QK_DIM) + NORM_EPS) * q_scale
        q_ref[:, base:base + LANES] = (qn * r * gq_ref[:, :LANES]).astype(BF16)
        q_ref[:, base + LANES:base + MLA_PAD] = rope64(qp * r * gq_ref[:, LANES:]).astype(BF16)

        kn = kv[:, base:base + LANES]
        ss = jnp.sum(kn * kn, axis=-1, keepdims=True) + kr_ss
        r = lax.rsqrt(ss * (1.0 / MLA_QK_DIM) + NORM_EPS)
        k_ref[:, base:base + LANES] = (kn * r * gkn_ref[...]).astype(BF16)
        k_ref[:, base + LANES:base + MLA_PAD] = rope64(kr * r * gkp_ref[...]).astype(BF16)
        v_h = kv[:, base + LANES:base + 2 * LANES].astype(BF16)
        vt_ref[h, :MLA_V_DIM, :] = _dot_nt(eye, v_h).astype(BF16)
        vt_ref[h, MLA_V_DIM:, :] = ones_rows

    cos = cos_ref[...]
    sin = sin_ref[...]
    d_scale = DIL_HEAD_DIM ** -0.5

    def dil_head(x, g, scale):
        x = x.astype(F32)
        r = lax.rsqrt(jnp.mean(x * x, axis=-1, keepdims=True) + NORM_EPS)
        y = x * r * g
        return ((y * cos + pltpu.roll(y, DIL_HEAD_DIM // 2, 1) * sin) * scale).astype(BF16)

    for h in range(N_DIL_HEADS):
        sl = slice(h * DIL_HEAD_DIM, (h + 1) * DIL_HEAD_DIM)
        dq_ref[:, sl] = dil_head(zdq_ref[:, sl], gdq_ref[...], d_scale)
        dk_ref[:, sl] = dil_head(zdk_ref[:, sl], gdk_ref[...], 1.0)

    for src, d4, d16 in ((dq_ref, dq4_ref, dq16_ref), (dk_ref, dk4_ref, dk16_ref), (zdv_ref, dv4_ref, dv16_ref)):
        for sub in range(tm // PERM_TILE):
            xs = src[sub * PERM_TILE:(sub + 1) * PERM_TILE, :]
            for dil, p_ref, dst in ((4, p4_ref, d4), (16, p16_ref, d16)):
                n = PERM_TILE // dil
                xp = _dot(p_ref[...], xs).astype(BF16)
                for r in range(dil):
                    dst[r, sub * n:(sub + 1) * n, :] = xp[r * n:(r + 1) * n, :]


def _prep(z, gcq, gckv, wq_ext, wkv, gq_ext, gk_nope, gk_pe, gdq, gdk, t64, cos128, sin128, p4, p16):
    S = z.shape[0]
    tm = ATT_TK
    row = lambda w, j: pl.BlockSpec((tm, w), lambda i, j=j: (i, j))
    full = lambda a: pl.BlockSpec(a.shape, lambda i: (0, 0))
    res = lambda dil: pl.BlockSpec((dil, tm // dil, HD), lambda i: (0, i, 0))
    qk_w = N_MLA_HEADS * MLA_PAD
    res_shape = lambda dil: jax.ShapeDtypeStruct((dil, S // dil, HD), BF16)
    return pl.pallas_call(
        _prep_kernel,
        grid=(S // tm,),
        in_specs=[
            row(HD, Z_DQ // HD), row(HD, Z_DK // HD), row(HD, Z_DV // HD),
            row(Q_LORA_RANK, Z_CQ // Q_LORA_RANK), row(KV_LORA_RANK, Z_CKV // KV_LORA_RANK),
            row(LANES, Z_KR // LANES),
            full(gcq), full(gckv), full(wq_ext), full(wkv), full(gq_ext), full(gk_nope), full(gk_pe),
            full(gdq), full(gdk),
            row(LANES, 0), row(LANES, 0), row(LANES, 0), full(p4), full(p16),
        ],
        out_specs=[row(qk_w, 0), row(qk_w, 0),
                   pl.BlockSpec((N_MLA_HEADS, None, VT_ROWS, tm), lambda i: (0, i, 0, 0)),
                   row(HD, 0), row(HD, 0),
                   res(4), res(4), res(4), res(16), res(16), res(16)],
        out_shape=[
            jax.ShapeDtypeStruct((S, qk_w), BF16), jax.ShapeDtypeStruct((S, qk_w), BF16),
            jax.ShapeDtypeStruct((N_MLA_HEADS, S // tm, VT_ROWS, tm), BF16),
            jax.ShapeDtypeStruct((S, HD), BF16), jax.ShapeDtypeStruct((S, HD), BF16),
            res_shape(4), res_shape(4), res_shape(4), res_shape(16), res_shape(16), res_shape(16),
        ],
        compiler_params=_cparams(("parallel",)),
        name="qkv_prep",
    )(z, z, z, z, z, z, gcq, gckv, wq_ext, wkv, gq_ext, gk_nope, gk_pe, gdq, gdk, t64, cos128, sin128, p4, p16)


def _mla_attn_kernel(q_ref, k_ref, vt_ref, o_ref, m_ref, acc_ref, s_ref):
    n_chunks = vt_ref.shape[0]
    q = q_ref[...]

    def scores(c):
        off = pl.multiple_of(c * ATT_TK, ATT_TK)
        return _dot_nt(k_ref[pl.ds(off, ATT_TK), :], q)

    m_ref[...] = jnp.full(m_ref.shape, -jnp.inf, F32)
    acc_ref[...] = jnp.zeros(acc_ref.shape, F32)

    def fold(c, slot):
        s = s_ref[slot]
        m_old = m_ref[...]
        m_new = jnp.maximum(m_old, jnp.max(s, axis=0, keepdims=True))
        alpha = jnp.exp2(m_old - m_new)
        p = jnp.exp2(s - m_new).astype(BF16)
        acc_ref[...] = alpha * acc_ref[...] + _dot(vt_ref[c], p)
        m_ref[...] = m_new

    s_ref[0] = scores(0)

    def body(t, carry):
        for u in range(ATT_UNROLL):
            c = ATT_UNROLL * t + u
            s_ref[(u + 1) % 2] = scores(jnp.minimum(c + 1, n_chunks - 1))
            fold(c, u % 2)
        return carry

    lax.fori_loop(0, n_chunks // ATT_UNROLL, body, 0)
    acc = acc_ref[...]
    o_t = acc[:MLA_V_DIM, :] / acc[MLA_V_DIM:MLA_V_DIM + 1, :]
    o_ref[...] = o_t.T.astype(o_ref.dtype)


def _mla_attn(q, k, vt):
    S = q.shape[0]
    n_chunks = vt.shape[1]
    return pl.pallas_call(
        _mla_attn_kernel,
        grid=(N_MLA_HEADS, S // ATT_TQ),
        in_specs=[
            pl.BlockSpec((ATT_TQ, MLA_PAD), lambda h, i: (i, h)),
            pl.BlockSpec((S, MLA_PAD), lambda h, i: (0, h)),
            pl.BlockSpec((None, n_chunks, VT_ROWS, ATT_TK), lambda h, i: (h, 0, 0, 0)),
        ],
        out_specs=pl.BlockSpec((ATT_TQ, MLA_V_DIM), lambda h, i: (i, h)),
        out_shape=jax.ShapeDtypeStruct((S, N_MLA_HEADS * MLA_V_DIM), BF16),
        scratch_shapes=[pltpu.VMEM((1, ATT_TQ), F32), pltpu.VMEM((VT_ROWS, ATT_TQ), F32),
                        pltpu.VMEM((2, ATT_TK, ATT_TQ), F32)],
        compiler_params=_cparams(("parallel", "parallel")),
        name="mla_attn",
    )(q, k, vt)


DIL_QB = 128


def _dil_kernel(q_ref, k_ref, v_ref, o_ref, lse_ref, *, half_w, hb, gb):
    L = q_ref.shape[0]
    kw = DIL_QB + 2 * half_w
    g_id = pl.program_id(1)
    lane = lax.broadcasted_iota(jnp.int32, (DIL_QB, LANES), 1)

    @pl.when(g_id == 0)
    def _():
        lse_ref[...] = jnp.zeros(lse_ref.shape, F32)

    def body(step, carry):
        qs_l, ks_l, qb, kb, vb = [], [], [], [], []
        for j in range(gb):
            qs = pl.multiple_of((step * gb + j) * DIL_QB, DIL_QB)
            ks = pl.multiple_of(jnp.clip(qs - half_w, 0, L - kw), half_w)
            qs_l.append(qs)
            ks_l.append(ks)
            for h in range(hb):
                sl = slice(h * DIL_HEAD_DIM, (h + 1) * DIL_HEAD_DIM)
                qb.append(q_ref[pl.ds(qs, DIL_QB), sl])
                kb.append(k_ref[pl.ds(ks, kw), sl])
                vb.append(v_ref[pl.ds(ks, kw), sl])
        q = jnp.stack(qb)
        k = jnp.stack(kb)
        v = jnp.stack(vb)
        s = jnp.einsum("gqd,gkd->gqk", q, k, preferred_element_type=F32)
        rel = (lax.broadcasted_iota(jnp.int32, (DIL_QB, kw), 0)
               - lax.broadcasted_iota(jnp.int32, (DIL_QB, kw), 1))
        bias = []
        for j in range(gb):
            mask = jnp.abs(rel + (qs_l[j] - ks_l[j])) <= half_w
            bias += [jnp.where(mask, 0.0, NEG_INF)] * hb
        s = s + jnp.stack(bias)
        m = jnp.max(s, axis=-1, keepdims=True)
        p = jnp.exp(s - m)
        den = jnp.sum(p, axis=-1, keepdims=True)
        o = jnp.einsum("gqk,gkd->gqd", p.astype(BF16), v, preferred_element_type=F32) / den
        lse = m + jnp.log(den)
        for j in range(gb):
            tile = lse_ref[pl.ds(qs_l[j], DIL_QB), :]
            for h in range(hb):
                sl = slice(h * DIL_HEAD_DIM, (h + 1) * DIL_HEAD_DIM)
                o_ref[pl.ds(qs_l[j], DIL_QB), sl] = o[j * hb + h].astype(o_ref.dtype)
                tile = jnp.where(lane == g_id * hb + h, lse[j * hb + h], tile)
            lse_ref[pl.ds(qs_l[j], DIL_QB), :] = tile
        return carry

    lax.fori_loop(0, L // (DIL_QB * gb), body, 0)


def _dil_pattern(dq, dk, dv, v_col0, window, dil, hb, gb):
    L = dq.shape[1]
    half_w = window // (2 * dil)
    w = hb * DIL_HEAD_DIM
    spec = lambda c0: pl.BlockSpec((None, L, w), lambda r, g, c0=c0: (r, 0, c0 + g))
    return pl.pallas_call(
        functools.partial(_dil_kernel, half_w=half_w, hb=hb, gb=gb),
        grid=(dil, N_DIL_HEADS // hb),
        in_specs=[spec(0), spec(0), spec(v_col0 // w)],
        out_specs=[spec(0), pl.BlockSpec((None, L, LANES), lambda r, g: (r, 0, 0))],
        out_shape=[jax.ShapeDtypeStruct((dil, L, HD), BF16), jax.ShapeDtypeStruct((dil, L, LANES), F32)],
        compiler_params=_cparams(("parallel", "arbitrary")),
        name=f"dil_attn_d{dil}",
    )(dq, dk, dv)


def _outproj_kernel(a_ref, o1_ref, l1_ref, o4_ref, l4_ref, o16_ref, l16_ref, p4t_ref, p16t_ref,
                    x_ref, wa_ref, wb_ref, g_ref, wr_ref, wrh_ref, br_ref,
                    x1_ref, xn_ref, ri_ref, rf_ref):
    tm = x_ref.shape[0]

    def to_token_order(o_ref, l_ref, pt_ref):
        pt = pt_ref[...]
        o = _dot(pt, o_ref[...].reshape(tm, HD))
        lse = l_ref[...].reshape(tm, LANES)
        hi = lse.astype(BF16)
        rem = lse - hi.astype(F32)
        mid = rem.astype(BF16)
        lo = (rem - mid.astype(F32)).astype(BF16)
        return o, _dot(pt, hi) + _dot(pt, mid) + _dot(pt, lo)

    o1 = o1_ref[...].astype(F32)
    l1 = l1_ref[...]
    o4, l4 = to_token_order(o4_ref, l4_ref, p4t_ref)
    o16, l16 = to_token_order(o16_ref, l16_ref, p16t_ref)
    big = jnp.maximum(jnp.maximum(l1, l4), l16)
    e1 = jnp.exp(l1 - big)
    e4 = jnp.exp(l4 - big)
    e16 = jnp.exp(l16 - big)
    inv = 1.0 / (e1 + e4 + e16)
    w1, w4, w16 = e1 * inv, e4 * inv, e16 * inv
    slabs = []
    for h in range(N_DIL_HEADS):
        sl = slice(h * DIL_HEAD_DIM, (h + 1) * DIL_HEAD_DIM)
        slabs.append(w1[:, h:h + 1] * o1[:, sl] + w4[:, h:h + 1] * o4[:, sl] + w16[:, h:h + 1] * o16[:, sl])
    dil_o = jnp.concatenate(slabs, axis=-1).astype(BF16)

    x1 = x_ref[...] + _dot(a_ref[...], wa_ref[...]) + _dot(dil_o, wb_ref[...])
    x1_ref[...] = x1
    r = lax.rsqrt(jnp.mean(x1 * x1, axis=-1, keepdims=True) + NORM_EPS)
    xn = x1 * r * g_ref[...]
    xn_ref[...] = xn
    xh = xn.astype(BF16)
    xl = (xn - xh.astype(F32)).astype(BF16)
    two = _dot(xh, wr_ref[...])
    logits = two[:, :LANES] + two[:, LANES:] + _dot(xl, wrh_ref[...]) + br_ref[...]
    lane = lax.broadcasted_iota(jnp.int32, logits.shape, 1)
    ninf = -jnp.inf

    def first_argmax(vals, vmax):
        return jnp.min(jnp.where(vals == vmax, lane, LANES), axis=-1, keepdims=True)

    coarse = jnp.where(lane < N_EXPERT_GROUPS, logits, ninf)
    cmax = jnp.max(coarse, axis=-1, keepdims=True)
    g = first_argmax(coarse, cmax)
    p_g = 1.0 / jnp.sum(jnp.exp(coarse - cmax), axis=-1, keepdims=True)
    lo_lane = N_EXPERT_GROUPS + g * EXPERTS_PER_GROUP
    fine = jnp.where((lane >= lo_lane) & (lane < lo_lane + EXPERTS_PER_GROUP), logits, ninf)
    v1 = jnp.max(fine, axis=-1, keepdims=True)
    j1 = first_argmax(fine, v1)
    fine2 = jnp.where(lane == j1, ninf, fine)
    v2 = jnp.max(fine2, axis=-1, keepdims=True)
    j2 = first_argmax(fine2, v2)
    e2 = jnp.exp(v2 - v1)
    g1 = 1.0 / (1.0 + e2)
    g2 = e2 / (1.0 + e2)
    ri_ref[...] = jnp.where(lane == 0, j1 - N_EXPERT_GROUPS, jnp.where(lane == 1, j2 - N_EXPERT_GROUPS, 0))
    rf_ref[...] = jnp.where(lane == 0, p_g * g1, jnp.where(lane == 1, p_g * g2, 0.0))


def _outproj_router(mla_o, dil1, dil4, dil16, p4t, p16t, x2, w_out_bf, g2, wr_two, wr_hi, b_router):
    S = x2.shape[0]
    tm = PERM_TILE
    half = N_MLA_HEADS * MLA_V_DIM
    row = lambda w: pl.BlockSpec((tm, w), lambda i: (i, 0))
    res = lambda dil, w: pl.BlockSpec((dil, tm // dil, w), lambda i: (0, i, 0))
    const = lambda a: pl.BlockSpec(a.shape, lambda i: (0, 0))
    return pl.pallas_call(
        _outproj_kernel,
        grid=(S // tm,),
        in_specs=[
            row(half),
            pl.BlockSpec((None, tm, HD), lambda i: (0, i, 0)), pl.BlockSpec((None, tm, LANES), lambda i: (0, i, 0)),
            res(4, HD), res(4, LANES), res(16, HD), res(16, LANES), const(p4t), const(p16t),
            row(D_MODEL),
            pl.BlockSpec((half, D_MODEL), lambda i: (0, 0)),
            pl.BlockSpec((HD, D_MODEL), lambda i: (1, 0)),
            const(g2), const(wr_two), const(wr_hi), const(b_router),
        ],
        out_specs=[row(D_MODEL), row(D_MODEL), row(LANES), row(LANES)],
        out_shape=[
            jax.ShapeDtypeStruct((S, D_MODEL), F32), jax.ShapeDtypeStruct((S, D_MODEL), F32),
            jax.ShapeDtypeStruct((S, LANES), jnp.int32), jax.ShapeDtypeStruct((S, LANES), F32),
        ],
        compiler_params=_cparams(("parallel",)),
        name="outproj_router",
    )(mla_o, dil1[0], dil1[1], dil4[0], dil4[1], dil16[0], dil16[1], p4t, p16t,
      x2, w_out_bf, w_out_bf, g2, wr_two, wr_hi, b_router)


MOE_CAP = 512
MOE_RH = 128
MOE_KC = 512
MOE_NK = D_MODEL // MOE_KC
MOE_FB = 768
MOE_NB = 2
MOE_FB_LAST = EXPERT_FF - (MOE_NB - 1) * MOE_FB
MOE_STEPS = MOE_NK + MOE_NB
DMA_UNROLL = 8
assert 0 < MOE_FB_LAST <= MOE_FB and MOE_FB % LANES == 0 and MOE_FB_LAST % LANES == 0


def _moe_kernel(ie_ref, is_ref, in_ref, ord_ref,
                xn_hbm, w1_ref, w3_ref, w2_ref,
                y_hbm, xf_ref, xb_ref, wcat_ref, ab_ref, h_ref, acc_ref, sem_in, sem_out):
    i = pl.program_id(0)
    j = pl.program_id(1)
    n_items = pl.num_programs(0)
    n_tokens = xn_hbm.shape[0]
    n = in_ref[i]
    slot = lax.rem(i, 2)

    def row_in(item, sl, base, u, width):
        tok = lax.shift_right_logical(ord_ref[is_ref[item] + base + u], 1)
        dst = xf_ref.at[sl, pl.ds(base, width), :].at[pl.ds(u, 1), :]
        return pltpu.make_async_copy(xn_hbm.at[pl.ds(tok, 1), :], dst, sem_in.at[sl])

    def row_out(item, base, u, width):
        a = ord_ref[is_ref[item] + base + u]
        dst = (a & 1) * n_tokens + lax.shift_right_logical(a, 1)
        src = acc_ref.at[pl.ds(base, width), :].at[pl.ds(u, 1), :]
        return pltpu.make_async_copy(src, y_hbm.at[pl.ds(dst, 1), :], sem_out)

    def for_rows(count, fn):
        groups = lax.div(count, DMA_UNROLL)

        def group(g, c):
            base = pl.multiple_of(g * DMA_UNROLL, DMA_UNROLL)
            for u in range(DMA_UNROLL):
                fn(base, u, DMA_UNROLL)
            return c
        lax.fori_loop(0, groups, group, 0)

        def single(r, c):
            fn(r, 0, 1)
            return c
        lax.fori_loop(groups * DMA_UNROLL, count, single, 0)

    def wait_rows(count, rows_desc):
        groups = lax.div(count, DMA_UNROLL)

        def group(g, c):
            rows_desc(pl.multiple_of(g * DMA_UNROLL, DMA_UNROLL), DMA_UNROLL).wait()
            return c
        lax.fori_loop(0, groups, group, 0)

        def single(r, c):
            rows_desc(r, 1).wait()
            return c
        lax.fori_loop(groups * DMA_UNROLL, count, single, 0)

    def rows_in(sl, r0, size):
        return pltpu.make_async_copy(xn_hbm.at[pl.ds(0, size), :], xf_ref.at[sl, pl.ds(r0, size), :], sem_in.at[sl])

    def rows_out(r0, size):
        return pltpu.make_async_copy(acc_ref.at[pl.ds(r0, size), :], y_hbm.at[pl.ds(0, size), :], sem_out)

    def for_row_tiles(fn):
        tiles = lax.div(n + (MOE_RH - 1), MOE_RH)
        for k in range(1, MOE_CAP // MOE_RH + 1):
            @pl.when(tiles == k)
            def _():
                fn(0, k * MOE_RH)

    @pl.when((i == 0) & (j == 0))
    def _():
        xf_ref[...] = jnp.zeros(xf_ref.shape, F32)
        for_rows(n, lambda *row: row_in(0, 0, *row).start())

    @pl.when((n > 0) & (j == 0))
    def _():
        wait_rows(n, functools.partial(rows_in, slot))

        @pl.when(i > 0)
        def _():
            wait_rows(in_ref[i - 1], rows_out)

        for t in range(MOE_CAP // MOE_RH):
            @pl.when(t * MOE_RH < n)
            def _():
                sl = pl.ds(t * MOE_RH, MOE_RH)
                for kc in range(MOE_NK):
                    xb_ref[kc, sl, :] = xf_ref[slot, sl, kc * MOE_KC:(kc + 1) * MOE_KC].astype(BF16)
                ab_ref[sl, :] = jnp.zeros((MOE_RH, 2 * EXPERT_FF), F32)

    @pl.when((n > 0) & (j == 1) & (i + 1 < n_items))
    def _():
        nxt = jnp.minimum(i + 1, n_items - 1)
        for_rows(in_ref[nxt], lambda *row: row_in(nxt, 1 - slot, *row).start())

    @pl.when((n > 0) & (j < MOE_NK))
    def _():
        wcat_ref[:, :EXPERT_FF] = w1_ref[...].astype(BF16)
        wcat_ref[:, EXPERT_FF:] = w3_ref[...].astype(BF16)
        kc = jnp.minimum(j, MOE_NK - 1)

        def up(off, size):
            ab_ref[pl.ds(off, size), :] += _dot(xb_ref[kc, pl.ds(off, size), :], wcat_ref[...])
        for_row_tiles(up)

    @pl.when((n > 0) & (j == MOE_NK))
    def _():
        def act(off, size):
            a = ab_ref[pl.ds(off, size), :EXPERT_FF]
            b = ab_ref[pl.ds(off, size), EXPERT_FF:]
            half_a = 0.5 * a
            h_ref[pl.ds(off, size), :] = ((half_a + half_a * jnp.tanh(half_a)) * b).astype(BF16)
        for_row_tiles(act)
        w2 = w2_ref[...].astype(BF16)

        def down(off, size):
            acc_ref[pl.ds(off, size), :] = _dot(h_ref[pl.ds(off, size), :MOE_FB], w2)
        for_row_tiles(down)

    @pl.when((n > 0) & (j == MOE_NK + 1))
    def _():
        w2 = w2_ref[:MOE_FB_LAST, :].astype(BF16)

        def down(off, size):
            acc_ref[pl.ds(off, size), :] += _dot(h_ref[pl.ds(off, size), MOE_FB:], w2)
        for_row_tiles(down)
        for_rows(n, lambda *row: row_out(i, *row).start())
        nxt = jnp.minimum(i + 1, n_items - 1)

        @pl.when((i == n_items - 1) | (in_ref[nxt] == 0))
        def _():
            wait_rows(n, rows_out)


def _moe_experts(xn, w1, w3, w2, item_e, item_start, item_n, order):
    S = xn.shape[0]
    n_items = item_e.shape[0]

    def up_idx(i, j, ie, is_, n_, o):
        return ie[i], jnp.where(n_[i] > 0, jnp.minimum(j, MOE_NK - 1), MOE_NK - 1), 0

    def down_idx(i, j, ie, is_, n_, o):
        return ie[i], jnp.where(n_[i] > 0, jnp.maximum(j - MOE_NK, 0), MOE_NB - 1), 0

    up = pl.BlockSpec((None, MOE_KC, EXPERT_FF), up_idx)
    down = pl.BlockSpec((None, MOE_FB, D_MODEL), down_idx)
    grid_spec = pltpu.PrefetchScalarGridSpec(
        num_scalar_prefetch=4,
        grid=(n_items, MOE_STEPS),
        in_specs=[pl.BlockSpec(memory_space=pl.ANY), up, up, down],
        out_specs=pl.BlockSpec(memory_space=pl.ANY),
        scratch_shapes=[
            pltpu.VMEM((2, MOE_CAP, D_MODEL), F32),
            pltpu.VMEM((MOE_NK, MOE_CAP, MOE_KC), BF16),
            pltpu.VMEM((MOE_KC, 2 * EXPERT_FF), BF16),
            pltpu.VMEM((MOE_CAP, 2 * EXPERT_FF), F32),
            pltpu.VMEM((MOE_CAP, EXPERT_FF), BF16),
            pltpu.VMEM((MOE_CAP, D_MODEL), F32),
            pltpu.SemaphoreType.DMA((2,)),
            pltpu.SemaphoreType.DMA(()),
        ],
    )
    return pl.pallas_call(
        _moe_kernel,
        grid_spec=grid_spec,
        out_shape=jax.ShapeDtypeStruct((TOP_K * S, D_MODEL), F32),
        compiler_params=_cparams(("arbitrary", "arbitrary")),
        name="moe_experts",
    )(item_e, item_start, item_n, order, xn, w1, w3, w2)


def _moe_items(eid, cap):
    A = eid.size
    flat_e = eid.reshape(A)
    order = jnp.argsort(flat_e, stable=True).astype(jnp.int32)
    experts = jnp.arange(N_EXPERTS + 1, dtype=jnp.int32)
    starts = jnp.sum((flat_e[None, :] < experts[:, None]).astype(jnp.int32), axis=1)
    counts = starts[1:] - starts[:-1]
    per_e = (counts + cap - 1) // cap
    item_end = jnp.cumsum(per_e)
    total = item_end[-1]
    n_items = N_EXPERTS + A // cap
    idx = jnp.arange(n_items, dtype=jnp.int32)
    clamped = jnp.minimum(idx, total - 1)
    e = jnp.minimum(jnp.searchsorted(item_end, clamped, side="right"), N_EXPERTS - 1).astype(jnp.int32)
    local = clamped - (item_end[e] - per_e[e])
    used = idx < total
    item_start = jnp.where(used, starts[e] + local * cap, 0).astype(jnp.int32)
    item_n = jnp.where(used, jnp.clip(counts[e] - local * cap, 0, cap), 0).astype(jnp.int32)
    return e, item_start, item_n, order


def _combine_kernel(x1_ref, y0_ref, y1_ref, g_ref, o_ref):
    g = g_ref[...]
    o_ref[...] = x1_ref[...] + (g[:, 0:1] * y0_ref[...] + g[:, 1:2] * y1_ref[...])


def _combine(x1, y, gates, tm=256):
    S = x1.shape[0]
    nb = S // tm
    return pl.pallas_call(
        _combine_kernel,
        grid=(nb,),
        in_specs=[
            pl.BlockSpec((tm, D_MODEL), lambda i: (i, 0)),
            pl.BlockSpec((tm, D_MODEL), lambda i: (i, 0)),
            pl.BlockSpec((tm, D_MODEL), lambda i: (nb + i, 0)),
            pl.BlockSpec((tm, LANES), lambda i: (i, 0)),
        ],
        out_specs=pl.BlockSpec((tm, D_MODEL), lambda i: (i, 0)),
        out_shape=jax.ShapeDtypeStruct((S, D_MODEL), F32),
        compiler_params=_cparams(("parallel",)),
        name="moe_combine",
    )(x1, y, y, gates)


def _rot_half_cols(w, half):
    return jnp.concatenate([-w[..., half:], w[..., :half]], axis=-1)


def _rope_tables(S):
    pos = np.arange(S, dtype=np.float32)[:, None]

    def cs(half):
        inv = (np.float32(ROPE_THETA) ** (-np.arange(half, dtype=np.float32) / np.float32(half))).astype(np.float32)
        ang = (pos * inv[None, :]).astype(np.float64)
        return np.cos(ang).astype(np.float32), np.sin(ang).astype(np.float32)

    c32, s32 = cs(MLA_ROPE_DIM // 2)
    t64 = np.concatenate([c32, c32, s32, s32], axis=-1)
    c64, s64 = cs(DIL_HEAD_DIM // 2)
    cos128 = np.concatenate([c64, c64], axis=-1)
    sin128 = np.concatenate([-s64, s64], axis=-1)
    return jnp.asarray(t64), jnp.asarray(cos128), jnp.asarray(sin128)


def _residue_perm(dil):
    n = PERM_TILE // dil
    rows = np.arange(PERM_TILE)
    src = (rows % n) * dil + rows // n
    return jnp.asarray(src[:, None] == np.arange(PERM_TILE)[None, :], dtype=BF16)


def kernel(x, norm1_g, w_in, g_cq, g_ckv, w_uq, w_ukv, mla_q_norm_g, mla_k_norm_g, dil_q_norm_g,
           dil_k_norm_g, w_out, norm2_g, w_group, b_group, w_expert, b_expert, w1, w3, w2):
    B, S, D = x.shape
    assert B == 1 and D == D_MODEL and norm1_g.shape[0] == 1
    x2 = x.reshape(S, D)
    half_r = MLA_ROPE_DIM // 2
    o1, o2 = Q_LORA_RANK, Q_LORA_RANK + KV_LORA_RANK
    o3 = o2 + MLA_ROPE_DIM

    wi = w_in[0]
    w_kr = wi[:, o2:o3]
    w_z = jnp.concatenate([wi[:, o3:], wi[:, :o2], w_kr, _rot_half_cols(w_kr, half_r)], axis=-1).astype(BF16)

    wq = w_uq[0].reshape(Q_LORA_RANK, N_MLA_HEADS, MLA_QK_DIM)
    wq_pe = wq[..., MLA_NOPE_DIM:]
    wq_ext = jnp.concatenate([wq, _rot_half_cols(wq_pe, half_r)], axis=-1)
    wq_ext = wq_ext.reshape(Q_LORA_RANK, N_MLA_HEADS * MLA_PAD).astype(BF16)
    wkv = w_ukv[0].astype(BF16)

    def ext_gain(g):
        pe = g[MLA_NOPE_DIM:]
        return jnp.concatenate([g, pe[half_r:], pe[:half_r]])[None, :]

    gq_ext = ext_gain(mla_q_norm_g[0])
    gk_ext = ext_gain(mla_k_norm_g[0])
    t64, cos128, sin128 = _rope_tables(S)
    p4, p16 = _residue_perm(4), _residue_perm(16)

    z = _inproj(x2, norm1_g, w_z)
    q, k, vt, dq, dk, dq4, dk4, dv4, dq16, dk16, dv16 = _prep(
        z, g_cq, g_ckv, wq_ext, wkv, gq_ext, gk_ext[:, :LANES], gk_ext[:, LANES:],
        dil_q_norm_g, dil_k_norm_g, t64, cos128, sin128, p4, p16)
    mla_o = _mla_attn(q, k, vt)

    (w_1, d_1), (w_4, d_4), (w_16, d_16) = DIL_PATTERNS
    dil1 = _dil_pattern(dq[None], dk[None], z[None], Z_DV, w_1, d_1, hb=2, gb=4)
    dil4 = _dil_pattern(dq4, dk4, dv4, 0, w_4, d_4, hb=8, gb=1)
    dil16 = _dil_pattern(dq16, dk16, dv16, 0, w_16, d_16, hb=8, gb=1)

    pad = LANES - N_EXPERT_GROUPS - N_EXPERTS
    w_router = jnp.concatenate([w_group[0], w_expert[0], jnp.zeros((D, pad), F32)], axis=-1)
    b_router = jnp.concatenate([b_group[0], b_expert[0], jnp.zeros((pad,), F32)])[None, :]
    wr_hi = w_router.astype(BF16)
    wr_lo = (w_router - wr_hi.astype(F32)).astype(BF16)
    x1, xn, route_i, route_f = _outproj_router(
        mla_o, dil1, dil4, dil16, p4.T, p16.T, x2, w_out[0].astype(BF16), norm2_g,
        jnp.concatenate([wr_hi, wr_lo], axis=-1), wr_hi, b_router)

    item_e, item_start, item_n, order = _moe_items(route_i[:, :TOP_K], MOE_CAP)
    y = _moe_experts(xn, w1[0], w3[0], w2[0], item_e, item_start, item_n, order)
    out = _combine(x1, y, route_f)
    return out.reshape(B, S, D)
```

```python
import functools
import math

import jax
import jax.numpy as jnp
import numpy as np
from jax import lax
from jax.experimental import pallas as pl
from jax.experimental.pallas import tpu as pltpu

D_MODEL = 2048
N_MLA_HEADS = 8
MLA_NOPE_DIM = 128
MLA_ROPE_DIM = 64
MLA_QK_DIM = MLA_NOPE_DIM + MLA_ROPE_DIM
MLA_V_DIM = 128
Q_LORA_RANK = 512
KV_LORA_RANK = 512
N_DIL_HEADS = 8
DIL_HEAD_DIM = 128
DIL_PATTERNS = ((128, 1), (512, 4), (2048, 16))
ROPE_THETA = 10000.0
NORM_EPS = 1e-6
NEG_INF = -1e30
N_EXPERT_GROUPS = 8
EXPERTS_PER_GROUP = 8
N_EXPERTS = N_EXPERT_GROUPS * EXPERTS_PER_GROUP
TOP_K = 2
EXPERT_FF = 1408

LANES = 128
BF16_ROWS = 16
MXU_DIM = 256
HD = N_DIL_HEADS * DIL_HEAD_DIM
MLA_PAD = MXU_DIM
Z_DQ, Z_DK, Z_DV = 0, HD, 2 * HD
Z_CQ = 3 * HD
Z_CKV = Z_CQ + Q_LORA_RANK
Z_KR = Z_CKV + KV_LORA_RANK
Z_WIDTH = Z_KR + 2 * MLA_ROPE_DIM

ATT_TK = 512
ATT_TQ = 1024
ATT_UNROLL = 4
VT_ROWS = MLA_V_DIM + BF16_ROWS
PERM_TILE = 256
LOG2E = math.log2(math.e)

VMEM_LIMIT = 56 * 1024 * 1024

F32 = jnp.float32
BF16 = jnp.bfloat16


def _cparams(sem, vmem=VMEM_LIMIT):
    return pltpu.CompilerParams(dimension_semantics=sem, vmem_limit_bytes=vmem)


def _dot(a, b):
    return jnp.dot(a, b, preferred_element_type=F32)


def _dot_nt(a, b):
    return lax.dot_general(a, b, (((1,), (1,)), ((), ())), preferred_element_type=F32)


def _inproj_kernel(x_ref, g_ref, w_ref, z_ref):
    x = x_ref[...]
    r = lax.rsqrt(jnp.mean(x * x, axis=-1, keepdims=True) + NORM_EPS)
    h = (x * r * g_ref[...]).astype(BF16)
    z_ref[...] = _dot(h, w_ref[...]).astype(BF16)


def _inproj(x2, g1, w_z, tm=256):
    S = x2.shape[0]
    return pl.pallas_call(
        _inproj_kernel,
        grid=(S // tm,),
        in_specs=[
            pl.BlockSpec((tm, D_MODEL), lambda i: (i, 0)),
            pl.BlockSpec((1, D_MODEL), lambda i: (0, 0)),
            pl.BlockSpec((D_MODEL, Z_WIDTH), lambda i: (0, 0)),
        ],
        out_specs=pl.BlockSpec((tm, Z_WIDTH), lambda i: (i, 0)),
        out_shape=jax.ShapeDtypeStruct((S, Z_WIDTH), BF16),
        compiler_params=_cparams(("parallel",)),
        name="inproj",
    )(x2, g1, w_z)


def _prep_kernel(zdq_ref, zdk_ref, zdv_ref, zcq_ref, zckv_ref, zkr_ref,
                 gcq_ref, gckv_ref, wq_ref, wkv_ref, gq_ref, gkn_ref, gkp_ref,
                 gdq_ref, gdk_ref, t64_ref, cos_ref, sin_ref, p4_ref, p16_ref,
                 q_ref, k_ref, vt_ref, dq_ref, dk_ref,
                 dq4_ref, dk4_ref, dv4_ref, dq16_ref, dk16_ref, dv16_ref):
    tm = zcq_ref.shape[0]
    lane = lax.broadcasted_iota(jnp.int32, (1, LANES), 1)
    first_half = lane < MLA_ROPE_DIM

    def rms_rows(c, g):
        c = c.astype(F32)
        r = lax.rsqrt(jnp.mean(c * c, axis=-1, keepdims=True) + NORM_EPS)
        return (c * r * g).astype(BF16)

    cq = rms_rows(zcq_ref[...], gcq_ref[...])
    ckv = rms_rows(zckv_ref[...], gckv_ref[...])
    qe = _dot(cq, wq_ref[...])
    kv = _dot(ckv, wkv_ref[...])
    kr = zkr_ref[...].astype(F32)
    kr_ss = jnp.sum(jnp.where(first_half, kr * kr, 0.0), axis=-1, keepdims=True)
    t64 = t64_ref[...]
    q_scale = MLA_QK_DIM ** -0.5 * LOG2E

    def rope64(ext):
        t = ext * t64
        return jnp.where(first_half, t + pltpu.roll(t, MLA_ROPE_DIM, 1), 0.0)

    ones_rows = (lax.broadcasted_iota(jnp.int32, (BF16_ROWS, tm), 0) == 0).astype(BF16)
    eye = (lax.broadcasted_iota(jnp.int32, (MLA_V_DIM, MLA_V_DIM), 0)
           == lax.broadcasted_iota(jnp.int32, (MLA_V_DIM, MLA_V_DIM), 1)).astype(BF16)
    for h in range(N_MLA_HEADS):
        base = h * MLA_PAD
        qn = qe[:, base:base + LANES]
        qp = qe[:, base + LANES:base + 2 * LANES]
        ss = (jnp.sum(qn * qn, axis=-1, keepdims=True)
              + jnp.sum(jnp.where(first_half, qp * qp, 0.0), axis=-1, keepdims=True))
        r = lax.rsqrt(ss * (1.0 / MLA_QK_DIM) + NORM_EPS) * q_scale
        q_ref[:, base:base + LANES] = (qn * r * gq_ref[:, :LANES]).astype(BF16)
        q_ref[:, base + LANES:base + MLA_PAD] = rope64(qp * r * gq_ref[:, LANES:]).astype(BF16)

        kn = kv[:, base:base + LANES]
        ss = jnp.sum(kn * kn, axis=-1, keepdims=True) + kr_ss
        r = lax.rsqrt(ss * (1.0 / MLA_QK_DIM) + NORM_EPS)
        k_ref[:, base:base + LANES] = (kn * r * gkn_ref[...]).astype(BF16)
        k_ref[:, base + LANES:base + MLA_PAD] = rope64(kr * r * gkp_ref[...]).astype(BF16)
        v_h = kv[:, base + LANES:base + 2 * LANES].astype(BF16)
        vt_ref[h, :MLA_V_DIM, :] = _dot_nt(eye, v_h).astype(BF16)
        vt_ref[h, MLA_V_DIM:, :] = ones_rows

    cos = cos_ref[...]
    sin = sin_ref[...]
    d_scale = DIL_HEAD_DIM ** -0.5

    def dil_head(x, g, scale):
        x = x.astype(F32)
        r = lax.rsqrt(jnp.mean(x * x, axis=-1, keepdims=True) + NORM_EPS)
        y = x * r * g
        return ((y * cos + pltpu.roll(y, DIL_HEAD_DIM // 2, 1) * sin) * scale).astype(BF16)

    for h in range(N_DIL_HEADS):
        sl = slice(h * DIL_HEAD_DIM, (h + 1) * DIL_HEAD_DIM)
        dq_ref[:, sl] = dil_head(zdq_ref[:, sl], gdq_ref[...], d_scale)
        dk_ref[:, sl] = dil_head(zdk_ref[:, sl], gdk_ref[...], 1.0)

    for src, d4, d16 in ((dq_ref, dq4_ref, dq16_ref), (dk_ref, dk4_ref, dk16_ref), (zdv_ref, dv4_ref, dv16_ref)):
        for sub in range(tm // PERM_TILE):
            xs = src[sub * PERM_TILE:(sub + 1) * PERM_TILE, :]
            for dil, p_ref, dst in ((4, p4_ref, d4), (16, p16_ref, d16)):
                n = PERM_TILE // dil
                xp = _dot(p_ref[...], xs).astype(BF16)
                for r in range(dil):
                    dst[r, sub * n:(sub + 1) * n, :] = xp[r * n:(r + 1) * n, :]


def _prep(z, gcq, gckv, wq_ext, wkv, gq_ext, gk_nope, gk_pe, gdq, gdk, t64, cos128, sin128, p4, p16):
    S = z.shape[0]
    tm = ATT_TK
    row = lambda w, j: pl.BlockSpec((tm, w), lambda i, j=j: (i, j))
    full = lambda a: pl.BlockSpec(a.shape, lambda i: (0, 0))
    res = lambda dil: pl.BlockSpec((dil, tm // dil, HD), lambda i: (0, i, 0))
    qk_w = N_MLA_HEADS * MLA_PAD
    res_shape = lambda dil: jax.ShapeDtypeStruct((dil, S // dil, HD), BF16)
    return pl.pallas_call(
        _prep_kernel,
        grid=(S // tm,),
        in_specs=[
            row(HD, Z_DQ // HD), row(HD, Z_DK // HD), row(HD, Z_DV // HD),
            row(Q_LORA_RANK, Z_CQ // Q_LORA_RANK), row(KV_LORA_RANK, Z_CKV // KV_LORA_RANK),
            row(LANES, Z_KR // LANES),
            full(gcq), full(gckv), full(wq_ext), full(wkv), full(gq_ext), full(gk_nope), full(gk_pe),
            full(gdq), full(gdk),
            row(LANES, 0), row(LANES, 0), row(LANES, 0), full(p4), full(p16),
        ],
        out_specs=[row(qk_w, 0), row(qk_w, 0),
                   pl.BlockSpec((N_MLA_HEADS, None, VT_ROWS, tm), lambda i: (0, i, 0, 0)),
                   row(HD, 0), row(HD, 0),
                   res(4), res(4), res(4), res(16), res(16), res(16)],
        out_shape=[
            jax.ShapeDtypeStruct((S, qk_w), BF16), jax.ShapeDtypeStruct((S, qk_w), BF16),
            jax.ShapeDtypeStruct((N_MLA_HEADS, S // tm, VT_ROWS, tm), BF16),
            jax.ShapeDtypeStruct((S, HD), BF16), jax.ShapeDtypeStruct((S, HD), BF16),
            res_shape(4), res_shape(4), res_shape(4), res_shape(16), res_shape(16), res_shape(16),
        ],
        compiler_params=_cparams(("parallel",)),
        name="qkv_prep",
    )(z, z, z, z, z, z, gcq, gckv, wq_ext, wkv, gq_ext, gk_nope, gk_pe, gdq, gdk, t64, cos128, sin128, p4, p16)


def _mla_attn_kernel(q_ref, k_ref, vt_ref, o_ref, m_ref, acc_ref, s_ref):
    n_chunks = vt_ref.shape[0]
    q = q_ref[...]

    def scores(c):
        off = pl.multiple_of(c * ATT_TK, ATT_TK)
        return _dot_nt(k_ref[pl.ds(off, ATT_TK), :], q)

    m_ref[...] = jnp.full(m_ref.shape, -jnp.inf, F32)
    acc_ref[...] = jnp.zeros(acc_ref.shape, F32)

    def fold(c, slot):
        s = s_ref[slot]
        m_old = m_ref[...]
        m_new = jnp.maximum(m_old, jnp.max(s, axis=0, keepdims=True))
        alpha = jnp.exp2(m_old - m_new)
        p = jnp.exp2(s - m_new).astype(BF16)
        acc_ref[...] = alpha * acc_ref[...] + _dot(vt_ref[c], p)
        m_ref[...] = m_new

    s_ref[0] = scores(0)

    def body(t, carry):
        for u in range(ATT_UNROLL):
            c = ATT_UNROLL * t + u
            s_ref[(u + 1) % 2] = scores(jnp.minimum(c + 1, n_chunks - 1))
            fold(c, u % 2)
        return carry

    lax.fori_loop(0, n_chunks // ATT_UNROLL, body, 0)
    acc = acc_ref[...]
    o_t = acc[:MLA_V_DIM, :] / acc[MLA_V_DIM:MLA_V_DIM + 1, :]
    o_ref[...] = o_t.T.astype(o_ref.dtype)


def _mla_attn(q, k, vt):
    S = q.shape[0]
    n_chunks = vt.shape[1]
    return pl.pallas_call(
        _mla_attn_kernel,
        grid=(N_MLA_HEADS, S // ATT_TQ),
        in_specs=[
            pl.BlockSpec((ATT_TQ, MLA_PAD), lambda h, i: (i, h)),
            pl.BlockSpec((S, MLA_PAD), lambda h, i: (0, h)),
            pl.BlockSpec((None, n_chunks, VT_ROWS, ATT_TK), lambda h, i: (h, 0, 0, 0)),
        ],
        out_specs=pl.BlockSpec((ATT_TQ, MLA_V_DIM), lambda h, i: (i, h)),
        out_shape=jax.ShapeDtypeStruct((S, N_MLA_HEADS * MLA_V_DIM), BF16),
        scratch_shapes=[pltpu.VMEM((1, ATT_TQ), F32), pltpu.VMEM((VT_ROWS, ATT_TQ), F32),
                        pltpu.VMEM((2, ATT_TK, ATT_TQ), F32)],
        compiler_params=_cparams(("parallel", "parallel")),
        name="mla_attn",
    )(q, k, vt)


DIL_QB = 128


def _dil_kernel(q_ref, k_ref, v_ref, o_ref, lse_ref, *, half_w, hb, gb):
    L = q_ref.shape[0]
    kw = DIL_QB + 2 * half_w
    g_id = pl.program_id(1)
    lane = lax.broadcasted_iota(jnp.int32, (DIL_QB, LANES), 1)

    @pl.when(g_id == 0)
    def _():
        lse_ref[...] = jnp.zeros(lse_ref.shape, F32)

    def body(step, carry):
        qs_l, ks_l, qb, kb, vb = [], [], [], [], []
        for j in range(gb):
            qs = pl.multiple_of((step * gb + j) * DIL_QB, DIL_QB)
            ks = pl.multiple_of(jnp.clip(qs - half_w, 0, L - kw), half_w)
            qs_l.append(qs)
            ks_l.append(ks)
            for h in range(hb):
                sl = slice(h * DIL_HEAD_DIM, (h + 1) * DIL_HEAD_DIM)
                qb.append(q_ref[pl.ds(qs, DIL_QB), sl])
                kb.append(k_ref[pl.ds(ks, kw), sl])
                vb.append(v_ref[pl.ds(ks, kw), sl])
        q = jnp.stack(qb)
        k = jnp.stack(kb)
        v = jnp.stack(vb)
        s = jnp.einsum("gqd,gkd->gqk", q, k, preferred_element_type=F32)
        rel = (lax.broadcasted_iota(jnp.int32, (DIL_QB, kw), 0)
               - lax.broadcasted_iota(jnp.int32, (DIL_QB, kw), 1))
        bias = []
        for j in range(gb):
            mask = jnp.abs(rel + (qs_l[j] - ks_l[j])) <= half_w
            bias += [jnp.where(mask, 0.0, NEG_INF)] * hb
        s = s + jnp.stack(bias)
        m = jnp.max(s, axis=-1, keepdims=True)
        p = jnp.exp(s - m)
        den = jnp.sum(p, axis=-1, keepdims=True)
        o = jnp.einsum("gqk,gkd->gqd", p.astype(BF16), v, preferred_element_type=F32) / den
        lse = m + jnp.log(den)
        for j in range(gb):
            tile = lse_ref[pl.ds(qs_l[j], DIL_QB), :]
            for h in range(hb):
                sl = slice(h * DIL_HEAD_DIM, (h + 1) * DIL_HEAD_DIM)
                o_ref[pl.ds(qs_l[j], DIL_QB), sl] = o[j * hb + h].astype(o_ref.dtype)
                tile = jnp.where(lane == g_id * hb + h, lse[j * hb + h], tile)
            lse_ref[pl.ds(qs_l[j], DIL_QB), :] = tile
        return carry

    lax.fori_loop(0, L // (DIL_QB * gb), body, 0)


def _dil_pattern(dq, dk, dv, v_col0, window, dil, hb, gb):
    L = dq.shape[1]
    half_w = window // (2 * dil)
    w = hb * DIL_HEAD_DIM
    spec = lambda c0: pl.BlockSpec((None, L, w), lambda r, g, c0=c0: (r, 0, c0 + g))
    return pl.pallas_call(
        functools.partial(_dil_kernel, half_w=half_w, hb=hb, gb=gb),
        grid=(dil, N_DIL_HEADS // hb),
        in_specs=[spec(0), spec(0), spec(v_col0 // w)],
        out_specs=[spec(0), pl.BlockSpec((None, L, LANES), lambda r, g: (r, 0, 0))],
        out_shape=[jax.ShapeDtypeStruct((dil, L, HD), BF16), jax.ShapeDtypeStruct((dil, L, LANES), F32)],
        compiler_params=_cparams(("parallel", "arbitrary")),
        name=f"dil_attn_d{dil}",
    )(dq, dk, dv)


def _outproj_kernel(a_ref, o1_ref, l1_ref, o4_ref, l4_ref, o16_ref, l16_ref, p4t_ref, p16t_ref,
                    x_ref, wa_ref, wb_ref, g_ref, wr_ref, wrh_ref, br_ref,
                    x1_ref, xn_ref, ri_ref, rf_ref):
    tm = x_ref.shape[0]

    def to_token_order(o_ref, l_ref, pt_ref):
        pt = pt_ref[...]
        o = _dot(pt, o_ref[...].reshape(tm, HD))
        lse = l_ref[...].reshape(tm, LANES)
        hi = lse.astype(BF16)
        rem = lse - hi.astype(F32)
        mid = rem.astype(BF16)
        lo = (rem - mid.astype(F32)).astype(BF16)
        return o, _dot(pt, hi) + _dot(pt, mid) + _dot(pt, lo)

    o1 = o1_ref[...].astype(F32)
    l1 = l1_ref[...]
    o4, l4 = to_token_order(o4_ref, l4_ref, p4t_ref)
    o16, l16 = to_token_order(o16_ref, l16_ref, p16t_ref)
    big = jnp.maximum(jnp.maximum(l1, l4), l16)
    e1 = jnp.exp(l1 - big)
    e4 = jnp.exp(l4 - big)
    e16 = jnp.exp(l16 - big)
    inv = 1.0 / (e1 + e4 + e16)
    w1, w4, w16 = e1 * inv, e4 * inv, e16 * inv
    slabs = []
    for h in range(N_DIL_HEADS):
        sl = slice(h * DIL_HEAD_DIM, (h + 1) * DIL_HEAD_DIM)
        slabs.append(w1[:, h:h + 1] * o1[:, sl] + w4[:, h:h + 1] * o4[:, sl] + w16[:, h:h + 1] * o16[:, sl])
    dil_o = jnp.concatenate(slabs, axis=-1).astype(BF16)

    x1 = x_ref[...] + _dot(a_ref[...], wa_ref[...]) + _dot(dil_o, wb_ref[...])
    x1_ref[...] = x1
    r = lax.rsqrt(jnp.mean(x1 * x1, axis=-1, keepdims=True) + NORM_EPS)
    xn = x1 * r * g_ref[...]
    xn_ref[...] = xn
    xh = xn.astype(BF16)
    xl = (xn - xh.astype(F32)).astype(BF16)
    two = _dot(xh, wr_ref[...])
    logits = two[:, :LANES] + two[:, LANES:] + _dot(xl, wrh_ref[...]) + br_ref[...]
    lane = lax.broadcasted_iota(jnp.int32, logits.shape, 1)
    ninf = -jnp.inf

    def first_argmax(vals, vmax):
        return jnp.min(jnp.where(vals == vmax, lane, LANES), axis=-1, keepdims=True)

    coarse = jnp.where(lane < N_EXPERT_GROUPS, logits, ninf)
    cmax = jnp.max(coarse, axis=-1, keepdims=True)
    g = first_argmax(coarse, cmax)
    p_g = 1.0 / jnp.sum(jnp.exp(coarse - cmax), axis=-1, keepdims=True)
    lo_lane = N_EXPERT_GROUPS + g * EXPERTS_PER_GROUP
    fine = jnp.where((lane >= lo_lane) & (lane < lo_lane + EXPERTS_PER_GROUP), logits, ninf)
    v1 = jnp.max(fine, axis=-1, keepdims=True)
    j1 = first_argmax(fine, v1)
    fine2 = jnp.where(lane == j1, ninf, fine)
    v2 = jnp.max(fine2, axis=-1, keepdims=True)
    j2 = first_argmax(fine2, v2)
    e2 = jnp.exp(v2 - v1)
    g1 = 1.0 / (1.0 + e2)
    g2 = e2 / (1.0 + e2)
    ri_ref[...] = jnp.where(lane == 0, j1 - N_EXPERT_GROUPS, jnp.where(lane == 1, j2 - N_EXPERT_GROUPS, 0))
    rf_ref[...] = jnp.where(lane == 0, p_g * g1, jnp.where(lane == 1, p_g * g2, 0.0))


def _outproj_router(mla_o, dil1, dil4, dil16, p4t, p16t, x2, w_out_bf, g2, wr_two, wr_hi, b_router):
    S = x2.shape[0]
    tm = PERM_TILE
    half = N_MLA_HEADS * MLA_V_DIM
    row = lambda w: pl.BlockSpec((tm, w), lambda i: (i, 0))
    res = lambda dil, w: pl.BlockSpec((dil, tm // dil, w), lambda i: (0, i, 0))
    const = lambda a: pl.BlockSpec(a.shape, lambda i: (0, 0))
    return pl.pallas_call(
        _outproj_kernel,
        grid=(S // tm,),
        in_specs=[
            row(half),
            pl.BlockSpec((None, tm, HD), lambda i: (0, i, 0)), pl.BlockSpec((None, tm, LANES), lambda i: (0, i, 0)),
            res(4, HD), res(4, LANES), res(16, HD), res(16, LANES), const(p4t), const(p16t),
            row(D_MODEL),
            pl.BlockSpec((half, D_MODEL), lambda i: (0, 0)),
            pl.BlockSpec((HD, D_MODEL), lambda i: (1, 0)),
            const(g2), const(wr_two), const(wr_hi), const(b_router),
        ],
        out_specs=[row(D_MODEL), row(D_MODEL), row(LANES), row(LANES)],
        out_shape=[
            jax.ShapeDtypeStruct((S, D_MODEL), F32), jax.ShapeDtypeStruct((S, D_MODEL), F32),
            jax.ShapeDtypeStruct((S, LANES), jnp.int32), jax.ShapeDtypeStruct((S, LANES), F32),
        ],
        compiler_params=_cparams(("parallel",)),
        name="outproj_router",
    )(mla_o, dil1[0], dil1[1], dil4[0], dil4[1], dil16[0], dil16[1], p4t, p16t,
      x2, w_out_bf, w_out_bf, g2, wr_two, wr_hi, b_router)


MOE_CAP = 512
MOE_RH = 128
MOE_KC = 512
MOE_NK = D_MODEL // MOE_KC
MOE_FB = 768
MOE_NB = 2
MOE_FB_LAST = EXPERT_FF - (MOE_NB - 1) * MOE_FB
MOE_STEPS = MOE_NK + MOE_NB
MOE_UP_PARTS = ((0, 1024), (1024, 2048), (2048, 2 * EXPERT_FF))
MOE_DOWN_PARTS = ((0, 1024), (1024, D_MODEL))
DMA_UNROLL = 8
assert 0 < MOE_FB_LAST <= MOE_FB and MOE_FB % LANES == 0 and MOE_FB_LAST % LANES == 0


def _moe_kernel(ie_ref, is_ref, in_ref, ord_ref,
                xn_hbm, w1_ref, w3_ref, w2_ref,
                y_hbm, xf_ref, xb_ref, ab_ref, h_ref, acc_ref, sem_in, sem_out):
    i = pl.program_id(0)
    j = pl.program_id(1)
    n_items = pl.num_programs(0)
    n_tokens = xn_hbm.shape[0]
    n = in_ref[i]
    slot = lax.rem(i, 2)

    def row_in(item, sl, base, u, width):
        tok = lax.shift_right_logical(ord_ref[is_ref[item] + base + u], 1)
        dst = xf_ref.at[sl, pl.ds(base, width), :].at[pl.ds(u, 1), :]
        return pltpu.make_async_copy(xn_hbm.at[pl.ds(tok, 1), :], dst, sem_in.at[sl])

    def row_out(item, base, u, width):
        a = ord_ref[is_ref[item] + base + u]
        dst = (a & 1) * n_tokens + lax.shift_right_logical(a, 1)
        src = acc_ref.at[pl.ds(base, width), :].at[pl.ds(u, 1), :]
        return pltpu.make_async_copy(src, y_hbm.at[pl.ds(dst, 1), :], sem_out)

    def for_rows(count, fn):
        groups = lax.div(count, DMA_UNROLL)

        def group(g, c):
            base = pl.multiple_of(g * DMA_UNROLL, DMA_UNROLL)
            for u in range(DMA_UNROLL):
                fn(base, u, DMA_UNROLL)
            return c
        lax.fori_loop(0, groups, group, 0)

        def single(r, c):
            fn(r, 0, 1)
            return c
        lax.fori_loop(groups * DMA_UNROLL, count, single, 0)

    def wait_rows(count, rows_desc):
        groups = lax.div(count, DMA_UNROLL)

        def group(g, c):
            rows_desc(pl.multiple_of(g * DMA_UNROLL, DMA_UNROLL), DMA_UNROLL).wait()
            return c
        lax.fori_loop(0, groups, group, 0)

        def single(r, c):
            rows_desc(r, 1).wait()
            return c
        lax.fori_loop(groups * DMA_UNROLL, count, single, 0)

    def rows_in(sl, r0, size):
        return pltpu.make_async_copy(xn_hbm.at[pl.ds(0, size), :], xf_ref.at[sl, pl.ds(r0, size), :], sem_in.at[sl])

    def rows_out(r0, size):
        return pltpu.make_async_copy(acc_ref.at[pl.ds(r0, size), :], y_hbm.at[pl.ds(0, size), :], sem_out)

    def for_row_tiles(fn):
        tiles = lax.div(n + (MOE_RH - 1), MOE_RH)
        for k in range(1, MOE_CAP // MOE_RH + 1):
            @pl.when(tiles == k)
            def _():
                fn(0, k * MOE_RH)

    @pl.when((i == 0) & (j == 0))
    def _():
        xf_ref[...] = jnp.zeros(xf_ref.shape, F32)
        for_rows(n, lambda *row: row_in(0, 0, *row).start())

    @pl.when((n > 0) & (j == 0))
    def _():
        wait_rows(n, functools.partial(rows_in, slot))

        @pl.when(i > 0)
        def _():
            wait_rows(in_ref[i - 1], rows_out)

        for t in range(MOE_CAP // MOE_RH):
            @pl.when(t * MOE_RH < n)
            def _():
                sl = pl.ds(t * MOE_RH, MOE_RH)
                for kc in range(MOE_NK):
                    xb_ref[kc, sl, :] = xf_ref[slot, sl, kc * MOE_KC:(kc + 1) * MOE_KC].astype(BF16)
                ab_ref[sl, :] = jnp.zeros((MOE_RH, 2 * EXPERT_FF), F32)

    @pl.when((n > 0) & (j == 1) & (i + 1 < n_items))
    def _():
        nxt = jnp.minimum(i + 1, n_items - 1)
        for_rows(in_ref[nxt], lambda *row: row_in(nxt, 1 - slot, *row).start())

    @pl.when((n > 0) & (j < MOE_NK))
    def _():
        kc = jnp.minimum(j, MOE_NK - 1)

        def cat_cols(c0, c1):
            pieces = []
            if c0 < EXPERT_FF:
                pieces.append(w1_ref[:, c0:min(c1, EXPERT_FF)].astype(BF16))
            if c1 > EXPERT_FF:
                pieces.append(w3_ref[:, max(c0, EXPERT_FF) - EXPERT_FF:c1 - EXPERT_FF].astype(BF16))
            return pieces[0] if len(pieces) == 1 else jnp.concatenate(pieces, axis=-1)

        def up(off, size):
            xs = xb_ref[kc, pl.ds(off, size), :]
            for c0, c1 in MOE_UP_PARTS:
                ab_ref[pl.ds(off, size), c0:c1] += _dot(xs, cat_cols(c0, c1))
        for_row_tiles(up)

    @pl.when((n > 0) & (j == MOE_NK))
    def _():
        def act(off, size):
            a = ab_ref[pl.ds(off, size), :EXPERT_FF]
            b = ab_ref[pl.ds(off, size), EXPERT_FF:]
            half_a = 0.5 * a
            h_ref[pl.ds(off, size), :] = ((half_a + half_a * jnp.tanh(half_a)) * b).astype(BF16)
        for_row_tiles(act)

        def down(off, size):
            hs = h_ref[pl.ds(off, size), :MOE_FB]
            for c0, c1 in MOE_DOWN_PARTS:
                acc_ref[pl.ds(off, size), c0:c1] = _dot(hs, w2_ref[:, c0:c1].astype(BF16))
        for_row_tiles(down)

    @pl.when((n > 0) & (j == MOE_NK + 1))
    def _():
        def down(off, size):
            hs = h_ref[pl.ds(off, size), MOE_FB:]
            for c0, c1 in MOE_DOWN_PARTS:
                acc_ref[pl.ds(off, size), c0:c1] += _dot(hs, w2_ref[:MOE_FB_LAST, c0:c1].astype(BF16))
        for_row_tiles(down)
        for_rows(n, lambda *row: row_out(i, *row).start())
        nxt = jnp.minimum(i + 1, n_items - 1)

        @pl.when((i == n_items - 1) | (in_ref[nxt] == 0))
        def _():
            wait_rows(n, rows_out)


def _moe_experts(xn, w1, w3, w2, item_e, item_start, item_n, order):
    S = xn.shape[0]
    n_items = item_e.shape[0]

    def up_idx(i, j, ie, is_, n_, o):
        return ie[i], jnp.where(n_[i] > 0, jnp.minimum(j, MOE_NK - 1), MOE_NK - 1), 0

    def down_idx(i, j, ie, is_, n_, o):
        return ie[i], jnp.where(n_[i] > 0, jnp.maximum(j - MOE_NK, 0), MOE_NB - 1), 0

    up = pl.BlockSpec((None, MOE_KC, EXPERT_FF), up_idx)
    down = pl.BlockSpec((None, MOE_FB, D_MODEL), down_idx)
    grid_spec = pltpu.PrefetchScalarGridSpec(
        num_scalar_prefetch=4,
        grid=(n_items, MOE_STEPS),
        in_specs=[pl.BlockSpec(memory_space=pl.ANY), up, up, down],
        out_specs=pl.BlockSpec(memory_space=pl.ANY),
        scratch_shapes=[
            pltpu.VMEM((2, MOE_CAP, D_MODEL), F32),
            pltpu.VMEM((MOE_NK, MOE_CAP, MOE_KC), BF16),
            pltpu.VMEM((MOE_CAP, 2 * EXPERT_FF), F32),
            pltpu.VMEM((MOE_CAP, EXPERT_FF), BF16),
            pltpu.VMEM((MOE_CAP, D_MODEL), F32),
            pltpu.SemaphoreType.DMA((2,)),
            pltpu.SemaphoreType.DMA(()),
        ],
    )
    return pl.pallas_call(
        _moe_kernel,
        grid_spec=grid_spec,
        out_shape=jax.ShapeDtypeStruct((TOP_K * S, D_MODEL), F32),
        compiler_params=_cparams(("arbitrary", "arbitrary")),
        name="moe_experts",
    )(item_e, item_start, item_n, order, xn, w1, w3, w2)


def _moe_items(eid, cap):
    A = eid.size
    flat_e = eid.reshape(A)
    order = jnp.argsort(flat_e, stable=True).astype(jnp.int32)
    experts = jnp.arange(N_EXPERTS + 1, dtype=jnp.int32)
    starts = jnp.sum((flat_e[None, :] < experts[:, None]).astype(jnp.int32), axis=1)
    counts = starts[1:] - starts[:-1]
    per_e = (counts + cap - 1) // cap
    item_end = jnp.cumsum(per_e)
    total = item_end[-1]
    n_items = N_EXPERTS + A // cap
    idx = jnp.arange(n_items, dtype=jnp.int32)
    clamped = jnp.minimum(idx, total - 1)
    e = jnp.minimum(jnp.searchsorted(item_end, clamped, side="right"), N_EXPERTS - 1).astype(jnp.int32)
    local = clamped - (item_end[e] - per_e[e])
    used = idx < total
    item_start = jnp.where(used, starts[e] + local * cap, 0).astype(jnp.int32)
    item_n = jnp.where(used, jnp.clip(counts[e] - local * cap, 0, cap), 0).astype(jnp.int32)
    return e, item_start, item_n, order


def _combine_kernel(x1_ref, y0_ref, y1_ref, g_ref, o_ref):
    g = g_ref[...]
    o_ref[...] = x1_ref[...] + (g[:, 0:1] * y0_ref[...] + g[:, 1:2] * y1_ref[...])


def _combine(x1, y, gates, tm=256):
    S = x1.shape[0]
    nb = S // tm
    return pl.pallas_call(
        _combine_kernel,
        grid=(nb,),
        in_specs=[
            pl.BlockSpec((tm, D_MODEL), lambda i: (i, 0)),
            pl.BlockSpec((tm, D_MODEL), lambda i: (i, 0)),
            pl.BlockSpec((tm, D_MODEL), lambda i: (nb + i, 0)),
            pl.BlockSpec((tm, LANES), lambda i: (i, 0)),
        ],
        out_specs=pl.BlockSpec((tm, D_MODEL), lambda i: (i, 0)),
        out_shape=jax.ShapeDtypeStruct((S, D_MODEL), F32),
        compiler_params=_cparams(("parallel",)),
        name="moe_combine",
    )(x1, y, y, gates)


def _rot_half_cols(w, half):
    return jnp.concatenate([-w[..., half:], w[..., :half]], axis=-1)


def _rope_tables(S):
    pos = np.arange(S, dtype=np.float32)[:, None]

    def cs(half):
        inv = (np.float32(ROPE_THETA) ** (-np.arange(half, dtype=np.float32) / np.float32(half))).astype(np.float32)
        ang = (pos * inv[None, :]).astype(np.float64)
        return np.cos(ang).astype(np.float32), np.sin(ang).astype(np.float32)

    c32, s32 = cs(MLA_ROPE_DIM // 2)
    t64 = np.concatenate([c32, c32, s32, s32], axis=-1)
    c64, s64 = cs(DIL_HEAD_DIM // 2)
    cos128 = np.concatenate([c64, c64], axis=-1)
    sin128 = np.concatenate([-s64, s64], axis=-1)
    return jnp.asarray(t64), jnp.asarray(cos128), jnp.asarray(sin128)


def _residue_perm(dil):
    n = PERM_TILE // dil
    rows = np.arange(PERM_TILE)
    src = (rows % n) * dil + rows // n
    return jnp.asarray(src[:, None] == np.arange(PERM_TILE)[None, :], dtype=BF16)


def kernel(x, norm1_g, w_in, g_cq, g_ckv, w_uq, w_ukv, mla_q_norm_g, mla_k_norm_g, dil_q_norm_g,
           dil_k_norm_g, w_out, norm2_g, w_group, b_group, w_expert, b_expert, w1, w3, w2):
    B, S, D = x.shape
    assert B == 1 and D == D_MODEL and norm1_g.shape[0] == 1
    x2 = x.reshape(S, D)
    half_r = MLA_ROPE_DIM // 2
    o1, o2 = Q_LORA_RANK, Q_LORA_RANK + KV_LORA_RANK
    o3 = o2 + MLA_ROPE_DIM

    wi = w_in[0]
    w_kr = wi[:, o2:o3]
    w_z = jnp.concatenate([wi[:, o3:], wi[:, :o2], w_kr, _rot_half_cols(w_kr, half_r)], axis=-1).astype(BF16)

    wq = w_uq[0].reshape(Q_LORA_RANK, N_MLA_HEADS, MLA_QK_DIM)
    wq_pe = wq[..., MLA_NOPE_DIM:]
    wq_ext = jnp.concatenate([wq, _rot_half_cols(wq_pe, half_r)], axis=-1)
    wq_ext = wq_ext.reshape(Q_LORA_RANK, N_MLA_HEADS * MLA_PAD).astype(BF16)
    wkv = w_ukv[0].astype(BF16)

    def ext_gain(g):
        pe = g[MLA_NOPE_DIM:]
        return jnp.concatenate([g, pe[half_r:], pe[:half_r]])[None, :]

    gq_ext = ext_gain(mla_q_norm_g[0])
    gk_ext = ext_gain(mla_k_norm_g[0])
    t64, cos128, sin128 = _rope_tables(S)
    p4, p16 = _residue_perm(4), _residue_perm(16)

    z = _inproj(x2, norm1_g, w_z)
    q, k, vt, dq, dk, dq4, dk4, dv4, dq16, dk16, dv16 = _prep(
        z, g_cq, g_ckv, wq_ext, wkv, gq_ext, gk_ext[:, :LANES], gk_ext[:, LANES:],
        dil_q_norm_g, dil_k_norm_g, t64, cos128, sin128, p4, p16)
    mla_o = _mla_attn(q, k, vt)

    (w_1, d_1), (w_4, d_4), (w_16, d_16) = DIL_PATTERNS
    dil1 = _dil_pattern(dq[None], dk[None], z[None], Z_DV, w_1, d_1, hb=2, gb=4)
    dil4 = _dil_pattern(dq4, dk4, dv4, 0, w_4, d_4, hb=8, gb=1)
    dil16 = _dil_pattern(dq16, dk16, dv16, 0, w_16, d_16, hb=8, gb=1)

    pad = LANES - N_EXPERT_GROUPS - N_EXPERTS
    w_router = jnp.concatenate([w_group[0], w_expert[0], jnp.zeros((D, pad), F32)], axis=-1)
    b_router = jnp.concatenate([b_group[0], b_expert[0], jnp.zeros((pad,), F32)])[None, :]
    wr_hi = w_router.astype(BF16)
    wr_lo = (w_router - wr_hi.astype(F32)).astype(BF16)
    x1, xn, route_i, route_f = _outproj_router(
        mla_o, dil1, dil4, dil16, p4.T, p16.T, x2, w_out[0].astype(BF16), norm2_g,
        jnp.concatenate([wr_hi, wr_lo], axis=-1), wr_hi, b_router)

    item_e, item_start, item_n, order = _moe_items(route_i[:, :TOP_K], MOE_CAP)
    y = _moe_experts(xn, w1[0], w3[0], w2[0], item_e, item_start, item_n, order)
    out = _combine(x1, y, route_f)
    return out.reshape(B, S, D)
```

```python
import functools
import math

import jax
import jax.numpy as jnp
import numpy as np
from jax import lax
from jax.experimental import pallas as pl
from jax.experimental.pallas import tpu as pltpu

D_MODEL = 2048
N_MLA_HEADS = 8
MLA_NOPE_DIM = 128
MLA_ROPE_DIM = 64
MLA_QK_DIM = MLA_NOPE_DIM + MLA_ROPE_DIM
MLA_V_DIM = 128
Q_LORA_RANK = 512
KV_LORA_RANK = 512
N_DIL_HEADS = 8
DIL_HEAD_DIM = 128
DIL_PATTERNS = ((128, 1), (512, 4), (2048, 16))
ROPE_THETA = 10000.0
NORM_EPS = 1e-6
NEG_INF = -1e30
N_EXPERT_GROUPS = 8
EXPERTS_PER_GROUP = 8
N_EXPERTS = N_EXPERT_GROUPS * EXPERTS_PER_GROUP
TOP_K = 2
EXPERT_FF = 1408

LANES = 128
BF16_ROWS = 16
MXU_DIM = 256
HD = N_DIL_HEADS * DIL_HEAD_DIM
MLA_PAD = MXU_DIM
Z_DQ, Z_DK, Z_DV = 0, HD, 2 * HD
Z_CQ = 3 * HD
Z_CKV = Z_CQ + Q_LORA_RANK
Z_KR = Z_CKV + KV_LORA_RANK
Z_WIDTH = Z_KR + 2 * MLA_ROPE_DIM

ATT_TK = 512
ATT_TQ = 1024
ATT_UNROLL = 4
VT_ROWS = MLA_V_DIM + BF16_ROWS
PERM_TILE = 256
LOG2E = math.log2(math.e)

VMEM_LIMIT = 56 * 1024 * 1024

F32 = jnp.float32
BF16 = jnp.bfloat16


def _cparams(sem, vmem=VMEM_LIMIT):
    return pltpu.CompilerParams(dimension_semantics=sem, vmem_limit_bytes=vmem)


def _dot(a, b):
    return jnp.dot(a, b, preferred_element_type=F32)


def _dot_nt(a, b):
    return lax.dot_general(a, b, (((1,), (1,)), ((), ())), preferred_element_type=F32)


def _inproj_kernel(x_ref, g_ref, w_ref, z_ref):
    x = x_ref[...]
    r = lax.rsqrt(jnp.mean(x * x, axis=-1, keepdims=True) + NORM_EPS)
    h = (x * r * g_ref[...]).astype(BF16)
    z_ref[...] = _dot(h, w_ref[...]).astype(BF16)


def _inproj(x2, g1, w_z, tm=256):
    S = x2.shape[0]
    return pl.pallas_call(
        _inproj_kernel,
        grid=(S // tm,),
        in_specs=[
            pl.BlockSpec((tm, D_MODEL), lambda i: (i, 0)),
            pl.BlockSpec((1, D_MODEL), lambda i: (0, 0)),
            pl.BlockSpec((D_MODEL, Z_WIDTH), lambda i: (0, 0)),
        ],
        out_specs=pl.BlockSpec((tm, Z_WIDTH), lambda i: (i, 0)),
        out_shape=jax.ShapeDtypeStruct((S, Z_WIDTH), BF16),
        compiler_params=_cparams(("parallel",)),
        name="inproj",
    )(x2, g1, w_z)


def _prep_kernel(zdq_ref, zdk_ref, zdv_ref, zcq_ref, zckv_ref, zkr_ref,
                 gcq_ref, gckv_ref, wq_ref, wkv_ref, gq_ref, gkn_ref, gkp_ref,
                 gdq_ref, gdk_ref, t64_ref, cos_ref, sin_ref, p4_ref, p16_ref,
                 q_ref, k_ref, vt_ref, dq_ref, dk_ref,
                 dq4_ref, dk4_ref, dv4_ref, dq16_ref, dk16_ref, dv16_ref):
    tm = zcq_ref.shape[0]
    lane = lax.broadcasted_iota(jnp.int32, (1, LANES), 1)
    first_half = lane < MLA_ROPE_DIM

    def rms_rows(c, g):
        c = c.astype(F32)
        r = lax.rsqrt(jnp.mean(c * c, axis=-1, keepdims=True) + NORM_EPS)
        return (c * r * g).astype(BF16)

    cq = rms_rows(zcq_ref[...], gcq_ref[...])
    ckv = rms_rows(zckv_ref[...], gckv_ref[...])
    qe = _dot(cq, wq_ref[...])
    kv = _dot(ckv, wkv_ref[...])
    kr = zkr_ref[...].astype(F32)
    kr_ss = jnp.sum(jnp.where(first_half, kr * kr, 0.0), axis=-1, keepdims=True)
    t64 = t64_ref[...]
    q_scale = MLA_QK_DIM ** -0.5 * LOG2E

    def rope64(ext):
        t = ext * t64
        return jnp.where(first_half, t + pltpu.roll(t, MLA_ROPE_DIM, 1), 0.0)

    ones_rows = (lax.broadcasted_iota(jnp.int32, (BF16_ROWS, tm), 0) == 0).astype(BF16)
    eye = (lax.broadcasted_iota(jnp.int32, (MLA_V_DIM, MLA_V_DIM), 0)
           == lax.broadcasted_iota(jnp.int32, (MLA_V_DIM, MLA_V_DIM), 1)).astype(BF16)
    for h in range(N_MLA_HEADS):
        base = h * MLA_PAD
        qn = qe[:, base:base + LANES]
        qp = qe[:, base + LANES:base + 2 * LANES]
        ss = (jnp.sum(qn * qn, axis=-1, keepdims=True)
              + jnp.sum(jnp.where(first_half, qp * qp, 0.0), axis=-1, keepdims=True))
        r = lax.rsqrt(ss * (1.0 / MLA_QK_DIM) + NORM_EPS) * q_scale
        q_ref[:, base:base + LANES] = (qn * r * gq_ref[:, :LANES]).astype(BF16)
        q_ref[:, base + LANES:base + MLA_PAD] = rope64(qp * r * gq_ref[:, LANES:]).astype(BF16)

        kn = kv[:, base:base + LANES]
        ss = jnp.sum(kn * kn, axis=-1, keepdims=True) + kr_ss
        r = lax.rsqrt(ss * (1.0 / MLA_QK_DIM) + NORM_EPS)
        k_ref[:, base:base + LANES] = (kn * r * gkn_ref[...]).astype(BF16)
        k_ref[:, base + LANES:base + MLA_PAD] = rope64(kr * r * gkp_ref[...]).astype(BF16)
        v_h = kv[:, base + LANES:base + 2 * LANES].astype(BF16)
        vt_ref[h, :MLA_V_DIM, :] = _dot_nt(eye, v_h).astype(BF16)
        vt_ref[h, MLA_V_DIM:, :] = ones_rows

    cos = cos_ref[...]
    sin = sin_ref[...]
    d_scale = DIL_HEAD_DIM ** -0.5

    def dil_head(x, g, scale):
        x = x.astype(F32)
        r = lax.rsqrt(jnp.mean(x * x, axis=-1, keepdims=True) + NORM_EPS)
        y = x * r * g
        return ((y * cos + pltpu.roll(y, DIL_HEAD_DIM // 2, 1) * sin) * scale).astype(BF16)

    for h in range(N_DIL_HEADS):
        sl = slice(h * DIL_HEAD_DIM, (h + 1) * DIL_HEAD_DIM)
        dq_ref[:, sl] = dil_head(zdq_ref[:, sl], gdq_ref[...], d_scale)
        dk_ref[:, sl] = dil_head(zdk_ref[:, sl], gdk_ref[...], 1.0)

    for src, d4, d16 in ((dq_ref, dq4_ref, dq16_ref), (dk_ref, dk4_ref, dk16_ref), (zdv_ref, dv4_ref, dv16_ref)):
        for sub in range(tm // PERM_TILE):
            xs = src[sub * PERM_TILE:(sub + 1) * PERM_TILE, :]
            for dil, p_ref, dst in ((4, p4_ref, d4), (16, p16_ref, d16)):
                n = PERM_TILE // dil
                xp = _dot(p_ref[...], xs).astype(BF16)
                for r in range(dil):
                    dst[r, sub * n:(sub + 1) * n, :] = xp[r * n:(r + 1) * n, :]


def _prep(z, gcq, gckv, wq_ext, wkv, gq_ext, gk_nope, gk_pe, gdq, gdk, t64, cos128, sin128, p4, p16):
    S = z.shape[0]
    tm = ATT_TK
    row = lambda w, j: pl.BlockSpec((tm, w), lambda i, j=j: (i, j))
    full = lambda a: pl.BlockSpec(a.shape, lambda i: (0, 0))
    res = lambda dil: pl.BlockSpec((dil, tm // dil, HD), lambda i: (0, i, 0))
    qk_w = N_MLA_HEADS * MLA_PAD
    res_shape = lambda dil: jax.ShapeDtypeStruct((dil, S // dil, HD), BF16)
    return pl.pallas_call(
        _prep_kernel,
        grid=(S // tm,),
        in_specs=[
            row(HD, Z_DQ // HD), row(HD, Z_DK // HD), row(HD, Z_DV // HD),
            row(Q_LORA_RANK, Z_CQ // Q_LORA_RANK), row(KV_LORA_RANK, Z_CKV // KV_LORA_RANK),
            row(LANES, Z_KR // LANES),
            full(gcq), full(gckv), full(wq_ext), full(wkv), full(gq_ext), full(gk_nope), full(gk_pe),
            full(gdq), full(gdk),
            row(LANES, 0), row(LANES, 0), row(LANES, 0), full(p4), full(p16),
        ],
        out_specs=[row(qk_w, 0), row(qk_w, 0),
                   pl.BlockSpec((N_MLA_HEADS, None, VT_ROWS, tm), lambda i: (0, i, 0, 0)),
                   row(HD, 0), row(HD, 0),
                   res(4), res(4), res(4), res(16), res(16), res(16)],
        out_shape=[
            jax.ShapeDtypeStruct((S, qk_w), BF16), jax.ShapeDtypeStruct((S, qk_w), BF16),
            jax.ShapeDtypeStruct((N_MLA_HEADS, S // tm, VT_ROWS, tm), BF16),
            jax.ShapeDtypeStruct((S, HD), BF16), jax.ShapeDtypeStruct((S, HD), BF16),
            res_shape(4), res_shape(4), res_shape(4), res_shape(16), res_shape(16), res_shape(16),
        ],
        compiler_params=_cparams(("parallel",)),
        name="qkv_prep",
    )(z, z, z, z, z, z, gcq, gckv, wq_ext, wkv, gq_ext, gk_nope, gk_pe, gdq, gdk, t64, cos128, sin128, p4, p16)


def _mla_attn_kernel(q_ref, k_ref, vt_ref, o_ref, m_ref, acc_ref, s_ref):
    n_chunks = vt_ref.shape[0]
    q = q_ref[...]

    def scores(c):
        off = pl.multiple_of(c * ATT_TK, ATT_TK)
        return _dot_nt(k_ref[pl.ds(off, ATT_TK), :], q)

    m_ref[...] = jnp.full(m_ref.shape, -jnp.inf, F32)
    acc_ref[...] = jnp.zeros(acc_ref.shape, F32)

    def fold(c, slot):
        s = s_ref[slot]
        m_old = m_ref[...]
        m_new = jnp.maximum(m_old, jnp.max(s, axis=0, keepdims=True))
        alpha = jnp.exp2(m_old - m_new)
        p = jnp.exp2(s - m_new).astype(BF16)
        acc_ref[...] = alpha * acc_ref[...] + _dot(vt_ref[c], p)
        m_ref[...] = m_new

    s_ref[0] = scores(0)

    def body(t, carry):
        for u in range(ATT_UNROLL):
            c = ATT_UNROLL * t + u
            s_ref[(u + 1) % 2] = scores(jnp.minimum(c + 1, n_chunks - 1))
            fold(c, u % 2)
        return carry

    lax.fori_loop(0, n_chunks // ATT_UNROLL, body, 0)
    acc = acc_ref[...]
    o_t = acc[:MLA_V_DIM, :] / acc[MLA_V_DIM:MLA_V_DIM + 1, :]
    o_ref[...] = o_t.T.astype(o_ref.dtype)


def _mla_attn(q, k, vt):
    S = q.shape[0]
    n_chunks = vt.shape[1]
    return pl.pallas_call(
        _mla_attn_kernel,
        grid=(N_MLA_HEADS, S // ATT_TQ),
        in_specs=[
            pl.BlockSpec((ATT_TQ, MLA_PAD), lambda h, i: (i, h)),
            pl.BlockSpec((S, MLA_PAD), lambda h, i: (0, h)),
            pl.BlockSpec((None, n_chunks, VT_ROWS, ATT_TK), lambda h, i: (h, 0, 0, 0)),
        ],
        out_specs=pl.BlockSpec((ATT_TQ, MLA_V_DIM), lambda h, i: (i, h)),
        out_shape=jax.ShapeDtypeStruct((S, N_MLA_HEADS * MLA_V_DIM), BF16),
        scratch_shapes=[pltpu.VMEM((1, ATT_TQ), F32), pltpu.VMEM((VT_ROWS, ATT_TQ), F32),
                        pltpu.VMEM((2, ATT_TK, ATT_TQ), F32)],
        compiler_params=_cparams(("parallel", "parallel")),
        name="mla_attn",
    )(q, k, vt)


DIL_QB = 128


def _dil_kernel(q_ref, k_ref, v_ref, o_ref, lse_ref, *, half_w, hb, gb):
    L = q_ref.shape[0]
    kw = DIL_QB + 2 * half_w
    g_id = pl.program_id(1)
    lane = lax.broadcasted_iota(jnp.int32, (DIL_QB, LANES), 1)

    @pl.when(g_id == 0)
    def _():
        lse_ref[...] = jnp.zeros(lse_ref.shape, F32)

    def body(step, carry):
        qs_l, ks_l, qb, kb, vb = [], [], [], [], []
        for j in range(gb):
            qs = pl.multiple_of((step * gb + j) * DIL_QB, DIL_QB)
            ks = pl.multiple_of(jnp.clip(qs - half_w, 0, L - kw), half_w)
            qs_l.append(qs)
            ks_l.append(ks)
            for h in range(hb):
                sl = slice(h * DIL_HEAD_DIM, (h + 1) * DIL_HEAD_DIM)
                qb.append(q_ref[pl.ds(qs, DIL_QB), sl])
                kb.append(k_ref[pl.ds(ks, kw), sl])
                vb.append(v_ref[pl.ds(ks, kw), sl])
        q = jnp.stack(qb)
        k = jnp.stack(kb)
        v = jnp.stack(vb)
        s = jnp.einsum("gqd,gkd->gqk", q, k, preferred_element_type=F32)
        rel = (lax.broadcasted_iota(jnp.int32, (DIL_QB, kw), 0)
               - lax.broadcasted_iota(jnp.int32, (DIL_QB, kw), 1))
        bias = []
        for j in range(gb):
            mask = jnp.abs(rel + (qs_l[j] - ks_l[j])) <= half_w
            bias += [jnp.where(mask, 0.0, NEG_INF)] * hb
        s = s + jnp.stack(bias)
        m = jnp.max(s, axis=-1, keepdims=True)
        p = jnp.exp(s - m)
        den = jnp.sum(p, axis=-1, keepdims=True)
        o = jnp.einsum("gqk,gkd->gqd", p.astype(BF16), v, preferred_element_type=F32) / den
        lse = m + jnp.log(den)
        for j in range(gb):
            tile = lse_ref[pl.ds(qs_l[j], DIL_QB), :]
            for h in range(hb):
                sl = slice(h * DIL_HEAD_DIM, (h + 1) * DIL_HEAD_DIM)
                o_ref[pl.ds(qs_l[j], DIL_QB), sl] = o[j * hb + h].astype(o_ref.dtype)
                tile = jnp.where(lane == g_id * hb + h, lse[j * hb + h], tile)
            lse_ref[pl.ds(qs_l[j], DIL_QB), :] = tile
        return carry

    lax.fori_loop(0, L // (DIL_QB * gb), body, 0)


def _dil_pattern(dq, dk, dv, v_col0, window, dil, hb, gb):
    L = dq.shape[1]
    half_w = window // (2 * dil)
    w = hb * DIL_HEAD_DIM
    spec = lambda c0: pl.BlockSpec((None, L, w), lambda r, g, c0=c0: (r, 0, c0 + g))
    return pl.pallas_call(
        functools.partial(_dil_kernel, half_w=half_w, hb=hb, gb=gb),
        grid=(dil, N_DIL_HEADS // hb),
        in_specs=[spec(0), spec(0), spec(v_col0 // w)],
        out_specs=[spec(0), pl.BlockSpec((None, L, LANES), lambda r, g: (r, 0, 0))],
        out_shape=[jax.ShapeDtypeStruct((dil, L, HD), BF16), jax.ShapeDtypeStruct((dil, L, LANES), F32)],
        compiler_params=_cparams(("parallel", "arbitrary")),
        name=f"dil_attn_d{dil}",
    )(dq, dk, dv)


def _outproj_kernel(a_ref, o1_ref, l1_ref, o4_ref, l4_ref, o16_ref, l16_ref, p4t_ref, p16t_ref,
                    x_ref, wa_ref, wb_ref, g_ref, wr_ref, wrh_ref, br_ref,
                    x1_ref, xn_ref, ri_ref, rf_ref):
    tm = x_ref.shape[0]

    def to_token_order(o_ref, l_ref, pt_ref):
        pt = pt_ref[...]
        o = _dot(pt, o_ref[...].reshape(tm, HD))
        lse = l_ref[...].reshape(tm, LANES)
        hi = lse.astype(BF16)
        rem = lse - hi.astype(F32)
        mid = rem.astype(BF16)
        lo = (rem - mid.astype(F32)).astype(BF16)
        return o, _dot(pt, hi) + _dot(pt, mid) + _dot(pt, lo)

    o1 = o1_ref[...].astype(F32)
    l1 = l1_ref[...]
    o4, l4 = to_token_order(o4_ref, l4_ref, p4t_ref)
    o16, l16 = to_token_order(o16_ref, l16_ref, p16t_ref)
    big = jnp.maximum(jnp.maximum(l1, l4), l16)
    e1 = jnp.exp(l1 - big)
    e4 = jnp.exp(l4 - big)
    e16 = jnp.exp(l16 - big)
    inv = 1.0 / (e1 + e4 + e16)
    w1, w4, w16 = e1 * inv, e4 * inv, e16 * inv
    slabs = []
    for h in range(N_DIL_HEADS):
        sl = slice(h * DIL_HEAD_DIM, (h + 1) * DIL_HEAD_DIM)
        slabs.append(w1[:, h:h + 1] * o1[:, sl] + w4[:, h:h + 1] * o4[:, sl] + w16[:, h:h + 1] * o16[:, sl])
    dil_o = jnp.concatenate(slabs, axis=-1).astype(BF16)

    x1 = x_ref[...] + _dot(a_ref[...], wa_ref[...]) + _dot(dil_o, wb_ref[...])
    x1_ref[...] = x1
    r = lax.rsqrt(jnp.mean(x1 * x1, axis=-1, keepdims=True) + NORM_EPS)
    xn = x1 * r * g_ref[...]
    xn_ref[...] = xn
    xh = xn.astype(BF16)
    xl = (xn - xh.astype(F32)).astype(BF16)
    two = _dot(xh, wr_ref[...])
    logits = two[:, :LANES] + two[:, LANES:] + _dot(xl, wrh_ref[...]) + br_ref[...]
    lane = lax.broadcasted_iota(jnp.int32, logits.shape, 1)
    ninf = -jnp.inf

    def first_argmax(vals, vmax):
        return jnp.min(jnp.where(vals == vmax, lane, LANES), axis=-1, keepdims=True)

    coarse = jnp.where(lane < N_EXPERT_GROUPS, logits, ninf)
    cmax = jnp.max(coarse, axis=-1, keepdims=True)
    g = first_argmax(coarse, cmax)
    p_g = 1.0 / jnp.sum(jnp.exp(coarse - cmax), axis=-1, keepdims=True)
    lo_lane = N_EXPERT_GROUPS + g * EXPERTS_PER_GROUP
    fine = jnp.where((lane >= lo_lane) & (lane < lo_lane + EXPERTS_PER_GROUP), logits, ninf)
    v1 = jnp.max(fine, axis=-1, keepdims=True)
    j1 = first_argmax(fine, v1)
    fine2 = jnp.where(lane == j1, ninf, fine)
    v2 = jnp.max(fine2, axis=-1, keepdims=True)
    j2 = first_argmax(fine2, v2)
    e2 = jnp.exp(v2 - v1)
    g1 = 1.0 / (1.0 + e2)
    g2 = e2 / (1.0 + e2)
    ri_ref[...] = jnp.where(lane == 0, j1 - N_EXPERT_GROUPS, jnp.where(lane == 1, j2 - N_EXPERT_GROUPS, 0))
    rf_ref[...] = jnp.where(lane == 0, p_g * g1, jnp.where(lane == 1, p_g * g2, 0.0))


def _outproj_router(mla_o, dil1, dil4, dil16, p4t, p16t, x2, w_out_bf, g2, wr_two, wr_hi, b_router):
    S = x2.shape[0]
    tm = PERM_TILE
    half = N_MLA_HEADS * MLA_V_DIM
    row = lambda w: pl.BlockSpec((tm, w), lambda i: (i, 0))
    res = lambda dil, w: pl.BlockSpec((dil, tm // dil, w), lambda i: (0, i, 0))
    const = lambda a: pl.BlockSpec(a.shape, lambda i: (0, 0))
    return pl.pallas_call(
        _outproj_kernel,
        grid=(S // tm,),
        in_specs=[
            row(half),
            pl.BlockSpec((None, tm, HD), lambda i: (0, i, 0)), pl.BlockSpec((None, tm, LANES), lambda i: (0, i, 0)),
            res(4, HD), res(4, LANES), res(16, HD), res(16, LANES), const(p4t), const(p16t),
            row(D_MODEL),
            pl.BlockSpec((half, D_MODEL), lambda i: (0, 0)),
            pl.BlockSpec((HD, D_MODEL), lambda i: (1, 0)),
            const(g2), const(wr_two), const(wr_hi), const(b_router),
        ],
        out_specs=[row(D_MODEL), row(D_MODEL), row(LANES), row(LANES)],
        out_shape=[
            jax.ShapeDtypeStruct((S, D_MODEL), F32), jax.ShapeDtypeStruct((S, D_MODEL), F32),
            jax.ShapeDtypeStruct((S, LANES), jnp.int32), jax.ShapeDtypeStruct((S, LANES), F32),
        ],
        compiler_params=_cparams(("parallel",)),
        name="outproj_router",
    )(mla_o, dil1[0], dil1[1], dil4[0], dil4[1], dil16[0], dil16[1], p4t, p16t,
      x2, w_out_bf, w_out_bf, g2, wr_two, wr_hi, b_router)


MOE_CAP = 512
MOE_RH = 128
MOE_KC = 512
MOE_NK = D_MODEL // MOE_KC
MOE_FB = 768
MOE_NB = 2
MOE_FB_LAST = EXPERT_FF - (MOE_NB - 1) * MOE_FB
MOE_STEPS = MOE_NK + MOE_NB
MOE_LOOKAHEAD = 2
MOE_UP_RING = MOE_LOOKAHEAD + 1
MOE_UP_PARTS = ((0, 1024), (1024, 2048), (2048, 2 * EXPERT_FF))
MOE_DOWN_PARTS = ((0, 1024), (1024, D_MODEL))
DMA_UNROLL = 8
assert 0 < MOE_FB_LAST <= MOE_FB and MOE_FB % LANES == 0 and MOE_FB_LAST % LANES == 0


def _moe_kernel(ie_ref, is_ref, in_ref, ord_ref,
                xn_hbm, w1_hbm, w3_hbm, w2_hbm,
                y_hbm, xf_ref, xb_ref, ab_ref, h_ref, acc_ref, w1r, w3r, w2r,
                sem_in, sem_out, sem_up, sem_dn):
    i = pl.program_id(0)
    j = pl.program_id(1)
    n_items = pl.num_programs(0)
    n_tokens = xn_hbm.shape[0]
    n = in_ref[i]
    slot = lax.rem(i, 2)

    def up_ring(item, kc):
        return lax.rem(item * MOE_NK + kc, MOE_UP_RING)

    def block_copies(item, step):
        e = ie_ref[item]
        if step < MOE_NK:
            ring = up_ring(item, step)
            rows = pl.ds(step * MOE_KC, MOE_KC)
            return (pltpu.make_async_copy(w1_hbm.at[e, rows, :], w1r.at[ring], sem_up.at[ring]),
                    pltpu.make_async_copy(w3_hbm.at[e, rows, :], w3r.at[ring], sem_up.at[ring]))
        blk = step - MOE_NK
        size = MOE_FB if blk < MOE_NB - 1 else MOE_FB_LAST
        return (pltpu.make_async_copy(w2_hbm.at[e, pl.ds(blk * MOE_FB, size), :],
                                      w2r.at[blk, pl.ds(0, size), :], sem_dn.at[blk]),)

    def start_block(item, step):
        for c in block_copies(item, step):
            c.start()

    for js in range(MOE_STEPS):
        @pl.when((n > 0) & (j == js))
        def _():
            if js == 0:
                @pl.when(i == 0)
                def _():
                    for s0 in range(MOE_LOOKAHEAD):
                        start_block(0, s0)
            for c in block_copies(i, js):
                c.wait()
            tgt = js + MOE_LOOKAHEAD
            if tgt < MOE_STEPS:
                start_block(i, tgt)
            else:
                nxt = jnp.minimum(i + 1, n_items - 1)

                @pl.when((i + 1 < n_items) & (in_ref[nxt] > 0))
                def _():
                    start_block(nxt, tgt - MOE_STEPS)

    def row_in(item, sl, base, u, width):
        tok = lax.shift_right_logical(ord_ref[is_ref[item] + base + u], 1)
        dst = xf_ref.at[sl, pl.ds(base, width), :].at[pl.ds(u, 1), :]
        return pltpu.make_async_copy(xn_hbm.at[pl.ds(tok, 1), :], dst, sem_in.at[sl])

    def row_out(item, base, u, width):
        a = ord_ref[is_ref[item] + base + u]
        dst = (a & 1) * n_tokens + lax.shift_right_logical(a, 1)
        src = acc_ref.at[pl.ds(base, width), :].at[pl.ds(u, 1), :]
        return pltpu.make_async_copy(src, y_hbm.at[pl.ds(dst, 1), :], sem_out)

    def for_rows(count, fn):
        groups = lax.div(count, DMA_UNROLL)

        def group(g, c):
            base = pl.multiple_of(g * DMA_UNROLL, DMA_UNROLL)
            for u in range(DMA_UNROLL):
                fn(base, u, DMA_UNROLL)
            return c
        lax.fori_loop(0, groups, group, 0)

        def single(r, c):
            fn(r, 0, 1)
            return c
        lax.fori_loop(groups * DMA_UNROLL, count, single, 0)

    def wait_rows(count, rows_desc):
        groups = lax.div(count, DMA_UNROLL)

        def group(g, c):
            rows_desc(pl.multiple_of(g * DMA_UNROLL, DMA_UNROLL), DMA_UNROLL).wait()
            return c
        lax.fori_loop(0, groups, group, 0)

        def single(r, c):
            rows_desc(r, 1).wait()
            return c
        lax.fori_loop(groups * DMA_UNROLL, count, single, 0)

    def rows_in(sl, r0, size):
        return pltpu.make_async_copy(xn_hbm.at[pl.ds(0, size), :], xf_ref.at[sl, pl.ds(r0, size), :], sem_in.at[sl])

    def rows_out(r0, size):
        return pltpu.make_async_copy(acc_ref.at[pl.ds(r0, size), :], y_hbm.at[pl.ds(0, size), :], sem_out)

    def for_row_tiles(fn):
        tiles = lax.div(n + (MOE_RH - 1), MOE_RH)
        for k in range(1, MOE_CAP // MOE_RH + 1):
            @pl.when(tiles == k)
            def _():
                fn(0, k * MOE_RH)

    @pl.when((i == 0) & (j == 0))
    def _():
        xf_ref[...] = jnp.zeros(xf_ref.shape, F32)
        for_rows(n, lambda *row: row_in(0, 0, *row).start())

    @pl.when((n > 0) & (j == 0))
    def _():
        wait_rows(n, functools.partial(rows_in, slot))

        @pl.when(i > 0)
        def _():
            wait_rows(in_ref[i - 1], rows_out)

        for t in range(MOE_CAP // MOE_RH):
            @pl.when(t * MOE_RH < n)
            def _():
                sl = pl.ds(t * MOE_RH, MOE_RH)
                for kc in range(MOE_NK):
                    xb_ref[kc, sl, :] = xf_ref[slot, sl, kc * MOE_KC:(kc + 1) * MOE_KC].astype(BF16)
                ab_ref[sl, :] = jnp.zeros((MOE_RH, 2 * EXPERT_FF), F32)

    @pl.when((n > 0) & (j == 1) & (i + 1 < n_items))
    def _():
        nxt = jnp.minimum(i + 1, n_items - 1)
        for_rows(in_ref[nxt], lambda *row: row_in(nxt, 1 - slot, *row).start())

    @pl.when((n > 0) & (j < MOE_NK))
    def _():
        kc = jnp.minimum(j, MOE_NK - 1)
        ring = up_ring(i, kc)

        def cat_cols(c0, c1):
            pieces = []
            if c0 < EXPERT_FF:
                pieces.append(w1r[ring, :, c0:min(c1, EXPERT_FF)].astype(BF16))
            if c1 > EXPERT_FF:
                pieces.append(w3r[ring, :, max(c0, EXPERT_FF) - EXPERT_FF:c1 - EXPERT_FF].astype(BF16))
            return pieces[0] if len(pieces) == 1 else jnp.concatenate(pieces, axis=-1)

        def up(off, size):
            xs = xb_ref[kc, pl.ds(off, size), :]
            for c0, c1 in MOE_UP_PARTS:
                ab_ref[pl.ds(off, size), c0:c1] += _dot(xs, cat_cols(c0, c1))
        for_row_tiles(up)

    @pl.when((n > 0) & (j == MOE_NK))
    def _():
        def act(off, size):
            a = ab_ref[pl.ds(off, size), :EXPERT_FF]
            b = ab_ref[pl.ds(off, size), EXPERT_FF:]
            half_a = 0.5 * a
            h_ref[pl.ds(off, size), :] = ((half_a + half_a * jnp.tanh(half_a)) * b).astype(BF16)
        for_row_tiles(act)

        def down(off, size):
            hs = h_ref[pl.ds(off, size), :MOE_FB]
            for c0, c1 in MOE_DOWN_PARTS:
                acc_ref[pl.ds(off, size), c0:c1] = _dot(hs, w2r[0, :, c0:c1].astype(BF16))
        for_row_tiles(down)

    @pl.when((n > 0) & (j == MOE_NK + 1))
    def _():
        def down(off, size):
            hs = h_ref[pl.ds(off, size), MOE_FB:]
            for c0, c1 in MOE_DOWN_PARTS:
                acc_ref[pl.ds(off, size), c0:c1] += _dot(hs, w2r[MOE_NB - 1, :MOE_FB_LAST, c0:c1].astype(BF16))
        for_row_tiles(down)
        for_rows(n, lambda *row: row_out(i, *row).start())
        nxt = jnp.minimum(i + 1, n_items - 1)

        @pl.when((i == n_items - 1) | (in_ref[nxt] == 0))
        def _():
            wait_rows(n, rows_out)


def _moe_experts(xn, w1, w3, w2, item_e, item_start, item_n, order):
    S = xn.shape[0]
    n_items = item_e.shape[0]

    hbm = pl.BlockSpec(memory_space=pl.ANY)
    grid_spec = pltpu.PrefetchScalarGridSpec(
        num_scalar_prefetch=4,
        grid=(n_items, MOE_STEPS),
        in_specs=[hbm, hbm, hbm, hbm],
        out_specs=hbm,
        scratch_shapes=[
            pltpu.VMEM((2, MOE_CAP, D_MODEL), F32),
            pltpu.VMEM((MOE_NK, MOE_CAP, MOE_KC), BF16),
            pltpu.VMEM((MOE_CAP, 2 * EXPERT_FF), F32),
            pltpu.VMEM((MOE_CAP, EXPERT_FF), BF16),
            pltpu.VMEM((MOE_CAP, D_MODEL), F32),
            pltpu.VMEM((MOE_UP_RING, MOE_KC, EXPERT_FF), F32),
            pltpu.VMEM((MOE_UP_RING, MOE_KC, EXPERT_FF), F32),
            pltpu.VMEM((MOE_NB, MOE_FB, D_MODEL), F32),
            pltpu.SemaphoreType.DMA((2,)),
            pltpu.SemaphoreType.DMA(()),
            pltpu.SemaphoreType.DMA((MOE_UP_RING,)),
            pltpu.SemaphoreType.DMA((MOE_NB,)),
        ],
    )
    return pl.pallas_call(
        _moe_kernel,
        grid_spec=grid_spec,
        out_shape=jax.ShapeDtypeStruct((TOP_K * S, D_MODEL), F32),
        compiler_params=_cparams(("arbitrary", "arbitrary")),
        name="moe_experts",
    )(item_e, item_start, item_n, order, xn, w1, w3, w2)


def _moe_items(eid, cap):
    A = eid.size
    flat_e = eid.reshape(A)
    order = jnp.argsort(flat_e, stable=True).astype(jnp.int32)
    experts = jnp.arange(N_EXPERTS + 1, dtype=jnp.int32)
    starts = jnp.sum((flat_e[None, :] < experts[:, None]).astype(jnp.int32), axis=1)
    counts = starts[1:] - starts[:-1]
    per_e = (counts + cap - 1) // cap
    item_end = jnp.cumsum(per_e)
    total = item_end[-1]
    n_items = N_EXPERTS + A // cap
    idx = jnp.arange(n_items, dtype=jnp.int32)
    clamped = jnp.minimum(idx, total - 1)
    e = jnp.minimum(jnp.searchsorted(item_end, clamped, side="right"), N_EXPERTS - 1).astype(jnp.int32)
    local = clamped - (item_end[e] - per_e[e])
    used = idx < total
    item_start = jnp.where(used, starts[e] + local * cap, 0).astype(jnp.int32)
    item_n = jnp.where(used, jnp.clip(counts[e] - local * cap, 0, cap), 0).astype(jnp.int32)
    return e, item_start, item_n, order


def _combine_kernel(x1_ref, y0_ref, y1_ref, g_ref, o_ref):
    g = g_ref[...]
    o_ref[...] = x1_ref[...] + (g[:, 0:1] * y0_ref[...] + g[:, 1:2] * y1_ref[...])


def _combine(x1, y, gates, tm=256):
    S = x1.shape[0]
    nb = S // tm
    return pl.pallas_call(
        _combine_kernel,
        grid=(nb,),
        in_specs=[
            pl.BlockSpec((tm, D_MODEL), lambda i: (i, 0)),
            pl.BlockSpec((tm, D_MODEL), lambda i: (i, 0)),
            pl.BlockSpec((tm, D_MODEL), lambda i: (nb + i, 0)),
            pl.BlockSpec((tm, LANES), lambda i: (i, 0)),
        ],
        out_specs=pl.BlockSpec((tm, D_MODEL), lambda i: (i, 0)),
        out_shape=jax.ShapeDtypeStruct((S, D_MODEL), F32),
        compiler_params=_cparams(("parallel",)),
        name="moe_combine",
    )(x1, y, y, gates)


def _rot_half_cols(w, half):
    return jnp.concatenate([-w[..., half:], w[..., :half]], axis=-1)


def _rope_tables(S):
    pos = np.arange(S, dtype=np.float32)[:, None]

    def cs(half):
        inv = (np.float32(ROPE_THETA) ** (-np.arange(half, dtype=np.float32) / np.float32(half))).astype(np.float32)
        ang = (pos * inv[None, :]).astype(np.float64)
        return np.cos(ang).astype(np.float32), np.sin(ang).astype(np.float32)

    c32, s32 = cs(MLA_ROPE_DIM // 2)
    t64 = np.concatenate([c32, c32, s32, s32], axis=-1)
    c64, s64 = cs(DIL_HEAD_DIM // 2)
    cos128 = np.concatenate([c64, c64], axis=-1)
    sin128 = np.concatenate([-s64, s64], axis=-1)
    return jnp.asarray(t64), jnp.asarray(cos128), jnp.asarray(sin128)


def _residue_perm(dil):
    n = PERM_TILE // dil
    rows = np.arange(PERM_TILE)
    src = (rows % n) * dil + rows // n
    return jnp.asarray(src[:, None] == np.arange(PERM_TILE)[None, :], dtype=BF16)


def kernel(x, norm1_g, w_in, g_cq, g_ckv, w_uq, w_ukv, mla_q_norm_g, mla_k_norm_g, dil_q_norm_g,
           dil_k_norm_g, w_out, norm2_g, w_group, b_group, w_expert, b_expert, w1, w3, w2):
    B, S, D = x.shape
    assert B == 1 and D == D_MODEL and norm1_g.shape[0] == 1
    x2 = x.reshape(S, D)
    half_r = MLA_ROPE_DIM // 2
    o1, o2 = Q_LORA_RANK, Q_LORA_RANK + KV_LORA_RANK
    o3 = o2 + MLA_ROPE_DIM

    wi = w_in[0]
    w_kr = wi[:, o2:o3]
    w_z = jnp.concatenate([wi[:, o3:], wi[:, :o2], w_kr, _rot_half_cols(w_kr, half_r)], axis=-1).astype(BF16)

    wq = w_uq[0].reshape(Q_LORA_RANK, N_MLA_HEADS, MLA_QK_DIM)
    wq_pe = wq[..., MLA_NOPE_DIM:]
    wq_ext = jnp.concatenate([wq, _rot_half_cols(wq_pe, half_r)], axis=-1)
    wq_ext = wq_ext.reshape(Q_LORA_RANK, N_MLA_HEADS * MLA_PAD).astype(BF16)
    wkv = w_ukv[0].astype(BF16)

    def ext_gain(g):
        pe = g[MLA_NOPE_DIM:]
        return jnp.concatenate([g, pe[half_r:], pe[:half_r]])[None, :]

    gq_ext = ext_gain(mla_q_norm_g[0])
    gk_ext = ext_gain(mla_k_norm_g[0])
    t64, cos128, sin128 = _rope_tables(S)
    p4, p16 = _residue_perm(4), _residue_perm(16)

    z = _inproj(x2, norm1_g, w_z)
    q, k, vt, dq, dk, dq4, dk4, dv4, dq16, dk16, dv16 = _prep(
        z, g_cq, g_ckv, wq_ext, wkv, gq_ext, gk_ext[:, :LANES], gk_ext[:, LANES:],
        dil_q_norm_g, dil_k_norm_g, t64, cos128, sin128, p4, p16)
    mla_o = _mla_attn(q, k, vt)

    (w_1, d_1), (w_4, d_4), (w_16, d_16) = DIL_PATTERNS
    dil1 = _dil_pattern(dq[None], dk[None], z[None], Z_DV, w_1, d_1, hb=2, gb=4)
    dil4 = _dil_pattern(dq4, dk4, dv4, 0, w_4, d_4, hb=8, gb=1)
    dil16 = _dil_pattern(dq16, dk16, dv16, 0, w_16, d_16, hb=8, gb=1)

    pad = LANES - N_EXPERT_GROUPS - N_EXPERTS
    w_router = jnp.concatenate([w_group[0], w_expert[0], jnp.zeros((D, pad), F32)], axis=-1)
    b_router = jnp.concatenate([b_group[0], b_expert[0], jnp.zeros((pad,), F32)])[None, :]
    wr_hi = w_router.astype(BF16)
    wr_lo = (w_router - wr_hi.astype(F32)).astype(BF16)
    x1, xn, route_i, route_f = _outproj_router(
        mla_o, dil1, dil4, dil16, p4.T, p16.T, x2, w_out[0].astype(BF16), norm2_g,
        jnp.concatenate([wr_hi, wr_lo], axis=-1), wr_hi, b_router)

    item_e, item_start, item_n, order = _moe_items(route_i[:, :TOP_K], MOE_CAP)
    y = _moe_experts(xn, w1[0], w3[0], w2[0], item_e, item_start, item_n, order)
    out = _combine(x1, y, route_f)
    return out.reshape(B, S, D)
```

```python
import functools
import math

import jax
import jax.numpy as jnp
import numpy as np
from jax import lax
from jax.experimental import pallas as pl
from jax.experimental.pallas import tpu as pltpu

D_MODEL = 2048
N_MLA_HEADS = 8
MLA_NOPE_DIM = 128
MLA_ROPE_DIM = 64
MLA_QK_DIM = MLA_NOPE_DIM + MLA_ROPE_DIM
MLA_V_DIM = 128
Q_LORA_RANK = 512
KV_LORA_RANK = 512
N_DIL_HEADS = 8
DIL_HEAD_DIM = 128
DIL_PATTERNS = ((128, 1), (512, 4), (2048, 16))
ROPE_THETA = 10000.0
NORM_EPS = 1e-6
NEG_INF = -1e30
N_EXPERT_GROUPS = 8
EXPERTS_PER_GROUP = 8
N_EXPERTS = N_EXPERT_GROUPS * EXPERTS_PER_GROUP
TOP_K = 2
EXPERT_FF = 1408

LANES = 128
BF16_ROWS = 16
MXU_DIM = 256
HD = N_DIL_HEADS * DIL_HEAD_DIM
MLA_PAD = MXU_DIM
Z_DQ, Z_DK, Z_DV = 0, HD, 2 * HD
Z_CQ = 3 * HD
Z_CKV = Z_CQ + Q_LORA_RANK
Z_KR = Z_CKV + KV_LORA_RANK
Z_WIDTH = Z_KR + 2 * MLA_ROPE_DIM

ATT_TK = 512
ATT_TQ = 1024
ATT_UNROLL = 8
VT_ROWS = MLA_V_DIM + BF16_ROWS
PERM_TILE = 256
LOG2E = math.log2(math.e)

VMEM_LIMIT = 56 * 1024 * 1024

F32 = jnp.float32
BF16 = jnp.bfloat16


def _cparams(sem, vmem=VMEM_LIMIT):
    return pltpu.CompilerParams(dimension_semantics=sem, vmem_limit_bytes=vmem)


def _dot(a, b):
    return jnp.dot(a, b, preferred_element_type=F32)


def _dot_nt(a, b):
    return lax.dot_general(a, b, (((1,), (1,)), ((), ())), preferred_element_type=F32)


def _regroup_kernel(w_ref, o_ref):
    lat = Q_LORA_RANK + KV_LORA_RANK
    o_ref[:, Z_DQ:Z_CQ] = w_ref[:, lat + MLA_ROPE_DIM:].astype(BF16)
    o_ref[:, Z_CQ:Z_KR] = w_ref[:, :lat].astype(BF16)
    slab = w_ref[:, lat:lat + LANES]
    lane = lax.broadcasted_iota(jnp.int32, slab.shape, 1)
    half = MLA_ROPE_DIM // 2
    rot = jnp.where(lane < MLA_ROPE_DIM + half, -pltpu.roll(slab, half, 1), pltpu.roll(slab, MLA_ROPE_DIM + half, 1))
    o_ref[:, Z_KR:] = jnp.where(lane < MLA_ROPE_DIM, slab, rot).astype(BF16)


def _regroup_w_in(w_in, tm=256):
    _, d, width = w_in.shape
    return pl.pallas_call(
        _regroup_kernel,
        grid=(d // tm,),
        in_specs=[pl.BlockSpec((None, tm, width), lambda i: (0, i, 0))],
        out_specs=pl.BlockSpec((tm, Z_WIDTH), lambda i: (i, 0)),
        out_shape=jax.ShapeDtypeStruct((d, Z_WIDTH), BF16),
        compiler_params=_cparams(("parallel",)),
        name="regroup_w_in",
    )(w_in)


def _inproj_kernel(x_ref, g_ref, w_ref, z_ref):
    x = x_ref[...]
    r = lax.rsqrt(jnp.mean(x * x, axis=-1, keepdims=True) + NORM_EPS)
    h = (x * r * g_ref[...]).astype(BF16)
    z_ref[...] = _dot(h, w_ref[...]).astype(BF16)


def _inproj(x2, g1, w_z, tm=256):
    S = x2.shape[0]
    return pl.pallas_call(
        _inproj_kernel,
        grid=(S // tm,),
        in_specs=[
            pl.BlockSpec((tm, D_MODEL), lambda i: (i, 0)),
            pl.BlockSpec((1, D_MODEL), lambda i: (0, 0)),
            pl.BlockSpec((D_MODEL, Z_WIDTH), lambda i: (0, 0)),
        ],
        out_specs=pl.BlockSpec((tm, Z_WIDTH), lambda i: (i, 0)),
        out_shape=jax.ShapeDtypeStruct((S, Z_WIDTH), BF16),
        compiler_params=_cparams(("parallel",)),
        name="inproj",
    )(x2, g1, w_z)


def _prep_kernel(zdq_ref, zdk_ref, zdv_ref, zcq_ref, zckv_ref, zkr_ref,
                 gcq_ref, gckv_ref, wq_ref, wkv_ref, gq_ref, gkn_ref, gkp_ref,
                 gdq_ref, gdk_ref, t64_ref, cos_ref, sin_ref, p4_ref, p16_ref,
                 q_ref, k_ref, vt_ref, dq_ref, dk_ref,
                 dq4_ref, dk4_ref, dv4_ref, dq16_ref, dk16_ref, dv16_ref):
    tm = zcq_ref.shape[0]
    lane = lax.broadcasted_iota(jnp.int32, (1, LANES), 1)
    first_half = lane < MLA_ROPE_DIM

    def rms_rows(c, g):
        c = c.astype(F32)
        r = lax.rsqrt(jnp.mean(c * c, axis=-1, keepdims=True) + NORM_EPS)
        return (c * r * g).astype(BF16)

    cq = rms_rows(zcq_ref[...], gcq_ref[...])
    ckv = rms_rows(zckv_ref[...], gckv_ref[...])
    qe = _dot(cq, wq_ref[...])
    kv = _dot(ckv, wkv_ref[...])
    kr = zkr_ref[...].astype(F32)
    kr_ss = jnp.sum(jnp.where(first_half, kr * kr, 0.0), axis=-1, keepdims=True)
    t64 = t64_ref[...]
    q_scale = MLA_QK_DIM ** -0.5 * LOG2E

    def rope64(ext):
        t = ext * t64
        return jnp.where(first_half, t + pltpu.roll(t, MLA_ROPE_DIM, 1), 0.0)

    ones_rows = (lax.broadcasted_iota(jnp.int32, (BF16_ROWS, tm), 0) == 0).astype(BF16)
    eye = (lax.broadcasted_iota(jnp.int32, (MLA_V_DIM, MLA_V_DIM), 0)
           == lax.broadcasted_iota(jnp.int32, (MLA_V_DIM, MLA_V_DIM), 1)).astype(BF16)
    for h in range(N_MLA_HEADS):
        base = h * MLA_PAD
        qn = qe[:, base:base + LANES]
        qp = qe[:, base + LANES:base + 2 * LANES]
        ss = (jnp.sum(qn * qn, axis=-1, keepdims=True)
              + jnp.sum(jnp.where(first_half, qp * qp, 0.0), axis=-1, keepdims=True))
        r = lax.rsqrt(ss * (1.0 / MLA_QK_DIM) + NORM_EPS) * q_scale
        q_ref[:, base:base + LANES] = (qn * r * gq_ref[:, :LANES]).astype(BF16)
        q_ref[:, base + LANES:base + MLA_PAD] = rope64(qp * r * gq_ref[:, LANES:]).astype(BF16)

        kn = kv[:, base:base + LANES]
        ss = jnp.sum(kn * kn, axis=-1, keepdims=True) + kr_ss
        r = lax.rsqrt(ss * (1.0 / MLA_QK_DIM) + NORM_EPS)
        k_ref[:, base:base + LANES] = (kn * r * gkn_ref[...]).astype(BF16)
        k_ref[:, base + LANES:base + MLA_PAD] = rope64(kr * r * gkp_ref[...]).astype(BF16)
        v_h = kv[:, base + LANES:base + 2 * LANES].astype(BF16)
        vt_ref[h, :MLA_V_DIM, :] = _dot_nt(eye, v_h).astype(BF16)
        vt_ref[h, MLA_V_DIM:, :] = ones_rows

    cos = cos_ref[...]
    sin = sin_ref[...]
    d_scale = DIL_HEAD_DIM ** -0.5

    def dil_head(x, g, scale):
        x = x.astype(F32)
        r = lax.rsqrt(jnp.mean(x * x, axis=-1, keepdims=True) + NORM_EPS)
        y = x * r * g
        return ((y * cos + pltpu.roll(y, DIL_HEAD_DIM // 2, 1) * sin) * scale).astype(BF16)

    for h in range(N_DIL_HEADS):
        sl = slice(h * DIL_HEAD_DIM, (h + 1) * DIL_HEAD_DIM)
        dq_ref[:, sl] = dil_head(zdq_ref[:, sl], gdq_ref[...], d_scale)
        dk_ref[:, sl] = dil_head(zdk_ref[:, sl], gdk_ref[...], 1.0)

    for src, d4, d16 in ((dq_ref, dq4_ref, dq16_ref), (dk_ref, dk4_ref, dk16_ref), (zdv_ref, dv4_ref, dv16_ref)):
        for sub in range(tm // PERM_TILE):
            xs = src[sub * PERM_TILE:(sub + 1) * PERM_TILE, :]
            for dil, p_ref, dst in ((4, p4_ref, d4), (16, p16_ref, d16)):
                n = PERM_TILE // dil
                xp = _dot(p_ref[...], xs).astype(BF16)
                for r in range(dil):
                    dst[r, sub * n:(sub + 1) * n, :] = xp[r * n:(r + 1) * n, :]


def _prep(z, gcq, gckv, wq_ext, wkv, gq_ext, gk_nope, gk_pe, gdq, gdk, t64, cos128, sin128, p4, p16):
    S = z.shape[0]
    tm = ATT_TK
    row = lambda w, j: pl.BlockSpec((tm, w), lambda i, j=j: (i, j))
    full = lambda a: pl.BlockSpec(a.shape, lambda i: (0, 0))
    res = lambda dil: pl.BlockSpec((dil, tm // dil, HD), lambda i: (0, i, 0))
    qk_w = N_MLA_HEADS * MLA_PAD
    res_shape = lambda dil: jax.ShapeDtypeStruct((dil, S // dil, HD), BF16)
    return pl.pallas_call(
        _prep_kernel,
        grid=(S // tm,),
        in_specs=[
            row(HD, Z_DQ // HD), row(HD, Z_DK // HD), row(HD, Z_DV // HD),
            row(Q_LORA_RANK, Z_CQ // Q_LORA_RANK), row(KV_LORA_RANK, Z_CKV // KV_LORA_RANK),
            row(LANES, Z_KR // LANES),
            full(gcq), full(gckv), full(wq_ext), full(wkv), full(gq_ext), full(gk_nope), full(gk_pe),
            full(gdq), full(gdk),
            row(LANES, 0), row(LANES, 0), row(LANES, 0), full(p4), full(p16),
        ],
        out_specs=[row(qk_w, 0), row(qk_w, 0),
                   pl.BlockSpec((N_MLA_HEADS, None, VT_ROWS, tm), lambda i: (0, i, 0, 0)),
                   row(HD, 0), row(HD, 0),
                   res(4), res(4), res(4), res(16), res(16), res(16)],
        out_shape=[
            jax.ShapeDtypeStruct((S, qk_w), BF16), jax.ShapeDtypeStruct((S, qk_w), BF16),
            jax.ShapeDtypeStruct((N_MLA_HEADS, S // tm, VT_ROWS, tm), BF16),
            jax.ShapeDtypeStruct((S, HD), BF16), jax.ShapeDtypeStruct((S, HD), BF16),
            res_shape(4), res_shape(4), res_shape(4), res_shape(16), res_shape(16), res_shape(16),
        ],
        compiler_params=_cparams(("parallel",)),
        name="qkv_prep",
    )(z, z, z, z, z, z, gcq, gckv, wq_ext, wkv, gq_ext, gk_nope, gk_pe, gdq, gdk, t64, cos128, sin128, p4, p16)


def _mla_attn_kernel(q_ref, k_ref, vt_ref, o_ref, m_ref, acc_ref, s_ref):
    n_chunks = vt_ref.shape[0]
    q = q_ref[...]

    def scores(c):
        off = pl.multiple_of(c * ATT_TK, ATT_TK)
        return _dot_nt(k_ref[pl.ds(off, ATT_TK), :], q)

    m_ref[...] = jnp.full(m_ref.shape, -jnp.inf, F32)
    acc_ref[...] = jnp.zeros(acc_ref.shape, F32)

    def fold(c, slot):
        s = s_ref[slot]
        m_old = m_ref[...]
        m_new = jnp.maximum(m_old, jnp.max(s, axis=0, keepdims=True))
        alpha = jnp.exp2(m_old - m_new)
        p = jnp.exp2(s - m_new).astype(BF16)
        acc_ref[...] = alpha * acc_ref[...] + _dot(vt_ref[c], p)
        m_ref[...] = m_new

    s_ref[0] = scores(0)

    def body(t, carry):
        for u in range(ATT_UNROLL):
            c = ATT_UNROLL * t + u
            s_ref[(u + 1) % 2] = scores(jnp.minimum(c + 1, n_chunks - 1))
            fold(c, u % 2)
        return carry

    lax.fori_loop(0, n_chunks // ATT_UNROLL, body, 0)
    acc = acc_ref[...]
    o_t = acc[:MLA_V_DIM, :] / acc[MLA_V_DIM:MLA_V_DIM + 1, :]
    o_ref[...] = o_t.T.astype(o_ref.dtype)


def _mla_attn(q, k, vt):
    S = q.shape[0]
    n_chunks = vt.shape[1]
    return pl.pallas_call(
        _mla_attn_kernel,
        grid=(N_MLA_HEADS, S // ATT_TQ),
        in_specs=[
            pl.BlockSpec((ATT_TQ, MLA_PAD), lambda h, i: (i, h)),
            pl.BlockSpec((S, MLA_PAD), lambda h, i: (0, h)),
            pl.BlockSpec((None, n_chunks, VT_ROWS, ATT_TK), lambda h, i: (h, 0, 0, 0)),
        ],
        out_specs=pl.BlockSpec((ATT_TQ, MLA_V_DIM), lambda h, i: (i, h)),
        out_shape=jax.ShapeDtypeStruct((S, N_MLA_HEADS * MLA_V_DIM), BF16),
        scratch_shapes=[pltpu.VMEM((1, ATT_TQ), F32), pltpu.VMEM((VT_ROWS, ATT_TQ), F32),
                        pltpu.VMEM((2, ATT_TK, ATT_TQ), F32)],
        compiler_params=_cparams(("parallel", "parallel")),
        name="mla_attn",
    )(q, k, vt)


DIL_QB = 128


def _dil_kernel(q_ref, k_ref, v_ref, o_ref, lse_ref, *, half_w, hb, gb):
    L = q_ref.shape[0]
    kw = DIL_QB + 2 * half_w
    g_id = pl.program_id(1)
    lane = lax.broadcasted_iota(jnp.int32, (DIL_QB, LANES), 1)

    @pl.when(g_id == 0)
    def _():
        lse_ref[...] = jnp.zeros(lse_ref.shape, F32)

    def body(step, carry):
        qs_l, ks_l, qb, kb, vb = [], [], [], [], []
        for j in range(gb):
            qs = pl.multiple_of((step * gb + j) * DIL_QB, DIL_QB)
            ks = pl.multiple_of(jnp.clip(qs - half_w, 0, L - kw), half_w)
            qs_l.append(qs)
            ks_l.append(ks)
            for h in range(hb):
                sl = slice(h * DIL_HEAD_DIM, (h + 1) * DIL_HEAD_DIM)
                qb.append(q_ref[pl.ds(qs, DIL_QB), sl])
                kb.append(k_ref[pl.ds(ks, kw), sl])
                vb.append(v_ref[pl.ds(ks, kw), sl])
        q = jnp.stack(qb)
        k = jnp.stack(kb)
        v = jnp.stack(vb)
        s = jnp.einsum("gqd,gkd->gqk", q, k, preferred_element_type=F32)
        rel = (lax.broadcasted_iota(jnp.int32, (DIL_QB, kw), 0)
               - lax.broadcasted_iota(jnp.int32, (DIL_QB, kw), 1))
        bias = []
        for j in range(gb):
            mask = jnp.abs(rel + (qs_l[j] - ks_l[j])) <= half_w
            bias += [jnp.where(mask, 0.0, NEG_INF)] * hb
        s = s + jnp.stack(bias)
        m = jnp.max(s, axis=-1, keepdims=True)
        p = jnp.exp(s - m)
        den = jnp.sum(p, axis=-1, keepdims=True)
        o = jnp.einsum("gqk,gkd->gqd", p.astype(BF16), v, preferred_element_type=F32) / den
        lse = m + jnp.log(den)
        for j in range(gb):
            tile = lse_ref[pl.ds(qs_l[j], DIL_QB), :]
            for h in range(hb):
                sl = slice(h * DIL_HEAD_DIM, (h + 1) * DIL_HEAD_DIM)
                o_ref[pl.ds(qs_l[j], DIL_QB), sl] = o[j * hb + h].astype(o_ref.dtype)
                tile = jnp.where(lane == g_id * hb + h, lse[j * hb + h], tile)
            lse_ref[pl.ds(qs_l[j], DIL_QB), :] = tile
        return carry

    lax.fori_loop(0, L // (DIL_QB * gb), body, 0)


def _dil_pattern(dq, dk, dv, v_col0, window, dil, hb, gb):
    L = dq.shape[1]
    half_w = window // (2 * dil)
    w = hb * DIL_HEAD_DIM
    spec = lambda c0: pl.BlockSpec((None, L, w), lambda r, g, c0=c0: (r, 0, c0 + g))
    return pl.pallas_call(
        functools.partial(_dil_kernel, half_w=half_w, hb=hb, gb=gb),
        grid=(dil, N_DIL_HEADS // hb),
        in_specs=[spec(0), spec(0), spec(v_col0 // w)],
        out_specs=[spec(0), pl.BlockSpec((None, L, LANES), lambda r, g: (r, 0, 0))],
        out_shape=[jax.ShapeDtypeStruct((dil, L, HD), BF16), jax.ShapeDtypeStruct((dil, L, LANES), F32)],
        compiler_params=_cparams(("parallel", "arbitrary")),
        name=f"dil_attn_d{dil}",
    )(dq, dk, dv)


def _outproj_kernel(a_ref, o1_ref, l1_ref, o4_ref, l4_ref, o16_ref, l16_ref, p4t_ref, p16t_ref,
                    x_ref, wa_ref, wb_ref, g_ref, wr_ref, wrh_ref, br_ref,
                    x1_ref, xn_ref, ri_ref, rf_ref):
    tm = x_ref.shape[0]

    def to_token_order(o_ref, l_ref, pt_ref):
        pt = pt_ref[...]
        o = _dot(pt, o_ref[...].reshape(tm, HD))
        lse = l_ref[...].reshape(tm, LANES)
        hi = lse.astype(BF16)
        rem = lse - hi.astype(F32)
        mid = rem.astype(BF16)
        lo = (rem - mid.astype(F32)).astype(BF16)
        return o, _dot(pt, hi) + _dot(pt, mid) + _dot(pt, lo)

    o1 = o1_ref[...].astype(F32)
    l1 = l1_ref[...]
    o4, l4 = to_token_order(o4_ref, l4_ref, p4t_ref)
    o16, l16 = to_token_order(o16_ref, l16_ref, p16t_ref)
    big = jnp.maximum(jnp.maximum(l1, l4), l16)
    e1 = jnp.exp(l1 - big)
    e4 = jnp.exp(l4 - big)
    e16 = jnp.exp(l16 - big)
    inv = 1.0 / (e1 + e4 + e16)
    w1, w4, w16 = e1 * inv, e4 * inv, e16 * inv
    slabs = []
    for h in range(N_DIL_HEADS):
        sl = slice(h * DIL_HEAD_DIM, (h + 1) * DIL_HEAD_DIM)
        slabs.append(w1[:, h:h + 1] * o1[:, sl] + w4[:, h:h + 1] * o4[:, sl] + w16[:, h:h + 1] * o16[:, sl])
    dil_o = jnp.concatenate(slabs, axis=-1).astype(BF16)

    x1 = x_ref[...] + _dot(a_ref[...], wa_ref[...]) + _dot(dil_o, wb_ref[...])
    x1_ref[...] = x1
    r = lax.rsqrt(jnp.mean(x1 * x1, axis=-1, keepdims=True) + NORM_EPS)
    xn = x1 * r * g_ref[...]
    xn_ref[...] = xn
    xh = xn.astype(BF16)
    xl = (xn - xh.astype(F32)).astype(BF16)
    two = _dot(xh, wr_ref[...])
    logits = two[:, :LANES] + two[:, LANES:] + _dot(xl, wrh_ref[...]) + br_ref[...]
    lane = lax.broadcasted_iota(jnp.int32, logits.shape, 1)
    ninf = -jnp.inf

    def first_argmax(vals, vmax):
        return jnp.min(jnp.where(vals == vmax, lane, LANES), axis=-1, keepdims=True)

    coarse = jnp.where(lane < N_EXPERT_GROUPS, logits, ninf)
    cmax = jnp.max(coarse, axis=-1, keepdims=True)
    g = first_argmax(coarse, cmax)
    p_g = 1.0 / jnp.sum(jnp.exp(coarse - cmax), axis=-1, keepdims=True)
    lo_lane = N_EXPERT_GROUPS + g * EXPERTS_PER_GROUP
    fine = jnp.where((lane >= lo_lane) & (lane < lo_lane + EXPERTS_PER_GROUP), logits, ninf)
    v1 = jnp.max(fine, axis=-1, keepdims=True)
    j1 = first_argmax(fine, v1)
    fine2 = jnp.where(lane == j1, ninf, fine)
    v2 = jnp.max(fine2, axis=-1, keepdims=True)
    j2 = first_argmax(fine2, v2)
    e2 = jnp.exp(v2 - v1)
    g1 = 1.0 / (1.0 + e2)
    g2 = e2 / (1.0 + e2)
    ri_ref[...] = jnp.where(lane == 0, j1 - N_EXPERT_GROUPS, jnp.where(lane == 1, j2 - N_EXPERT_GROUPS, 0))
    rf_ref[...] = jnp.where(lane == 0, p_g * g1, jnp.where(lane == 1, p_g * g2, 0.0))


def _outproj_router(mla_o, dil1, dil4, dil16, p4t, p16t, x2, w_out_bf, g2, wr_two, wr_hi, b_router):
    S = x2.shape[0]
    tm = PERM_TILE
    half = N_MLA_HEADS * MLA_V_DIM
    row = lambda w: pl.BlockSpec((tm, w), lambda i: (i, 0))
    res = lambda dil, w: pl.BlockSpec((dil, tm // dil, w), lambda i: (0, i, 0))
    const = lambda a: pl.BlockSpec(a.shape, lambda i: (0, 0))
    return pl.pallas_call(
        _outproj_kernel,
        grid=(S // tm,),
        in_specs=[
            row(half),
            pl.BlockSpec((None, tm, HD), lambda i: (0, i, 0)), pl.BlockSpec((None, tm, LANES), lambda i: (0, i, 0)),
            res(4, HD), res(4, LANES), res(16, HD), res(16, LANES), const(p4t), const(p16t),
            row(D_MODEL),
            pl.BlockSpec((half, D_MODEL), lambda i: (0, 0)),
            pl.BlockSpec((HD, D_MODEL), lambda i: (1, 0)),
            const(g2), const(wr_two), const(wr_hi), const(b_router),
        ],
        out_specs=[row(D_MODEL), row(D_MODEL), row(LANES), row(LANES)],
        out_shape=[
            jax.ShapeDtypeStruct((S, D_MODEL), F32), jax.ShapeDtypeStruct((S, D_MODEL), F32),
            jax.ShapeDtypeStruct((S, LANES), jnp.int32), jax.ShapeDtypeStruct((S, LANES), F32),
        ],
        compiler_params=_cparams(("parallel",)),
        name="outproj_router",
    )(mla_o, dil1[0], dil1[1], dil4[0], dil4[1], dil16[0], dil16[1], p4t, p16t,
      x2, w_out_bf, w_out_bf, g2, wr_two, wr_hi, b_router)


MOE_CAP = 512
MOE_RH = 128
MOE_KC = 512
MOE_NK = D_MODEL // MOE_KC
MOE_FB = 768
MOE_NB = 2
MOE_FB_LAST = EXPERT_FF - (MOE_NB - 1) * MOE_FB
MOE_STEPS = MOE_NK + MOE_NB
MOE_LOOKAHEAD = 2
MOE_UP_RING = MOE_LOOKAHEAD + 1
MOE_UP_PARTS = ((0, 1024), (1024, 2048), (2048, 2 * EXPERT_FF))
MOE_DOWN_PARTS = ((0, 1024), (1024, D_MODEL))
DMA_UNROLL = 8
assert 0 < MOE_FB_LAST <= MOE_FB and MOE_FB % LANES == 0 and MOE_FB_LAST % LANES == 0


def _moe_kernel(ie_ref, is_ref, in_ref, ord_ref,
                xn_hbm, w1_hbm, w3_hbm, w2_hbm,
                y_hbm, xf_ref, xb_ref, ab_ref, h_ref, acc_ref, w1r, w3r, w2r,
                sem_in, sem_out, sem_up, sem_dn):
    i = pl.program_id(0)
    j = pl.program_id(1)
    n_items = pl.num_programs(0)
    n_tokens = xn_hbm.shape[0]
    n = in_ref[i]
    slot = lax.rem(i, 2)

    def up_ring(item, kc):
        return lax.rem(item * MOE_NK + kc, MOE_UP_RING)

    def block_copies(item, step):
        e = ie_ref[item]
        if step < MOE_NK:
            ring = up_ring(item, step)
            rows = pl.ds(step * MOE_KC, MOE_KC)
            return (pltpu.make_async_copy(w1_hbm.at[e, rows, :], w1r.at[ring], sem_up.at[ring]),
                    pltpu.make_async_copy(w3_hbm.at[e, rows, :], w3r.at[ring], sem_up.at[ring]))
        blk = step - MOE_NK
        size = MOE_FB if blk < MOE_NB - 1 else MOE_FB_LAST
        return (pltpu.make_async_copy(w2_hbm.at[e, pl.ds(blk * MOE_FB, size), :],
                                      w2r.at[blk, pl.ds(0, size), :], sem_dn.at[blk]),)

    def start_block(item, step):
        for c in block_copies(item, step):
            c.start()

    for js in range(MOE_STEPS):
        @pl.when((n > 0) & (j == js))
        def _():
            if js == 0:
                @pl.when(i == 0)
                def _():
                    for s0 in range(MOE_LOOKAHEAD):
                        start_block(0, s0)
            for c in block_copies(i, js):
                c.wait()
            tgt = js + MOE_LOOKAHEAD
            if tgt < MOE_STEPS:
                start_block(i, tgt)
            else:
                nxt = jnp.minimum(i + 1, n_items - 1)

                @pl.when((i + 1 < n_items) & (in_ref[nxt] > 0))
                def _():
                    start_block(nxt, tgt - MOE_STEPS)

    def row_in(item, sl, base, u, width):
        tok = lax.shift_right_logical(ord_ref[is_ref[item] + base + u], 1)
        dst = xf_ref.at[sl, pl.ds(base, width), :].at[pl.ds(u, 1), :]
        return pltpu.make_async_copy(xn_hbm.at[pl.ds(tok, 1), :], dst, sem_in.at[sl])

    def row_out(item, base, u, width):
        a = ord_ref[is_ref[item] + base + u]
        dst = (a & 1) * n_tokens + lax.shift_right_logical(a, 1)
        src = acc_ref.at[pl.ds(base, width), :].at[pl.ds(u, 1), :]
        return pltpu.make_async_copy(src, y_hbm.at[pl.ds(dst, 1), :], sem_out)

    def for_rows(count, fn):
        groups = lax.div(count, DMA_UNROLL)

        def group(g, c):
            base = pl.multiple_of(g * DMA_UNROLL, DMA_UNROLL)
            for u in range(DMA_UNROLL):
                fn(base, u, DMA_UNROLL)
            return c
        lax.fori_loop(0, groups, group, 0)

        def single(r, c):
            fn(r, 0, 1)
            return c
        lax.fori_loop(groups * DMA_UNROLL, count, single, 0)

    def wait_rows(count, rows_desc):
        groups = lax.div(count, DMA_UNROLL)

        def group(g, c):
            rows_desc(pl.multiple_of(g * DMA_UNROLL, DMA_UNROLL), DMA_UNROLL).wait()
            return c
        lax.fori_loop(0, groups, group, 0)

        def single(r, c):
            rows_desc(r, 1).wait()
            return c
        lax.fori_loop(groups * DMA_UNROLL, count, single, 0)

    def rows_in(sl, r0, size):
        return pltpu.make_async_copy(xn_hbm.at[pl.ds(0, size), :], xf_ref.at[sl, pl.ds(r0, size), :], sem_in.at[sl])

    def rows_out(r0, size):
        return pltpu.make_async_copy(acc_ref.at[pl.ds(r0, size), :], y_hbm.at[pl.ds(0, size), :], sem_out)

    def for_row_tiles(fn):
        tiles = lax.div(n + (MOE_RH - 1), MOE_RH)
        for k in range(1, MOE_CAP // MOE_RH + 1):
            @pl.when(tiles == k)
            def _():
                fn(0, k * MOE_RH)

    @pl.when((i == 0) & (j == 0))
    def _():
        xf_ref[...] = jnp.zeros(xf_ref.shape, F32)
        for_rows(n, lambda *row: row_in(0, 0, *row).start())

    @pl.when((n > 0) & (j == 0))
    def _():
        wait_rows(n, functools.partial(rows_in, slot))

        @pl.when(i > 0)
        def _():
            wait_rows(in_ref[i - 1], rows_out)

        for t in range(MOE_CAP // MOE_RH):
            @pl.when(t * MOE_RH < n)
            def _():
                sl = pl.ds(t * MOE_RH, MOE_RH)
                for kc in range(MOE_NK):
                    xb_ref[kc, sl, :] = xf_ref[slot, sl, kc * MOE_KC:(kc + 1) * MOE_KC].astype(BF16)
                ab_ref[sl, :] = jnp.zeros((MOE_RH, 2 * EXPERT_FF), F32)

    @pl.when((n > 0) & (j == 1) & (i + 1 < n_items))
    def _():
        nxt = jnp.minimum(i + 1, n_items - 1)
        for_rows(in_ref[nxt], lambda *row: row_in(nxt, 1 - slot, *row).start())

    @pl.when((n > 0) & (j < MOE_NK))
    def _():
        kc = jnp.minimum(j, MOE_NK - 1)
        ring = up_ring(i, kc)

        def cat_cols(c0, c1):
            pieces = []
            if c0 < EXPERT_FF:
                pieces.append(w1r[ring, :, c0:min(c1, EXPERT_FF)].astype(BF16))
            if c1 > EXPERT_FF:
                pieces.append(w3r[ring, :, max(c0, EXPERT_FF) - EXPERT_FF:c1 - EXPERT_FF].astype(BF16))
            return pieces[0] if len(pieces) == 1 else jnp.concatenate(pieces, axis=-1)

        def up(off, size):
            xs = xb_ref[kc, pl.ds(off, size), :]
            for c0, c1 in MOE_UP_PARTS:
                ab_ref[pl.ds(off, size), c0:c1] += _dot(xs, cat_cols(c0, c1))
        for_row_tiles(up)

    @pl.when((n > 0) & (j == MOE_NK))
    def _():
        def act(off, size):
            a = ab_ref[pl.ds(off, size), :EXPERT_FF]
            b = ab_ref[pl.ds(off, size), EXPERT_FF:]
            half_a = 0.5 * a
            h_ref[pl.ds(off, size), :] = ((half_a + half_a * jnp.tanh(half_a)) * b).astype(BF16)
        for_row_tiles(act)

        def down(off, size):
            hs = h_ref[pl.ds(off, size), :MOE_FB]
            for c0, c1 in MOE_DOWN_PARTS:
                acc_ref[pl.ds(off, size), c0:c1] = _dot(hs, w2r[0, :, c0:c1].astype(BF16))
        for_row_tiles(down)

    @pl.when((n > 0) & (j == MOE_NK + 1))
    def _():
        def down(off, size):
            hs = h_ref[pl.ds(off, size), MOE_FB:]
            for c0, c1 in MOE_DOWN_PARTS:
                acc_ref[pl.ds(off, size), c0:c1] += _dot(hs, w2r[MOE_NB - 1, :MOE_FB_LAST, c0:c1].astype(BF16))
        for_row_tiles(down)
        for_rows(n, lambda *row: row_out(i, *row).start())
        nxt = jnp.minimum(i + 1, n_items - 1)

        @pl.when((i == n_items - 1) | (in_ref[nxt] == 0))
        def _():
            wait_rows(n, rows_out)


def _moe_experts(xn, w1, w3, w2, item_e, item_start, item_n, order):
    S = xn.shape[0]
    n_items = item_e.shape[0]

    hbm = pl.BlockSpec(memory_space=pl.ANY)
    grid_spec = pltpu.PrefetchScalarGridSpec(
        num_scalar_prefetch=4,
        grid=(n_items, MOE_STEPS),
        in_specs=[hbm, hbm, hbm, hbm],
        out_specs=hbm,
        scratch_shapes=[
            pltpu.VMEM((2, MOE_CAP, D_MODEL), F32),
            pltpu.VMEM((MOE_NK, MOE_CAP, MOE_KC), BF16),
            pltpu.VMEM((MOE_CAP, 2 * EXPERT_FF), F32),
            pltpu.VMEM((MOE_CAP, EXPERT_FF), BF16),
            pltpu.VMEM((MOE_CAP, D_MODEL), F32),
            pltpu.VMEM((MOE_UP_RING, MOE_KC, EXPERT_FF), F32),
            pltpu.VMEM((MOE_UP_RING, MOE_KC, EXPERT_FF), F32),
            pltpu.VMEM((MOE_NB, MOE_FB, D_MODEL), F32),
            pltpu.SemaphoreType.DMA((2,)),
            pltpu.SemaphoreType.DMA(()),
            pltpu.SemaphoreType.DMA((MOE_UP_RING,)),
            pltpu.SemaphoreType.DMA((MOE_NB,)),
        ],
    )
    return pl.pallas_call(
        _moe_kernel,
        grid_spec=grid_spec,
        out_shape=jax.ShapeDtypeStruct((TOP_K * S, D_MODEL), F32),
        compiler_params=_cparams(("arbitrary", "arbitrary")),
        name="moe_experts",
    )(item_e, item_start, item_n, order, xn, w1, w3, w2)


def _moe_items(eid, cap):
    A = eid.size
    flat_e = eid.reshape(A)
    order = jnp.argsort(flat_e, stable=True).astype(jnp.int32)
    experts = jnp.arange(N_EXPERTS + 1, dtype=jnp.int32)
    starts = jnp.sum((flat_e[None, :] < experts[:, None]).astype(jnp.int32), axis=1)
    counts = starts[1:] - starts[:-1]
    per_e = (counts + cap - 1) // cap
    item_end = jnp.cumsum(per_e)
    total = item_end[-1]
    n_items = N_EXPERTS + A // cap
    idx = jnp.arange(n_items, dtype=jnp.int32)
    clamped = jnp.minimum(idx, total - 1)
    first_after = jnp.sum((item_end[None, :] <= clamped[:, None]).astype(jnp.int32), axis=1)
    e = jnp.minimum(first_after, N_EXPERTS - 1)
    local = clamped - (item_end[e] - per_e[e])
    used = idx < total
    item_start = jnp.where(used, starts[e] + local * cap, 0).astype(jnp.int32)
    item_n = jnp.where(used, jnp.clip(counts[e] - local * cap, 0, cap), 0).astype(jnp.int32)
    return e, item_start, item_n, order


def _combine_kernel(x1_ref, y0_ref, y1_ref, g_ref, o_ref):
    g = g_ref[...]
    o_ref[...] = x1_ref[...] + (g[:, 0:1] * y0_ref[...] + g[:, 1:2] * y1_ref[...])


def _combine(x1, y, gates, tm=256):
    S = x1.shape[0]
    nb = S // tm
    return pl.pallas_call(
        _combine_kernel,
        grid=(nb,),
        in_specs=[
            pl.BlockSpec((tm, D_MODEL), lambda i: (i, 0)),
            pl.BlockSpec((tm, D_MODEL), lambda i: (i, 0)),
            pl.BlockSpec((tm, D_MODEL), lambda i: (nb + i, 0)),
            pl.BlockSpec((tm, LANES), lambda i: (i, 0)),
        ],
        out_specs=pl.BlockSpec((tm, D_MODEL), lambda i: (i, 0)),
        out_shape=jax.ShapeDtypeStruct((S, D_MODEL), F32),
        compiler_params=_cparams(("parallel",)),
        name="moe_combine",
    )(x1, y, y, gates)


def _rot_half_cols(w, half):
    return jnp.concatenate([-w[..., half:], w[..., :half]], axis=-1)


def _rope_tables(S):
    pos = np.arange(S, dtype=np.float32)[:, None]

    def cs(half):
        inv = (np.float32(ROPE_THETA) ** (-np.arange(half, dtype=np.float32) / np.float32(half))).astype(np.float32)
        ang = (pos * inv[None, :]).astype(np.float64)
        return np.cos(ang).astype(np.float32), np.sin(ang).astype(np.float32)

    c32, s32 = cs(MLA_ROPE_DIM // 2)
    t64 = np.concatenate([c32, c32, s32, s32], axis=-1)
    c64, s64 = cs(DIL_HEAD_DIM // 2)
    cos128 = np.concatenate([c64, c64], axis=-1)
    sin128 = np.concatenate([-s64, s64], axis=-1)
    return jnp.asarray(t64), jnp.asarray(cos128), jnp.asarray(sin128)


def _residue_perm(dil):
    n = PERM_TILE // dil
    rows = np.arange(PERM_TILE)
    src = (rows % n) * dil + rows // n
    return jnp.asarray(src[:, None] == np.arange(PERM_TILE)[None, :], dtype=BF16)


def kernel(x, norm1_g, w_in, g_cq, g_ckv, w_uq, w_ukv, mla_q_norm_g, mla_k_norm_g, dil_q_norm_g,
           dil_k_norm_g, w_out, norm2_g, w_group, b_group, w_expert, b_expert, w1, w3, w2):
    B, S, D = x.shape
    assert B == 1 and D == D_MODEL and norm1_g.shape[0] == 1
    x2 = x.reshape(S, D)
    half_r = MLA_ROPE_DIM // 2
    o1, o2 = Q_LORA_RANK, Q_LORA_RANK + KV_LORA_RANK
    o3 = o2 + MLA_ROPE_DIM

    w_z = _regroup_w_in(w_in)

    wq = w_uq[0].reshape(Q_LORA_RANK, N_MLA_HEADS, MLA_QK_DIM)
    wq_pe = wq[..., MLA_NOPE_DIM:]
    wq_ext = jnp.concatenate([wq, _rot_half_cols(wq_pe, half_r)], axis=-1)
    wq_ext = wq_ext.reshape(Q_LORA_RANK, N_MLA_HEADS * MLA_PAD).astype(BF16)
    wkv = w_ukv[0].astype(BF16)

    def ext_gain(g):
        pe = g[MLA_NOPE_DIM:]
        return jnp.concatenate([g, pe[half_r:], pe[:half_r]])[None, :]

    gq_ext = ext_gain(mla_q_norm_g[0])
    gk_ext = ext_gain(mla_k_norm_g[0])
    t64, cos128, sin128 = _rope_tables(S)
    p4, p16 = _residue_perm(4), _residue_perm(16)

    z = _inproj(x2, norm1_g, w_z)
    q, k, vt, dq, dk, dq4, dk4, dv4, dq16, dk16, dv16 = _prep(
        z, g_cq, g_ckv, wq_ext, wkv, gq_ext, gk_ext[:, :LANES], gk_ext[:, LANES:],
        dil_q_norm_g, dil_k_norm_g, t64, cos128, sin128, p4, p16)
    mla_o = _mla_attn(q, k, vt)

    (w_1, d_1), (w_4, d_4), (w_16, d_16) = DIL_PATTERNS
    dil1 = _dil_pattern(dq[None], dk[None], z[None], Z_DV, w_1, d_1, hb=2, gb=4)
    dil4 = _dil_pattern(dq4, dk4, dv4, 0, w_4, d_4, hb=8, gb=1)
    dil16 = _dil_pattern(dq16, dk16, dv16, 0, w_16, d_16, hb=8, gb=1)

    pad = LANES - N_EXPERT_GROUPS - N_EXPERTS
    w_router = jnp.concatenate([w_group[0], w_expert[0], jnp.zeros((D, pad), F32)], axis=-1)
    b_router = jnp.concatenate([b_group[0], b_expert[0], jnp.zeros((pad,), F32)])[None, :]
    wr_hi = w_router.astype(BF16)
    wr_lo = (w_router - wr_hi.astype(F32)).astype(BF16)
    x1, xn, route_i, route_f = _outproj_router(
        mla_o, dil1, dil4, dil16, p4.T, p16.T, x2, w_out[0].astype(BF16), norm2_g,
        jnp.concatenate([wr_hi, wr_lo], axis=-1), wr_hi, b_router)

    item_e, item_start, item_n, order = _moe_items(route_i[:, :TOP_K], MOE_CAP)
    y = _moe_experts(xn, w1[0], w3[0], w2[0], item_e, item_start, item_n, order)
    out = _combine(x1, y, route_f)
    return out.reshape(B, S, D)
```

```python
import functools
import math

import jax
import jax.numpy as jnp
import numpy as np
from jax import lax
from jax.experimental import pallas as pl
from jax.experimental.pallas import tpu as pltpu

D_MODEL = 2048
N_MLA_HEADS = 8
MLA_NOPE_DIM = 128
MLA_ROPE_DIM = 64
MLA_QK_DIM = MLA_NOPE_DIM + MLA_ROPE_DIM
MLA_V_DIM = 128
Q_LORA_RANK = 512
KV_LORA_RANK = 512
N_DIL_HEADS = 8
DIL_HEAD_DIM = 128
DIL_PATTERNS = ((128, 1), (512, 4), (2048, 16))
ROPE_THETA = 10000.0
NORM_EPS = 1e-6
NEG_INF = -1e30
N_EXPERT_GROUPS = 8
EXPERTS_PER_GROUP = 8
N_EXPERTS = N_EXPERT_GROUPS * EXPERTS_PER_GROUP
TOP_K = 2
EXPERT_FF = 1408

LANES = 128
BF16_ROWS = 16
MXU_DIM = 256
HD = N_DIL_HEADS * DIL_HEAD_DIM
MLA_PAD = MXU_DIM
Z_DQ, Z_DK, Z_DV = 0, HD, 2 * HD
Z_CQ = 3 * HD
Z_CKV = Z_CQ + Q_LORA_RANK
Z_KR = Z_CKV + KV_LORA_RANK
Z_WIDTH = Z_KR + 2 * MLA_ROPE_DIM

ATT_VC = 512
ATT_TK = 512
ATT_TQ = 1024
ATT_UNROLL = 8
VT_ROWS = MLA_V_DIM + BF16_ROWS
PERM_TILE = 256
LOG2E = math.log2(math.e)

VMEM_LIMIT = 56 * 1024 * 1024

F32 = jnp.float32
BF16 = jnp.bfloat16


def _cparams(sem, vmem=VMEM_LIMIT):
    return pltpu.CompilerParams(dimension_semantics=sem, vmem_limit_bytes=vmem)


def _dot(a, b):
    return jnp.dot(a, b, preferred_element_type=F32)


def _dot_nt(a, b):
    return lax.dot_general(a, b, (((1,), (1,)), ((), ())), preferred_element_type=F32)


def _regroup_kernel(w_ref, o_ref):
    lat = Q_LORA_RANK + KV_LORA_RANK
    o_ref[:, Z_DQ:Z_CQ] = w_ref[:, lat + MLA_ROPE_DIM:].astype(BF16)
    o_ref[:, Z_CQ:Z_KR] = w_ref[:, :lat].astype(BF16)
    slab = w_ref[:, lat:lat + LANES]
    lane = lax.broadcasted_iota(jnp.int32, slab.shape, 1)
    half = MLA_ROPE_DIM // 2
    rot = jnp.where(lane < MLA_ROPE_DIM + half, -pltpu.roll(slab, half, 1), pltpu.roll(slab, MLA_ROPE_DIM + half, 1))
    o_ref[:, Z_KR:] = jnp.where(lane < MLA_ROPE_DIM, slab, rot).astype(BF16)


def _regroup_w_in(w_in, tm=256):
    _, d, width = w_in.shape
    return pl.pallas_call(
        _regroup_kernel,
        grid=(d // tm,),
        in_specs=[pl.BlockSpec((None, tm, width), lambda i: (0, i, 0))],
        out_specs=pl.BlockSpec((tm, Z_WIDTH), lambda i: (i, 0)),
        out_shape=jax.ShapeDtypeStruct((d, Z_WIDTH), BF16),
        compiler_params=_cparams(("parallel",)),
        name="regroup_w_in",
    )(w_in)


def _inproj_kernel(x_ref, g_ref, w_ref, z_ref):
    x = x_ref[...]
    r = lax.rsqrt(jnp.mean(x * x, axis=-1, keepdims=True) + NORM_EPS)
    h = (x * r * g_ref[...]).astype(BF16)
    z_ref[...] = _dot(h, w_ref[...]).astype(BF16)


def _inproj(x2, g1, w_z, tm=256):
    S = x2.shape[0]
    return pl.pallas_call(
        _inproj_kernel,
        grid=(S // tm,),
        in_specs=[
            pl.BlockSpec((tm, D_MODEL), lambda i: (i, 0)),
            pl.BlockSpec((1, D_MODEL), lambda i: (0, 0)),
            pl.BlockSpec((D_MODEL, Z_WIDTH), lambda i: (0, 0)),
        ],
        out_specs=pl.BlockSpec((tm, Z_WIDTH), lambda i: (i, 0)),
        out_shape=jax.ShapeDtypeStruct((S, Z_WIDTH), BF16),
        compiler_params=_cparams(("parallel",)),
        name="inproj",
    )(x2, g1, w_z)


def _prep_kernel(zdq_ref, zdk_ref, zdv_ref, zcq_ref, zckv_ref, zkr_ref,
                 gcq_ref, gckv_ref, wq_ref, wkv_ref, gq_ref, gkn_ref, gkp_ref,
                 gdq_ref, gdk_ref, t64_ref, cos_ref, sin_ref, p4_ref, p16_ref,
                 q_ref, k_ref, vt_ref, dq_ref, dk_ref,
                 dq4_ref, dk4_ref, dv4_ref, dq16_ref, dk16_ref, dv16_ref):
    tm = zcq_ref.shape[0]
    lane = lax.broadcasted_iota(jnp.int32, (1, LANES), 1)
    first_half = lane < MLA_ROPE_DIM

    def rms_rows(c, g):
        c = c.astype(F32)
        r = lax.rsqrt(jnp.mean(c * c, axis=-1, keepdims=True) + NORM_EPS)
        return (c * r * g).astype(BF16)

    cq = rms_rows(zcq_ref[...], gcq_ref[...])
    ckv = rms_rows(zckv_ref[...], gckv_ref[...])
    qe = _dot(cq, wq_ref[...])
    kv = _dot(ckv, wkv_ref[...])
    kr = zkr_ref[...].astype(F32)
    kr_ss = jnp.sum(jnp.where(first_half, kr * kr, 0.0), axis=-1, keepdims=True)
    t64 = t64_ref[...]
    q_scale = MLA_QK_DIM ** -0.5 * LOG2E

    def rope64(ext):
        t = ext * t64
        return jnp.where(first_half, t + pltpu.roll(t, MLA_ROPE_DIM, 1), 0.0)

    ones_rows = (lax.broadcasted_iota(jnp.int32, (BF16_ROWS, tm), 0) == 0).astype(BF16)
    eye = (lax.broadcasted_iota(jnp.int32, (MLA_V_DIM, MLA_V_DIM), 0)
           == lax.broadcasted_iota(jnp.int32, (MLA_V_DIM, MLA_V_DIM), 1)).astype(BF16)
    for h in range(N_MLA_HEADS):
        base = h * MLA_PAD
        qn = qe[:, base:base + LANES]
        qp = qe[:, base + LANES:base + 2 * LANES]
        ss = (jnp.sum(qn * qn, axis=-1, keepdims=True)
              + jnp.sum(jnp.where(first_half, qp * qp, 0.0), axis=-1, keepdims=True))
        r = lax.rsqrt(ss * (1.0 / MLA_QK_DIM) + NORM_EPS) * q_scale
        q_ref[:, base:base + LANES] = (qn * r * gq_ref[:, :LANES]).astype(BF16)
        q_ref[:, base + LANES:base + MLA_PAD] = rope64(qp * r * gq_ref[:, LANES:]).astype(BF16)

        kn = kv[:, base:base + LANES]
        ss = jnp.sum(kn * kn, axis=-1, keepdims=True) + kr_ss
        r = lax.rsqrt(ss * (1.0 / MLA_QK_DIM) + NORM_EPS)
        k_ref[:, base:base + LANES] = (kn * r * gkn_ref[...]).astype(BF16)
        k_ref[:, base + LANES:base + MLA_PAD] = rope64(kr * r * gkp_ref[...]).astype(BF16)
        v_h = kv[:, base + LANES:base + 2 * LANES].astype(BF16)
        vt_ref[h, :MLA_V_DIM, :] = _dot_nt(eye, v_h).astype(BF16)
        vt_ref[h, MLA_V_DIM:, :] = ones_rows

    cos = cos_ref[...]
    sin = sin_ref[...]
    d_scale = DIL_HEAD_DIM ** -0.5

    def dil_head(x, g, scale):
        x = x.astype(F32)
        r = lax.rsqrt(jnp.mean(x * x, axis=-1, keepdims=True) + NORM_EPS)
        y = x * r * g
        return ((y * cos + pltpu.roll(y, DIL_HEAD_DIM // 2, 1) * sin) * scale).astype(BF16)

    for h in range(N_DIL_HEADS):
        sl = slice(h * DIL_HEAD_DIM, (h + 1) * DIL_HEAD_DIM)
        dq_ref[:, sl] = dil_head(zdq_ref[:, sl], gdq_ref[...], d_scale)
        dk_ref[:, sl] = dil_head(zdk_ref[:, sl], gdk_ref[...], 1.0)

    for src, d4, d16 in ((dq_ref, dq4_ref, dq16_ref), (dk_ref, dk4_ref, dk16_ref), (zdv_ref, dv4_ref, dv16_ref)):
        for sub in range(tm // PERM_TILE):
            xs = src[sub * PERM_TILE:(sub + 1) * PERM_TILE, :]
            for dil, p_ref, dst in ((4, p4_ref, d4), (16, p16_ref, d16)):
                n = PERM_TILE // dil
                xp = _dot(p_ref[...], xs).astype(BF16)
                for r in range(dil):
                    dst[r, sub * n:(sub + 1) * n, :] = xp[r * n:(r + 1) * n, :]


def _prep(z, gcq, gckv, wq_ext, wkv, gq_ext, gk_nope, gk_pe, gdq, gdk, t64, cos128, sin128, p4, p16):
    S = z.shape[0]
    tm = ATT_VC
    row = lambda w, j: pl.BlockSpec((tm, w), lambda i, j=j: (i, j))
    full = lambda a: pl.BlockSpec(a.shape, lambda i: (0, 0))
    res = lambda dil: pl.BlockSpec((dil, tm // dil, HD), lambda i: (0, i, 0))
    qk_w = N_MLA_HEADS * MLA_PAD
    res_shape = lambda dil: jax.ShapeDtypeStruct((dil, S // dil, HD), BF16)
    return pl.pallas_call(
        _prep_kernel,
        grid=(S // tm,),
        in_specs=[
            row(HD, Z_DQ // HD), row(HD, Z_DK // HD), row(HD, Z_DV // HD),
            row(Q_LORA_RANK, Z_CQ // Q_LORA_RANK), row(KV_LORA_RANK, Z_CKV // KV_LORA_RANK),
            row(LANES, Z_KR // LANES),
            full(gcq), full(gckv), full(wq_ext), full(wkv), full(gq_ext), full(gk_nope), full(gk_pe),
            full(gdq), full(gdk),
            row(LANES, 0), row(LANES, 0), row(LANES, 0), full(p4), full(p16),
        ],
        out_specs=[row(qk_w, 0), row(qk_w, 0),
                   pl.BlockSpec((N_MLA_HEADS, None, VT_ROWS, tm), lambda i: (0, i, 0, 0)),
                   row(HD, 0), row(HD, 0),
                   res(4), res(4), res(4), res(16), res(16), res(16)],
        out_shape=[
            jax.ShapeDtypeStruct((S, qk_w), BF16), jax.ShapeDtypeStruct((S, qk_w), BF16),
            jax.ShapeDtypeStruct((N_MLA_HEADS, S // tm, VT_ROWS, tm), BF16),
            jax.ShapeDtypeStruct((S, HD), BF16), jax.ShapeDtypeStruct((S, HD), BF16),
            res_shape(4), res_shape(4), res_shape(4), res_shape(16), res_shape(16), res_shape(16),
        ],
        compiler_params=_cparams(("parallel",)),
        name="qkv_prep",
    )(z, z, z, z, z, z, gcq, gckv, wq_ext, wkv, gq_ext, gk_nope, gk_pe, gdq, gdk, t64, cos128, sin128, p4, p16)


def _mla_attn_kernel(q_ref, k_ref, vt_ref, o_ref, m_ref, acc_ref, s_ref):
    per_chunk = ATT_TK // ATT_VC
    n_chunks = vt_ref.shape[0] // per_chunk
    q = q_ref[...]

    def scores(c):
        off = pl.multiple_of(c * ATT_TK, ATT_TK)
        return _dot_nt(k_ref[pl.ds(off, ATT_TK), :], q)

    m_ref[...] = jnp.full(m_ref.shape, -jnp.inf, F32)
    acc_ref[...] = jnp.zeros(acc_ref.shape, F32)

    def fold(c, slot):
        s = s_ref[slot]
        m_old = m_ref[...]
        m_new = jnp.maximum(m_old, jnp.max(s, axis=0, keepdims=True))
        alpha = jnp.exp2(m_old - m_new)
        p = jnp.exp2(s - m_new).astype(BF16)
        pv = _dot(vt_ref[c * per_chunk], p[:ATT_VC])
        for u in range(1, per_chunk):
            pv += _dot(vt_ref[c * per_chunk + u], p[u * ATT_VC:(u + 1) * ATT_VC])
        acc_ref[...] = alpha * acc_ref[...] + pv
        m_ref[...] = m_new

    s_ref[0] = scores(0)

    def body(t, carry):
        for u in range(ATT_UNROLL):
            c = ATT_UNROLL * t + u
            s_ref[(u + 1) % 2] = scores(jnp.minimum(c + 1, n_chunks - 1))
            fold(c, u % 2)
        return carry

    lax.fori_loop(0, n_chunks // ATT_UNROLL, body, 0)
    acc = acc_ref[...]
    o_t = acc[:MLA_V_DIM, :] / acc[MLA_V_DIM:MLA_V_DIM + 1, :]
    o_ref[...] = o_t.T.astype(o_ref.dtype)


def _mla_attn(q, k, vt):
    S = q.shape[0]
    n_chunks = vt.shape[1]
    return pl.pallas_call(
        _mla_attn_kernel,
        grid=(N_MLA_HEADS, S // ATT_TQ),
        in_specs=[
            pl.BlockSpec((ATT_TQ, MLA_PAD), lambda h, i: (i, h)),
            pl.BlockSpec((S, MLA_PAD), lambda h, i: (0, h)),
            pl.BlockSpec((None, n_chunks, VT_ROWS, ATT_VC), lambda h, i: (h, 0, 0, 0)),
        ],
        out_specs=pl.BlockSpec((ATT_TQ, MLA_V_DIM), lambda h, i: (i, h)),
        out_shape=jax.ShapeDtypeStruct((S, N_MLA_HEADS * MLA_V_DIM), BF16),
        scratch_shapes=[pltpu.VMEM((1, ATT_TQ), F32), pltpu.VMEM((VT_ROWS, ATT_TQ), F32),
                        pltpu.VMEM((2, ATT_TK, ATT_TQ), F32)],
        compiler_params=_cparams(("parallel", "parallel")),
        name="mla_attn",
    )(q, k, vt)


DIL_QB = 128


def _dil_kernel(q_ref, k_ref, v_ref, o_ref, lse_ref, *, half_w, hb, gb):
    L = q_ref.shape[0]
    kw = DIL_QB + 2 * half_w
    g_id = pl.program_id(1)
    lane = lax.broadcasted_iota(jnp.int32, (DIL_QB, LANES), 1)

    @pl.when(g_id == 0)
    def _():
        lse_ref[...] = jnp.zeros(lse_ref.shape, F32)

    def body(step, carry):
        qs_l, ks_l, qb, kb, vb = [], [], [], [], []
        for j in range(gb):
            qs = pl.multiple_of((step * gb + j) * DIL_QB, DIL_QB)
            ks = pl.multiple_of(jnp.clip(qs - half_w, 0, L - kw), half_w)
            qs_l.append(qs)
            ks_l.append(ks)
            for h in range(hb):
                sl = slice(h * DIL_HEAD_DIM, (h + 1) * DIL_HEAD_DIM)
                qb.append(q_ref[pl.ds(qs, DIL_QB), sl])
                kb.append(k_ref[pl.ds(ks, kw), sl])
                vb.append(v_ref[pl.ds(ks, kw), sl])
        q = jnp.stack(qb)
        k = jnp.stack(kb)
        v = jnp.stack(vb)
        s = jnp.einsum("gqd,gkd->gqk", q, k, preferred_element_type=F32)
        rel = (lax.broadcasted_iota(jnp.int32, (DIL_QB, kw), 0)
               - lax.broadcasted_iota(jnp.int32, (DIL_QB, kw), 1))
        bias = []
        for j in range(gb):
            mask = jnp.abs(rel + (qs_l[j] - ks_l[j])) <= half_w
            bias += [jnp.where(mask, 0.0, NEG_INF)] * hb
        s = s + jnp.stack(bias)
        m = jnp.max(s, axis=-1, keepdims=True)
        p = jnp.exp(s - m)
        den = jnp.sum(p, axis=-1, keepdims=True)
        o = jnp.einsum("gqk,gkd->gqd", p.astype(BF16), v, preferred_element_type=F32) / den
        lse = m + jnp.log(den)
        for j in range(gb):
            tile = lse_ref[pl.ds(qs_l[j], DIL_QB), :]
            for h in range(hb):
                sl = slice(h * DIL_HEAD_DIM, (h + 1) * DIL_HEAD_DIM)
                o_ref[pl.ds(qs_l[j], DIL_QB), sl] = o[j * hb + h].astype(o_ref.dtype)
                tile = jnp.where(lane == g_id * hb + h, lse[j * hb + h], tile)
            lse_ref[pl.ds(qs_l[j], DIL_QB), :] = tile
        return carry

    lax.fori_loop(0, L // (DIL_QB * gb), body, 0)


def _dil_pattern(dq, dk, dv, v_col0, window, dil, hb, gb):
    L = dq.shape[1]
    half_w = window // (2 * dil)
    w = hb * DIL_HEAD_DIM
    spec = lambda c0: pl.BlockSpec((None, L, w), lambda r, g, c0=c0: (r, 0, c0 + g))
    return pl.pallas_call(
        functools.partial(_dil_kernel, half_w=half_w, hb=hb, gb=gb),
        grid=(dil, N_DIL_HEADS // hb),
        in_specs=[spec(0), spec(0), spec(v_col0 // w)],
        out_specs=[spec(0), pl.BlockSpec((None, L, LANES), lambda r, g: (r, 0, 0))],
        out_shape=[jax.ShapeDtypeStruct((dil, L, HD), BF16), jax.ShapeDtypeStruct((dil, L, LANES), F32)],
        compiler_params=_cparams(("parallel", "arbitrary")),
        name=f"dil_attn_d{dil}",
    )(dq, dk, dv)


def _outproj_kernel(a_ref, o1_ref, l1_ref, o4_ref, l4_ref, o16_ref, l16_ref, p4t_ref, p16t_ref,
                    x_ref, wa_ref, wb_ref, g_ref, wr_ref, wrh_ref, br_ref,
                    x1_ref, xn_ref, ri_ref, rf_ref):
    tm = x_ref.shape[0]

    def to_token_order(o_ref, l_ref, pt_ref):
        pt = pt_ref[...]
        o = _dot(pt, o_ref[...].reshape(tm, HD))
        lse = l_ref[...].reshape(tm, LANES)
        hi = lse.astype(BF16)
        rem = lse - hi.astype(F32)
        mid = rem.astype(BF16)
        lo = (rem - mid.astype(F32)).astype(BF16)
        return o, _dot(pt, hi) + _dot(pt, mid) + _dot(pt, lo)

    o1 = o1_ref[...].astype(F32)
    l1 = l1_ref[...]
    o4, l4 = to_token_order(o4_ref, l4_ref, p4t_ref)
    o16, l16 = to_token_order(o16_ref, l16_ref, p16t_ref)
    big = jnp.maximum(jnp.maximum(l1, l4), l16)
    e1 = jnp.exp(l1 - big)
    e4 = jnp.exp(l4 - big)
    e16 = jnp.exp(l16 - big)
    inv = 1.0 / (e1 + e4 + e16)
    w1, w4, w16 = e1 * inv, e4 * inv, e16 * inv
    slabs = []
    for h in range(N_DIL_HEADS):
        sl = slice(h * DIL_HEAD_DIM, (h + 1) * DIL_HEAD_DIM)
        slabs.append(w1[:, h:h + 1] * o1[:, sl] + w4[:, h:h + 1] * o4[:, sl] + w16[:, h:h + 1] * o16[:, sl])
    dil_o = jnp.concatenate(slabs, axis=-1).astype(BF16)

    x1 = x_ref[...] + _dot(a_ref[...], wa_ref[...]) + _dot(dil_o, wb_ref[...])
    x1_ref[...] = x1
    r = lax.rsqrt(jnp.mean(x1 * x1, axis=-1, keepdims=True) + NORM_EPS)
    xn = x1 * r * g_ref[...]
    xn_ref[...] = xn
    xh = xn.astype(BF16)
    xl = (xn - xh.astype(F32)).astype(BF16)
    two = _dot(xh, wr_ref[...])
    logits = two[:, :LANES] + two[:, LANES:] + _dot(xl, wrh_ref[...]) + br_ref[...]
    lane = lax.broadcasted_iota(jnp.int32, logits.shape, 1)
    ninf = -jnp.inf

    def first_argmax(vals, vmax):
        return jnp.min(jnp.where(vals == vmax, lane, LANES), axis=-1, keepdims=True)

    coarse = jnp.where(lane < N_EXPERT_GROUPS, logits, ninf)
    cmax = jnp.max(coarse, axis=-1, keepdims=True)
    g = first_argmax(coarse, cmax)
    p_g = 1.0 / jnp.sum(jnp.exp(coarse - cmax), axis=-1, keepdims=True)
    lo_lane = N_EXPERT_GROUPS + g * EXPERTS_PER_GROUP
    fine = jnp.where((lane >= lo_lane) & (lane < lo_lane + EXPERTS_PER_GROUP), logits, ninf)
    v1 = jnp.max(fine, axis=-1, keepdims=True)
    j1 = first_argmax(fine, v1)
    fine2 = jnp.where(lane == j1, ninf, fine)
    v2 = jnp.max(fine2, axis=-1, keepdims=True)
    j2 = first_argmax(fine2, v2)
    e2 = jnp.exp(v2 - v1)
    g1 = 1.0 / (1.0 + e2)
    g2 = e2 / (1.0 + e2)
    ri_ref[...] = jnp.where(lane == 0, j1 - N_EXPERT_GROUPS, jnp.where(lane == 1, j2 - N_EXPERT_GROUPS, 0))
    rf_ref[...] = jnp.where(lane == 0, p_g * g1, jnp.where(lane == 1, p_g * g2, 0.0))


def _outproj_router(mla_o, dil1, dil4, dil16, p4t, p16t, x2, w_out_bf, g2, wr_two, wr_hi, b_router):
    S = x2.shape[0]
    tm = PERM_TILE
    half = N_MLA_HEADS * MLA_V_DIM
    row = lambda w: pl.BlockSpec((tm, w), lambda i: (i, 0))
    res = lambda dil, w: pl.BlockSpec((dil, tm // dil, w), lambda i: (0, i, 0))
    const = lambda a: pl.BlockSpec(a.shape, lambda i: (0, 0))
    return pl.pallas_call(
        _outproj_kernel,
        grid=(S // tm,),
        in_specs=[
            row(half),
            pl.BlockSpec((None, tm, HD), lambda i: (0, i, 0)), pl.BlockSpec((None, tm, LANES), lambda i: (0, i, 0)),
            res(4, HD), res(4, LANES), res(16, HD), res(16, LANES), const(p4t), const(p16t),
            row(D_MODEL),
            pl.BlockSpec((half, D_MODEL), lambda i: (0, 0)),
            pl.BlockSpec((HD, D_MODEL), lambda i: (1, 0)),
            const(g2), const(wr_two), const(wr_hi), const(b_router),
        ],
        out_specs=[row(D_MODEL), row(D_MODEL), row(LANES), row(LANES)],
        out_shape=[
            jax.ShapeDtypeStruct((S, D_MODEL), F32), jax.ShapeDtypeStruct((S, D_MODEL), F32),
            jax.ShapeDtypeStruct((S, LANES), jnp.int32), jax.ShapeDtypeStruct((S, LANES), F32),
        ],
        compiler_params=_cparams(("parallel",)),
        name="outproj_router",
    )(mla_o, dil1[0], dil1[1], dil4[0], dil4[1], dil16[0], dil16[1], p4t, p16t,
      x2, w_out_bf, w_out_bf, g2, wr_two, wr_hi, b_router)


MOE_CAP = 512
MOE_RH = 128
MOE_KC = 512
MOE_NK = D_MODEL // MOE_KC
MOE_FB = 768
MOE_NB = 2
MOE_FB_LAST = EXPERT_FF - (MOE_NB - 1) * MOE_FB
MOE_STEPS = MOE_NK + MOE_NB
MOE_LOOKAHEAD = 3
MOE_UP_RING = MOE_LOOKAHEAD + 1
MOE_UP_PARTS = ((0, 1024), (1024, 2048), (2048, 2 * EXPERT_FF))
MOE_DOWN_PARTS = ((0, 1024), (1024, D_MODEL))
DMA_UNROLL = 8
assert 0 < MOE_FB_LAST <= MOE_FB and MOE_FB % LANES == 0 and MOE_FB_LAST % LANES == 0


def _moe_kernel(ie_ref, is_ref, in_ref, ord_ref,
                xn_hbm, w1_hbm, w3_hbm, w2_hbm,
                y_hbm, xf_ref, xb_ref, ab_ref, h_ref, acc_ref, w1r, w3r, w2r,
                sem_in, sem_out, sem_up, sem_dn):
    i = pl.program_id(0)
    j = pl.program_id(1)
    n_items = pl.num_programs(0)
    n_tokens = xn_hbm.shape[0]
    n = in_ref[i]
    slot = 0

    def up_ring(item, kc):
        return lax.rem(item * MOE_NK + kc, MOE_UP_RING)

    def block_copies(item, step):
        e = ie_ref[item]
        if step < MOE_NK:
            ring = up_ring(item, step)
            rows = pl.ds(step * MOE_KC, MOE_KC)
            return (pltpu.make_async_copy(w1_hbm.at[e, rows, :], w1r.at[ring], sem_up.at[ring]),
                    pltpu.make_async_copy(w3_hbm.at[e, rows, :], w3r.at[ring], sem_up.at[ring]))
        blk = step - MOE_NK
        size = MOE_FB if blk < MOE_NB - 1 else MOE_FB_LAST
        return (pltpu.make_async_copy(w2_hbm.at[e, pl.ds(blk * MOE_FB, size), :],
                                      w2r.at[blk, pl.ds(0, size), :], sem_dn.at[blk]),)

    def start_block(item, step):
        for c in block_copies(item, step):
            c.start()

    for js in range(MOE_STEPS):
        @pl.when((n > 0) & (j == js))
        def _():
            if js == 0:
                @pl.when(i == 0)
                def _():
                    for s0 in range(MOE_LOOKAHEAD):
                        start_block(0, s0)
            for c in block_copies(i, js):
                c.wait()
            tgt = js + MOE_LOOKAHEAD
            if tgt < MOE_STEPS:
                start_block(i, tgt)
            else:
                nxt = jnp.minimum(i + 1, n_items - 1)

                @pl.when((i + 1 < n_items) & (in_ref[nxt] > 0))
                def _():
                    start_block(nxt, tgt - MOE_STEPS)

    def row_in(item, sl, base, u, width):
        tok = lax.shift_right_logical(ord_ref[is_ref[item] + base + u], 1)
        dst = xf_ref.at[sl, pl.ds(base, width), :].at[pl.ds(u, 1), :]
        return pltpu.make_async_copy(xn_hbm.at[pl.ds(tok, 1), :], dst, sem_in.at[sl])

    def row_out(item, base, u, width):
        a = ord_ref[is_ref[item] + base + u]
        dst = (a & 1) * n_tokens + lax.shift_right_logical(a, 1)
        src = acc_ref.at[pl.ds(base, width), :].at[pl.ds(u, 1), :]
        return pltpu.make_async_copy(src, y_hbm.at[pl.ds(dst, 1), :], sem_out)

    def for_rows(count, fn):
        groups = lax.div(count, DMA_UNROLL)

        def group(g, c):
            base = pl.multiple_of(g * DMA_UNROLL, DMA_UNROLL)
            for u in range(DMA_UNROLL):
                fn(base, u, DMA_UNROLL)
            return c
        lax.fori_loop(0, groups, group, 0)

        def single(r, c):
            fn(r, 0, 1)
            return c
        lax.fori_loop(groups * DMA_UNROLL, count, single, 0)

    def wait_rows(count, rows_desc):
        groups = lax.div(count, DMA_UNROLL)

        def group(g, c):
            rows_desc(pl.multiple_of(g * DMA_UNROLL, DMA_UNROLL), DMA_UNROLL).wait()
            return c
        lax.fori_loop(0, groups, group, 0)

        def single(r, c):
            rows_desc(r, 1).wait()
            return c
        lax.fori_loop(groups * DMA_UNROLL, count, single, 0)

    def rows_in(sl, r0, size):
        return pltpu.make_async_copy(xn_hbm.at[pl.ds(0, size), :], xf_ref.at[sl, pl.ds(r0, size), :], sem_in.at[sl])

    def rows_out(r0, size):
        return pltpu.make_async_copy(acc_ref.at[pl.ds(r0, size), :], y_hbm.at[pl.ds(0, size), :], sem_out)

    def for_row_tiles(fn):
        tiles = lax.div(n + (MOE_RH - 1), MOE_RH)
        for k in range(1, MOE_CAP // MOE_RH + 1):
            @pl.when(tiles == k)
            def _():
                fn(0, k * MOE_RH)

    @pl.when((i == 0) & (j == 0))
    def _():
        xf_ref[...] = jnp.zeros(xf_ref.shape, F32)
        for_rows(n, lambda *row: row_in(0, 0, *row).start())

    @pl.when((n > 0) & (j == 0))
    def _():
        wait_rows(n, functools.partial(rows_in, slot))

        @pl.when(i > 0)
        def _():
            wait_rows(in_ref[i - 1], rows_out)

        for t in range(MOE_CAP // MOE_RH):
            @pl.when(t * MOE_RH < n)
            def _():
                sl = pl.ds(t * MOE_RH, MOE_RH)
                for kc in range(MOE_NK):
                    xb_ref[kc, sl, :] = xf_ref[slot, sl, kc * MOE_KC:(kc + 1) * MOE_KC].astype(BF16)
                ab_ref[sl, :] = jnp.zeros((MOE_RH, 2 * EXPERT_FF), F32)

    @pl.when((n > 0) & (j == 1) & (i + 1 < n_items))
    def _():
        nxt = jnp.minimum(i + 1, n_items - 1)
        for_rows(in_ref[nxt], lambda *row: row_in(nxt, slot, *row).start())

    @pl.when((n > 0) & (j < MOE_NK))
    def _():
        kc = jnp.minimum(j, MOE_NK - 1)
        ring = up_ring(i, kc)

        def cat_cols(c0, c1):
            pieces = []
            if c0 < EXPERT_FF:
                pieces.append(w1r[ring, :, c0:min(c1, EXPERT_FF)].astype(BF16))
            if c1 > EXPERT_FF:
                pieces.append(w3r[ring, :, max(c0, EXPERT_FF) - EXPERT_FF:c1 - EXPERT_FF].astype(BF16))
            return pieces[0] if len(pieces) == 1 else jnp.concatenate(pieces, axis=-1)

        def up(off, size):
            xs = xb_ref[kc, pl.ds(off, size), :]
            for c0, c1 in MOE_UP_PARTS:
                ab_ref[pl.ds(off, size), c0:c1] += _dot(xs, cat_cols(c0, c1))
        for_row_tiles(up)

    @pl.when((n > 0) & (j == MOE_NK))
    def _():
        def act(off, size):
            a = ab_ref[pl.ds(off, size), :EXPERT_FF]
            b = ab_ref[pl.ds(off, size), EXPERT_FF:]
            half_a = 0.5 * a
            h_ref[pl.ds(off, size), :] = ((half_a + half_a * jnp.tanh(half_a)) * b).astype(BF16)
        for_row_tiles(act)

        def down(off, size):
            hs = h_ref[pl.ds(off, size), :MOE_FB]
            for c0, c1 in MOE_DOWN_PARTS:
                acc_ref[pl.ds(off, size), c0:c1] = _dot(hs, w2r[0, :, c0:c1].astype(BF16))
        for_row_tiles(down)

    @pl.when((n > 0) & (j == MOE_NK + 1))
    def _():
        def down(off, size):
            hs = h_ref[pl.ds(off, size), MOE_FB:]
            for c0, c1 in MOE_DOWN_PARTS:
                acc_ref[pl.ds(off, size), c0:c1] += _dot(hs, w2r[MOE_NB - 1, :MOE_FB_LAST, c0:c1].astype(BF16))
        for_row_tiles(down)
        for_rows(n, lambda *row: row_out(i, *row).start())
        nxt = jnp.minimum(i + 1, n_items - 1)

        @pl.when((i == n_items - 1) | (in_ref[nxt] == 0))
        def _():
            wait_rows(n, rows_out)


def _moe_experts(xn, w1, w3, w2, item_e, item_start, item_n, order):
    S = xn.shape[0]
    n_items = item_e.shape[0]

    hbm = pl.BlockSpec(memory_space=pl.ANY)
    grid_spec = pltpu.PrefetchScalarGridSpec(
        num_scalar_prefetch=4,
        grid=(n_items, MOE_STEPS),
        in_specs=[hbm, hbm, hbm, hbm],
        out_specs=hbm,
        scratch_shapes=[
            pltpu.VMEM((1, MOE_CAP, D_MODEL), F32),
            pltpu.VMEM((MOE_NK, MOE_CAP, MOE_KC), BF16),
            pltpu.VMEM((MOE_CAP, 2 * EXPERT_FF), F32),
            pltpu.VMEM((MOE_CAP, EXPERT_FF), BF16),
            pltpu.VMEM((MOE_CAP, D_MODEL), F32),
            pltpu.VMEM((MOE_UP_RING, MOE_KC, EXPERT_FF), F32),
            pltpu.VMEM((MOE_UP_RING, MOE_KC, EXPERT_FF), F32),
            pltpu.VMEM((MOE_NB, MOE_FB, D_MODEL), F32),
            pltpu.SemaphoreType.DMA((1,)),
            pltpu.SemaphoreType.DMA(()),
            pltpu.SemaphoreType.DMA((MOE_UP_RING,)),
            pltpu.SemaphoreType.DMA((MOE_NB,)),
        ],
    )
    return pl.pallas_call(
        _moe_kernel,
        grid_spec=grid_spec,
        out_shape=jax.ShapeDtypeStruct((TOP_K * S, D_MODEL), F32),
        compiler_params=_cparams(("arbitrary", "arbitrary")),
        name="moe_experts",
    )(item_e, item_start, item_n, order, xn, w1, w3, w2)


def _moe_items(eid, cap):
    A = eid.size
    flat_e = eid.reshape(A)
    order = jnp.argsort(flat_e, stable=True).astype(jnp.int32)
    experts = jnp.arange(N_EXPERTS + 1, dtype=jnp.int32)
    starts = jnp.sum((flat_e[None, :] < experts[:, None]).astype(jnp.int32), axis=1)
    counts = starts[1:] - starts[:-1]
    per_e = (counts + cap - 1) // cap
    item_end = jnp.cumsum(per_e)
    total = item_end[-1]
    n_items = N_EXPERTS + A // cap
    idx = jnp.arange(n_items, dtype=jnp.int32)
    clamped = jnp.minimum(idx, total - 1)
    first_after = jnp.sum((item_end[None, :] <= clamped[:, None]).astype(jnp.int32), axis=1)
    e = jnp.minimum(first_after, N_EXPERTS - 1)
    local = clamped - (item_end[e] - per_e[e])
    used = idx < total
    item_start = jnp.where(used, starts[e] + local * cap, 0).astype(jnp.int32)
    item_n = jnp.where(used, jnp.clip(counts[e] - local * cap, 0, cap), 0).astype(jnp.int32)
    return e, item_start, item_n, order


def _combine_kernel(x1_ref, y0_ref, y1_ref, g_ref, o_ref):
    g = g_ref[...]
    o_ref[...] = x1_ref[...] + (g[:, 0:1] * y0_ref[...] + g[:, 1:2] * y1_ref[...])


def _combine(x1, y, gates, tm=512):
    S = x1.shape[0]
    nb = S // tm
    return pl.pallas_call(
        _combine_kernel,
        grid=(nb,),
        in_specs=[
            pl.BlockSpec((tm, D_MODEL), lambda i: (i, 0)),
            pl.BlockSpec((tm, D_MODEL), lambda i: (i, 0)),
            pl.BlockSpec((tm, D_MODEL), lambda i: (nb + i, 0)),
            pl.BlockSpec((tm, LANES), lambda i: (i, 0)),
        ],
        out_specs=pl.BlockSpec((tm, D_MODEL), lambda i: (i, 0)),
        out_shape=jax.ShapeDtypeStruct((S, D_MODEL), F32),
        compiler_params=_cparams(("parallel",)),
        name="moe_combine",
    )(x1, y, y, gates)


def _rot_half_cols(w, half):
    return jnp.concatenate([-w[..., half:], w[..., :half]], axis=-1)


def _rope_tables(S):
    pos = np.arange(S, dtype=np.float32)[:, None]

    def cs(half):
        inv = (np.float32(ROPE_THETA) ** (-np.arange(half, dtype=np.float32) / np.float32(half))).astype(np.float32)
        ang = (pos * inv[None, :]).astype(np.float64)
        return np.cos(ang).astype(np.float32), np.sin(ang).astype(np.float32)

    c32, s32 = cs(MLA_ROPE_DIM // 2)
    t64 = np.concatenate([c32, c32, s32, s32], axis=-1)
    c64, s64 = cs(DIL_HEAD_DIM // 2)
    cos128 = np.concatenate([c64, c64], axis=-1)
    sin128 = np.concatenate([-s64, s64], axis=-1)
    return jnp.asarray(t64), jnp.asarray(cos128), jnp.asarray(sin128)


def _residue_perm(dil):
    n = PERM_TILE // dil
    rows = np.arange(PERM_TILE)
    src = (rows % n) * dil + rows // n
    return jnp.asarray(src[:, None] == np.arange(PERM_TILE)[None, :], dtype=BF16)


def kernel(x, norm1_g, w_in, g_cq, g_ckv, w_uq, w_ukv, mla_q_norm_g, mla_k_norm_g, dil_q_norm_g,
           dil_k_norm_g, w_out, norm2_g, w_group, b_group, w_expert, b_expert, w1, w3, w2):
    B, S, D = x.shape
    assert B == 1 and D == D_MODEL and norm1_g.shape[0] == 1
    x2 = x.reshape(S, D)
    half_r = MLA_ROPE_DIM // 2
    o1, o2 = Q_LORA_RANK, Q_LORA_RANK + KV_LORA_RANK
    o3 = o2 + MLA_ROPE_DIM

    w_z = _regroup_w_in(w_in)

    wq = w_uq[0].reshape(Q_LORA_RANK, N_MLA_HEADS, MLA_QK_DIM)
    wq_pe = wq[..., MLA_NOPE_DIM:]
    wq_ext = jnp.concatenate([wq, _rot_half_cols(wq_pe, half_r)], axis=-1)
    wq_ext = wq_ext.reshape(Q_LORA_RANK, N_MLA_HEADS * MLA_PAD).astype(BF16)
    wkv = w_ukv[0].astype(BF16)

    def ext_gain(g):
        pe = g[MLA_NOPE_DIM:]
        return jnp.concatenate([g, pe[half_r:], pe[:half_r]])[None, :]

    gq_ext = ext_gain(mla_q_norm_g[0])
    gk_ext = ext_gain(mla_k_norm_g[0])
    t64, cos128, sin128 = _rope_tables(S)
    p4, p16 = _residue_perm(4), _residue_perm(16)

    z = _inproj(x2, norm1_g, w_z)
    q, k, vt, dq, dk, dq4, dk4, dv4, dq16, dk16, dv16 = _prep(
        z, g_cq, g_ckv, wq_ext, wkv, gq_ext, gk_ext[:, :LANES], gk_ext[:, LANES:],
        dil_q_norm_g, dil_k_norm_g, t64, cos128, sin128, p4, p16)
    mla_o = _mla_attn(q, k, vt)

    (w_1, d_1), (w_4, d_4), (w_16, d_16) = DIL_PATTERNS
    dil1 = _dil_pattern(dq[None], dk[None], z[None], Z_DV, w_1, d_1, hb=2, gb=4)
    dil4 = _dil_pattern(dq4, dk4, dv4, 0, w_4, d_4, hb=8, gb=1)
    dil16 = _dil_pattern(dq16, dk16, dv16, 0, w_16, d_16, hb=8, gb=1)

    pad = LANES - N_EXPERT_GROUPS - N_EXPERTS
    w_router = jnp.concatenate([w_group[0], w_expert[0], jnp.zeros((D, pad), F32)], axis=-1)
    b_router = jnp.concatenate([b_group[0], b_expert[0], jnp.zeros((pad,), F32)])[None, :]
    wr_hi = w_router.astype(BF16)
    wr_lo = (w_router - wr_hi.astype(F32)).astype(BF16)
    x1, xn, route_i, route_f = _outproj_router(
        mla_o, dil1, dil4, dil16, p4.T, p16.T, x2, w_out[0].astype(BF16), norm2_g,
        jnp.concatenate([wr_hi, wr_lo], axis=-1), wr_hi, b_router)

    item_e, item_start, item_n, order = _moe_items(route_i[:, :TOP_K], MOE_CAP)
    y = _moe_experts(xn, w1[0], w3[0], w2[0], item_e, item_start, item_n, order)
    out = _combine(x1, y, route_f)
    return out.reshape(B, S, D)
```

```python
import functools
import math

import jax
import jax.numpy as jnp
import numpy as np
from jax import lax
from jax.experimental import pallas as pl
from jax.experimental.pallas import tpu as pltpu

D_MODEL = 2048
N_MLA_HEADS = 8
MLA_NOPE_DIM = 128
MLA_ROPE_DIM = 64
MLA_QK_DIM = MLA_NOPE_DIM + MLA_ROPE_DIM
MLA_V_DIM = 128
Q_LORA_RANK = 512
KV_LORA_RANK = 512
N_DIL_HEADS = 8
DIL_HEAD_DIM = 128
DIL_PATTERNS = ((128, 1), (512, 4), (2048, 16))
ROPE_THETA = 10000.0
NORM_EPS = 1e-6
NEG_INF = -1e30
N_EXPERT_GROUPS = 8
EXPERTS_PER_GROUP = 8
N_EXPERTS = N_EXPERT_GROUPS * EXPERTS_PER_GROUP
TOP_K = 2
EXPERT_FF = 1408

LANES = 128
BF16_ROWS = 16
MXU_DIM = 256
HD = N_DIL_HEADS * DIL_HEAD_DIM
MLA_PAD = MXU_DIM
Z_DQ, Z_DK, Z_DV = 0, HD, 2 * HD
Z_CQ = 3 * HD
Z_CKV = Z_CQ + Q_LORA_RANK
Z_KR = Z_CKV + KV_LORA_RANK
Z_WIDTH = Z_KR + 2 * MLA_ROPE_DIM

ATT_VC = 512
ATT_TK = 512
ATT_TQ = 1024
ATT_UNROLL = 8
VT_ROWS = MLA_V_DIM + BF16_ROWS
PERM_TILE = 256
LOG2E = math.log2(math.e)

VMEM_LIMIT = 56 * 1024 * 1024

F32 = jnp.float32
BF16 = jnp.bfloat16


def _cparams(sem, vmem=VMEM_LIMIT):
    return pltpu.CompilerParams(dimension_semantics=sem, vmem_limit_bytes=vmem)


def _dot(a, b):
    return jnp.dot(a, b, preferred_element_type=F32)


def _dot_nt(a, b):
    return lax.dot_general(a, b, (((1,), (1,)), ((), ())), preferred_element_type=F32)


REGROUP_ROWS = MLA_ROPE_DIM


def _regroup_kernel(w_ref, o_ref):
    x = w_ref[...]
    half = MLA_ROPE_DIM // 2
    rot = jnp.concatenate([-x[half:], x[:half]], axis=0)
    is_rot = pl.program_id(0) == pl.num_programs(0) - 1
    o_ref[...] = jnp.where(is_rot, rot, x).astype(BF16)


def _regroup_w_in(w_in):
    w_t = jnp.swapaxes(w_in[0], 0, 1)
    d = w_t.shape[1]
    rb = REGROUP_ROWS
    n_tail = (Z_CQ - Z_DQ) // rb
    n_lat = (Z_KR - Z_CQ) // rb
    first_tail = n_lat + 1

    def src(i):
        return jnp.where(i < n_tail, i + first_tail, jnp.where(i < n_tail + n_lat, i - n_tail, n_lat)), 0

    return pl.pallas_call(
        _regroup_kernel,
        grid=(Z_WIDTH // rb,),
        in_specs=[pl.BlockSpec((rb, d), src)],
        out_specs=pl.BlockSpec((rb, d), lambda i: (i, 0)),
        out_shape=jax.ShapeDtypeStruct((Z_WIDTH, d), BF16),
        compiler_params=_cparams(("parallel",)),
        name="regroup_w_in",
    )(w_t)


def _inproj_kernel(x_ref, g_ref, w_ref, z_ref):
    x = x_ref[...]
    r = lax.rsqrt(jnp.mean(x * x, axis=-1, keepdims=True) + NORM_EPS)
    h = (x * r * g_ref[...]).astype(BF16)
    z_ref[...] = _dot_nt(h, w_ref[...]).astype(BF16)


def _inproj(x2, g1, w_z, tm=256):
    S = x2.shape[0]
    return pl.pallas_call(
        _inproj_kernel,
        grid=(S // tm,),
        in_specs=[
            pl.BlockSpec((tm, D_MODEL), lambda i: (i, 0)),
            pl.BlockSpec((1, D_MODEL), lambda i: (0, 0)),
            pl.BlockSpec((Z_WIDTH, D_MODEL), lambda i: (0, 0)),
        ],
        out_specs=pl.BlockSpec((tm, Z_WIDTH), lambda i: (i, 0)),
        out_shape=jax.ShapeDtypeStruct((S, Z_WIDTH), BF16),
        compiler_params=_cparams(("parallel",)),
        name="inproj",
    )(x2, g1, w_z)


def _prep_kernel(zdq_ref, zdk_ref, zdv_ref, zcq_ref, zckv_ref, zkr_ref,
                 gcq_ref, gckv_ref, wq_ref, wkv_ref, gq_ref, gkn_ref, gkp_ref,
                 gdq_ref, gdk_ref, t64_ref, cos_ref, sin_ref, p4_ref, p16_ref,
                 q_ref, k_ref, vt_ref, dq_ref, dk_ref,
                 dq4_ref, dk4_ref, dv4_ref, dq16_ref, dk16_ref, dv16_ref):
    tm = zcq_ref.shape[0]
    lane = lax.broadcasted_iota(jnp.int32, (1, LANES), 1)
    first_half = lane < MLA_ROPE_DIM

    def rms_rows(c, g):
        c = c.astype(F32)
        r = lax.rsqrt(jnp.mean(c * c, axis=-1, keepdims=True) + NORM_EPS)
        return (c * r * g).astype(BF16)

    cq = rms_rows(zcq_ref[...], gcq_ref[...])
    ckv = rms_rows(zckv_ref[...], gckv_ref[...])
    qe = _dot(cq, wq_ref[...])
    kv = _dot(ckv, wkv_ref[...])
    kr = zkr_ref[...].astype(F32)
    kr_ss = jnp.sum(jnp.where(first_half, kr * kr, 0.0), axis=-1, keepdims=True)
    t64 = t64_ref[...]
    q_scale = MLA_QK_DIM ** -0.5 * LOG2E

    def rope64(ext):
        t = ext * t64
        return jnp.where(first_half, t + pltpu.roll(t, MLA_ROPE_DIM, 1), 0.0)

    ones_rows = (lax.broadcasted_iota(jnp.int32, (BF16_ROWS, tm), 0) == 0).astype(BF16)
    eye = (lax.broadcasted_iota(jnp.int32, (MLA_V_DIM, MLA_V_DIM), 0)
           == lax.broadcasted_iota(jnp.int32, (MLA_V_DIM, MLA_V_DIM), 1)).astype(BF16)
    for h in range(N_MLA_HEADS):
        base = h * MLA_PAD
        qn = qe[:, base:base + LANES]
        qp = qe[:, base + LANES:base + 2 * LANES]
        ss = (jnp.sum(qn * qn, axis=-1, keepdims=True)
              + jnp.sum(jnp.where(first_half, qp * qp, 0.0), axis=-1, keepdims=True))
        r = lax.rsqrt(ss * (1.0 / MLA_QK_DIM) + NORM_EPS) * q_scale
        q_ref[:, base:base + LANES] = (qn * r * gq_ref[:, :LANES]).astype(BF16)
        q_ref[:, base + LANES:base + MLA_PAD] = rope64(qp * r * gq_ref[:, LANES:]).astype(BF16)

        kn = kv[:, base:base + LANES]
        ss = jnp.sum(kn * kn, axis=-1, keepdims=True) + kr_ss
        r = lax.rsqrt(ss * (1.0 / MLA_QK_DIM) + NORM_EPS)
        k_ref[:, base:base + LANES] = (kn * r * gkn_ref[...]).astype(BF16)
        k_ref[:, base + LANES:base + MLA_PAD] = rope64(kr * r * gkp_ref[...]).astype(BF16)
        v_h = kv[:, base + LANES:base + 2 * LANES].astype(BF16)
        vt_ref[h, :MLA_V_DIM, :] = _dot_nt(eye, v_h).astype(BF16)
        vt_ref[h, MLA_V_DIM:, :] = ones_rows

    cos = cos_ref[...]
    sin = sin_ref[...]
    d_scale = DIL_HEAD_DIM ** -0.5

    def dil_head(x, g, scale):
        x = x.astype(F32)
        r = lax.rsqrt(jnp.mean(x * x, axis=-1, keepdims=True) + NORM_EPS)
        y = x * r * g
        return ((y * cos + pltpu.roll(y, DIL_HEAD_DIM // 2, 1) * sin) * scale).astype(BF16)

    for h in range(N_DIL_HEADS):
        sl = slice(h * DIL_HEAD_DIM, (h + 1) * DIL_HEAD_DIM)
        dq_ref[:, sl] = dil_head(zdq_ref[:, sl], gdq_ref[...], d_scale)
        dk_ref[:, sl] = dil_head(zdk_ref[:, sl], gdk_ref[...], 1.0)

    for src, d4, d16 in ((dq_ref, dq4_ref, dq16_ref), (dk_ref, dk4_ref, dk16_ref), (zdv_ref, dv4_ref, dv16_ref)):
        for sub in range(tm // PERM_TILE):
            xs = src[sub * PERM_TILE:(sub + 1) * PERM_TILE, :]
            for dil, p_ref, dst in ((4, p4_ref, d4), (16, p16_ref, d16)):
                n = PERM_TILE // dil
                xp = _dot(p_ref[...], xs).astype(BF16)
                for r in range(dil):
                    dst[r, sub * n:(sub + 1) * n, :] = xp[r * n:(r + 1) * n, :]


def _prep(z, gcq, gckv, wq_ext, wkv, gq_ext, gk_nope, gk_pe, gdq, gdk, t64, cos128, sin128, p4, p16):
    S = z.shape[0]
    tm = ATT_VC
    row = lambda w, j: pl.BlockSpec((tm, w), lambda i, j=j: (i, j))
    full = lambda a: pl.BlockSpec(a.shape, lambda i: (0, 0))
    res = lambda dil: pl.BlockSpec((dil, tm // dil, HD), lambda i: (0, i, 0))
    qk_w = N_MLA_HEADS * MLA_PAD
    res_shape = lambda dil: jax.ShapeDtypeStruct((dil, S // dil, HD), BF16)
    return pl.pallas_call(
        _prep_kernel,
        grid=(S // tm,),
        in_specs=[
            row(HD, Z_DQ // HD), row(HD, Z_DK // HD), row(HD, Z_DV // HD),
            row(Q_LORA_RANK, Z_CQ // Q_LORA_RANK), row(KV_LORA_RANK, Z_CKV // KV_LORA_RANK),
            row(LANES, Z_KR // LANES),
            full(gcq), full(gckv), full(wq_ext), full(wkv), full(gq_ext), full(gk_nope), full(gk_pe),
            full(gdq), full(gdk),
            row(LANES, 0), row(LANES, 0), row(LANES, 0), full(p4), full(p16),
        ],
        out_specs=[row(qk_w, 0), row(qk_w, 0),
                   pl.BlockSpec((N_MLA_HEADS, None, VT_ROWS, tm), lambda i: (0, i, 0, 0)),
                   row(HD, 0), row(HD, 0),
                   res(4), res(4), res(4), res(16), res(16), res(16)],
        out_shape=[
            jax.ShapeDtypeStruct((S, qk_w), BF16), jax.ShapeDtypeStruct((S, qk_w), BF16),
            jax.ShapeDtypeStruct((N_MLA_HEADS, S // tm, VT_ROWS, tm), BF16),
            jax.ShapeDtypeStruct((S, HD), BF16), jax.ShapeDtypeStruct((S, HD), BF16),
            res_shape(4), res_shape(4), res_shape(4), res_shape(16), res_shape(16), res_shape(16),
        ],
        compiler_params=_cparams(("parallel",)),
        name="qkv_prep",
    )(z, z, z, z, z, z, gcq, gckv, wq_ext, wkv, gq_ext, gk_nope, gk_pe, gdq, gdk, t64, cos128, sin128, p4, p16)


def _mla_attn_kernel(q_ref, k_ref, vt_ref, o_ref, m_ref, acc_ref, s_ref):
    per_chunk = ATT_TK // ATT_VC
    n_chunks = vt_ref.shape[0] // per_chunk
    q = q_ref[...]

    def scores(c):
        off = pl.multiple_of(c * ATT_TK, ATT_TK)
        return _dot_nt(k_ref[pl.ds(off, ATT_TK), :], q)

    m_ref[...] = jnp.full(m_ref.shape, -jnp.inf, F32)
    acc_ref[...] = jnp.zeros(acc_ref.shape, F32)

    def fold(c, slot):
        s = s_ref[slot]
        m_old = m_ref[...]
        m_new = jnp.maximum(m_old, jnp.max(s, axis=0, keepdims=True))
        alpha = jnp.exp2(m_old - m_new)
        p = jnp.exp2(s - m_new).astype(BF16)
        pv = _dot(vt_ref[c * per_chunk], p[:ATT_VC])
        for u in range(1, per_chunk):
            pv += _dot(vt_ref[c * per_chunk + u], p[u * ATT_VC:(u + 1) * ATT_VC])
        acc_ref[...] = alpha * acc_ref[...] + pv
        m_ref[...] = m_new

    s_ref[0] = scores(0)

    def body(t, carry):
        for u in range(ATT_UNROLL):
            c = ATT_UNROLL * t + u
            s_ref[(u + 1) % 2] = scores(jnp.minimum(c + 1, n_chunks - 1))
            fold(c, u % 2)
        return carry

    lax.fori_loop(0, n_chunks // ATT_UNROLL, body, 0)
    acc = acc_ref[...]
    o_t = acc[:MLA_V_DIM, :] / acc[MLA_V_DIM:MLA_V_DIM + 1, :]
    o_ref[...] = o_t.T.astype(o_ref.dtype)


def _mla_attn(q, k, vt):
    S = q.shape[0]
    n_chunks = vt.shape[1]
    return pl.pallas_call(
        _mla_attn_kernel,
        grid=(N_MLA_HEADS, S // ATT_TQ),
        in_specs=[
            pl.BlockSpec((ATT_TQ, MLA_PAD), lambda h, i: (i, h)),
            pl.BlockSpec((S, MLA_PAD), lambda h, i: (0, h)),
            pl.BlockSpec((None, n_chunks, VT_ROWS, ATT_VC), lambda h, i: (h, 0, 0, 0)),
        ],
        out_specs=pl.BlockSpec((ATT_TQ, MLA_V_DIM), lambda h, i: (i, h)),
        out_shape=jax.ShapeDtypeStruct((S, N_MLA_HEADS * MLA_V_DIM), BF16),
        scratch_shapes=[pltpu.VMEM((1, ATT_TQ), F32), pltpu.VMEM((VT_ROWS, ATT_TQ), F32),
                        pltpu.VMEM((2, ATT_TK, ATT_TQ), F32)],
        compiler_params=_cparams(("parallel", "parallel")),
        name="mla_attn",
    )(q, k, vt)


DIL_QB = 128


def _dil_kernel(q_ref, k_ref, v_ref, o_ref, lse_ref, *, half_w, hb, gb):
    L = q_ref.shape[0]
    kw = DIL_QB + 2 * half_w
    g_id = pl.program_id(1)
    lane = lax.broadcasted_iota(jnp.int32, (DIL_QB, LANES), 1)

    @pl.when(g_id == 0)
    def _():
        lse_ref[...] = jnp.zeros(lse_ref.shape, F32)

    def body(step, carry):
        qs_l, ks_l, qb, kb, vb = [], [], [], [], []
        for j in range(gb):
            qs = pl.multiple_of((step * gb + j) * DIL_QB, DIL_QB)
            ks = pl.multiple_of(jnp.clip(qs - half_w, 0, L - kw), half_w)
            qs_l.append(qs)
            ks_l.append(ks)
            for h in range(hb):
                sl = slice(h * DIL_HEAD_DIM, (h + 1) * DIL_HEAD_DIM)
                qb.append(q_ref[pl.ds(qs, DIL_QB), sl])
                kb.append(k_ref[pl.ds(ks, kw), sl])
                vb.append(v_ref[pl.ds(ks, kw), sl])
        q = jnp.stack(qb)
        k = jnp.stack(kb)
        v = jnp.stack(vb)
        s = jnp.einsum("gqd,gkd->gqk", q, k, preferred_element_type=F32)
        rel = (lax.broadcasted_iota(jnp.int32, (DIL_QB, kw), 0)
               - lax.broadcasted_iota(jnp.int32, (DIL_QB, kw), 1))
        bias = []
        for j in range(gb):
            mask = jnp.abs(rel + (qs_l[j] - ks_l[j])) <= half_w
            bias += [jnp.where(mask, 0.0, NEG_INF)] * hb
        s = s + jnp.stack(bias)
        m = jnp.max(s, axis=-1, keepdims=True)
        p = jnp.exp(s - m)
        den = jnp.sum(p, axis=-1, keepdims=True)
        o = jnp.einsum("gqk,gkd->gqd", p.astype(BF16), v, preferred_element_type=F32) / den
        lse = m + jnp.log(den)
        for j in range(gb):
            tile = lse_ref[pl.ds(qs_l[j], DIL_QB), :]
            for h in range(hb):
                sl = slice(h * DIL_HEAD_DIM, (h + 1) * DIL_HEAD_DIM)
                o_ref[pl.ds(qs_l[j], DIL_QB), sl] = o[j * hb + h].astype(o_ref.dtype)
                tile = jnp.where(lane == g_id * hb + h, lse[j * hb + h], tile)
            lse_ref[pl.ds(qs_l[j], DIL_QB), :] = tile
        return carry

    lax.fori_loop(0, L // (DIL_QB * gb), body, 0)


def _dil_pattern(dq, dk, dv, v_col0, window, dil, hb, gb):
    L = dq.shape[1]
    half_w = window // (2 * dil)
    w = hb * DIL_HEAD_DIM
    spec = lambda c0: pl.BlockSpec((None, L, w), lambda r, g, c0=c0: (r, 0, c0 + g))
    return pl.pallas_call(
        functools.partial(_dil_kernel, half_w=half_w, hb=hb, gb=gb),
        grid=(dil, N_DIL_HEADS // hb),
        in_specs=[spec(0), spec(0), spec(v_col0 // w)],
        out_specs=[spec(0), pl.BlockSpec((None, L, LANES), lambda r, g: (r, 0, 0))],
        out_shape=[jax.ShapeDtypeStruct((dil, L, HD), BF16), jax.ShapeDtypeStruct((dil, L, LANES), F32)],
        compiler_params=_cparams(("parallel", "arbitrary")),
        name=f"dil_attn_d{dil}",
    )(dq, dk, dv)


def _outproj_kernel(a_ref, o1_ref, l1_ref, o4_ref, l4_ref, o16_ref, l16_ref, p4t_ref, p16t_ref,
                    x_ref, wa_ref, wb_ref, g_ref, wr_ref, wrh_ref, br_ref,
                    x1_ref, xn_ref, ri_ref, rf_ref):
    tm = x_ref.shape[0]

    def to_token_order(o_ref, l_ref, pt_ref):
        pt = pt_ref[...]
        o = _dot(pt, o_ref[...].reshape(tm, HD))
        lse = l_ref[...].reshape(tm, LANES)
        hi = lse.astype(BF16)
        rem = lse - hi.astype(F32)
        mid = rem.astype(BF16)
        lo = (rem - mid.astype(F32)).astype(BF16)
        return o, _dot(pt, hi) + _dot(pt, mid) + _dot(pt, lo)

    o1 = o1_ref[...].astype(F32)
    l1 = l1_ref[...]
    o4, l4 = to_token_order(o4_ref, l4_ref, p4t_ref)
    o16, l16 = to_token_order(o16_ref, l16_ref, p16t_ref)
    big = jnp.maximum(jnp.maximum(l1, l4), l16)
    e1 = jnp.exp(l1 - big)
    e4 = jnp.exp(l4 - big)
    e16 = jnp.exp(l16 - big)
    inv = 1.0 / (e1 + e4 + e16)
    w1, w4, w16 = e1 * inv, e4 * inv, e16 * inv
    slabs = []
    for h in range(N_DIL_HEADS):
        sl = slice(h * DIL_HEAD_DIM, (h + 1) * DIL_HEAD_DIM)
        slabs.append(w1[:, h:h + 1] * o1[:, sl] + w4[:, h:h + 1] * o4[:, sl] + w16[:, h:h + 1] * o16[:, sl])
    dil_o = jnp.concatenate(slabs, axis=-1).astype(BF16)

    x1 = x_ref[...] + _dot(a_ref[...], wa_ref[...]) + _dot(dil_o, wb_ref[...])
    x1_ref[...] = x1
    r = lax.rsqrt(jnp.mean(x1 * x1, axis=-1, keepdims=True) + NORM_EPS)
    xn = x1 * r * g_ref[...]
    xn_ref[...] = xn
    xh = xn.astype(BF16)
    xl = (xn - xh.astype(F32)).astype(BF16)
    two = _dot(xh, wr_ref[...])
    logits = two[:, :LANES] + two[:, LANES:] + _dot(xl, wrh_ref[...]) + br_ref[...]
    lane = lax.broadcasted_iota(jnp.int32, logits.shape, 1)
    ninf = -jnp.inf

    def first_argmax(vals, vmax):
        return jnp.min(jnp.where(vals == vmax, lane, LANES), axis=-1, keepdims=True)

    coarse = jnp.where(lane < N_EXPERT_GROUPS, logits, ninf)
    cmax = jnp.max(coarse, axis=-1, keepdims=True)
    g = first_argmax(coarse, cmax)
    p_g = 1.0 / jnp.sum(jnp.exp(coarse - cmax), axis=-1, keepdims=True)
    lo_lane = N_EXPERT_GROUPS + g * EXPERTS_PER_GROUP
    fine = jnp.where((lane >= lo_lane) & (lane < lo_lane + EXPERTS_PER_GROUP), logits, ninf)
    v1 = jnp.max(fine, axis=-1, keepdims=True)
    j1 = first_argmax(fine, v1)
    fine2 = jnp.where(lane == j1, ninf, fine)
    v2 = jnp.max(fine2, axis=-1, keepdims=True)
    j2 = first_argmax(fine2, v2)
    e2 = jnp.exp(v2 - v1)
    g1 = 1.0 / (1.0 + e2)
    g2 = e2 / (1.0 + e2)
    ri_ref[...] = jnp.where(lane == 0, j1 - N_EXPERT_GROUPS, jnp.where(lane == 1, j2 - N_EXPERT_GROUPS, 0))
    rf_ref[...] = jnp.where(lane == 0, p_g * g1, jnp.where(lane == 1, p_g * g2, 0.0))


def _outproj_router(mla_o, dil1, dil4, dil16, p4t, p16t, x2, w_out_bf, g2, wr_two, wr_hi, b_router):
    S = x2.shape[0]
    tm = PERM_TILE
    half = N_MLA_HEADS * MLA_V_DIM
    row = lambda w: pl.BlockSpec((tm, w), lambda i: (i, 0))
    res = lambda dil, w: pl.BlockSpec((dil, tm // dil, w), lambda i: (0, i, 0))
    const = lambda a: pl.BlockSpec(a.shape, lambda i: (0, 0))
    return pl.pallas_call(
        _outproj_kernel,
        grid=(S // tm,),
        in_specs=[
            row(half),
            pl.BlockSpec((None, tm, HD), lambda i: (0, i, 0)), pl.BlockSpec((None, tm, LANES), lambda i: (0, i, 0)),
            res(4, HD), res(4, LANES), res(16, HD), res(16, LANES), const(p4t), const(p16t),
            row(D_MODEL),
            pl.BlockSpec((half, D_MODEL), lambda i: (0, 0)),
            pl.BlockSpec((HD, D_MODEL), lambda i: (1, 0)),
            const(g2), const(wr_two), const(wr_hi), const(b_router),
        ],
        out_specs=[row(D_MODEL), row(D_MODEL), row(LANES), row(LANES)],
        out_shape=[
            jax.ShapeDtypeStruct((S, D_MODEL), F32), jax.ShapeDtypeStruct((S, D_MODEL), F32),
            jax.ShapeDtypeStruct((S, LANES), jnp.int32), jax.ShapeDtypeStruct((S, LANES), F32),
        ],
        compiler_params=_cparams(("parallel",)),
        name="outproj_router",
    )(mla_o, dil1[0], dil1[1], dil4[0], dil4[1], dil16[0], dil16[1], p4t, p16t,
      x2, w_out_bf, w_out_bf, g2, wr_two, wr_hi, b_router)


MOE_CAP = 512
MOE_RH = 128
MOE_KC = 512
MOE_NK = D_MODEL // MOE_KC
MOE_FB = 768
MOE_NB = 2
MOE_FB_LAST = EXPERT_FF - (MOE_NB - 1) * MOE_FB
MOE_STEPS = MOE_NK + MOE_NB
MOE_LOOKAHEAD = 3
MOE_UP_RING = MOE_LOOKAHEAD + 1
MOE_UP_PARTS = ((0, 1024), (1024, 2048), (2048, 2 * EXPERT_FF))
MOE_DOWN_PARTS = ((0, 1024), (1024, D_MODEL))
DMA_UNROLL = 8
assert 0 < MOE_FB_LAST <= MOE_FB and MOE_FB % LANES == 0 and MOE_FB_LAST % LANES == 0


def _moe_kernel(ie_ref, is_ref, in_ref, ord_ref,
                xn_hbm, w1_hbm, w3_hbm, w2_hbm,
                y_hbm, xf_ref, xb_ref, ab_ref, h_ref, acc_ref, w1r, w3r, w2r,
                sem_in, sem_out, sem_up, sem_dn):
    i = pl.program_id(0)
    j = pl.program_id(1)
    n_items = pl.num_programs(0)
    n_tokens = xn_hbm.shape[0]
    n = in_ref[i]
    slot = 0

    def up_ring(item, kc):
        return lax.rem(item * MOE_NK + kc, MOE_UP_RING)

    def block_copies(item, step):
        e = ie_ref[item]
        if step < MOE_NK:
            ring = up_ring(item, step)
            rows = pl.ds(step * MOE_KC, MOE_KC)
            return (pltpu.make_async_copy(w1_hbm.at[e, rows, :], w1r.at[ring], sem_up.at[ring]),
                    pltpu.make_async_copy(w3_hbm.at[e, rows, :], w3r.at[ring], sem_up.at[ring]))
        blk = step - MOE_NK
        size = MOE_FB if blk < MOE_NB - 1 else MOE_FB_LAST
        return (pltpu.make_async_copy(w2_hbm.at[e, pl.ds(blk * MOE_FB, size), :],
                                      w2r.at[blk, pl.ds(0, size), :], sem_dn.at[blk]),)

    def start_block(item, step):
        for c in block_copies(item, step):
            c.start()

    for js in range(MOE_STEPS):
        @pl.when((n > 0) & (j == js))
        def _():
            if js == 0:
                @pl.when(i == 0)
                def _():
                    for s0 in range(MOE_LOOKAHEAD):
                        start_block(0, s0)
            for c in block_copies(i, js):
                c.wait()
            tgt = js + MOE_LOOKAHEAD
            if tgt < MOE_STEPS:
                start_block(i, tgt)
            else:
                nxt = jnp.minimum(i + 1, n_items - 1)

                @pl.when((i + 1 < n_items) & (in_ref[nxt] > 0))
                def _():
                    start_block(nxt, tgt - MOE_STEPS)

    def row_in(item, sl, base, u, width):
        tok = lax.shift_right_logical(ord_ref[is_ref[item] + base + u], 1)
        dst = xf_ref.at[sl, pl.ds(base, width), :].at[pl.ds(u, 1), :]
        return pltpu.make_async_copy(xn_hbm.at[pl.ds(tok, 1), :], dst, sem_in.at[sl])

    def row_out(item, base, u, width):
        a = ord_ref[is_ref[item] + base + u]
        dst = (a & 1) * n_tokens + lax.shift_right_logical(a, 1)
        src = acc_ref.at[pl.ds(base, width), :].at[pl.ds(u, 1), :]
        return pltpu.make_async_copy(src, y_hbm.at[pl.ds(dst, 1), :], sem_out)

    def for_rows(count, fn):
        groups = lax.div(count, DMA_UNROLL)

        def group(g, c):
            base = pl.multiple_of(g * DMA_UNROLL, DMA_UNROLL)
            for u in range(DMA_UNROLL):
                fn(base, u, DMA_UNROLL)
            return c
        lax.fori_loop(0, groups, group, 0)

        def single(r, c):
            fn(r, 0, 1)
            return c
        lax.fori_loop(groups * DMA_UNROLL, count, single, 0)

    def wait_rows(count, rows_desc):
        groups = lax.div(count, DMA_UNROLL)

        def group(g, c):
            rows_desc(pl.multiple_of(g * DMA_UNROLL, DMA_UNROLL), DMA_UNROLL).wait()
            return c
        lax.fori_loop(0, groups, group, 0)

        def single(r, c):
            rows_desc(r, 1).wait()
            return c
        lax.fori_loop(groups * DMA_UNROLL, count, single, 0)

    def rows_in(sl, r0, size):
        return pltpu.make_async_copy(xn_hbm.at[pl.ds(0, size), :], xf_ref.at[sl, pl.ds(r0, size), :], sem_in.at[sl])

    def rows_out(r0, size):
        return pltpu.make_async_copy(acc_ref.at[pl.ds(r0, size), :], y_hbm.at[pl.ds(0, size), :], sem_out)

    def for_row_tiles(fn):
        tiles = lax.div(n + (MOE_RH - 1), MOE_RH)
        for k in range(1, MOE_CAP // MOE_RH + 1):
            @pl.when(tiles == k)
            def _():
                fn(0, k * MOE_RH)

    @pl.when((i == 0) & (j == 0))
    def _():
        xf_ref[...] = jnp.zeros(xf_ref.shape, F32)
        for_rows(n, lambda *row: row_in(0, 0, *row).start())

    @pl.when((n > 0) & (j == 0))
    def _():
        wait_rows(n, functools.partial(rows_in, slot))

        @pl.when(i > 0)
        def _():
            wait_rows(in_ref[i - 1], rows_out)

        for t in range(MOE_CAP // MOE_RH):
            @pl.when(t * MOE_RH < n)
            def _():
                sl = pl.ds(t * MOE_RH, MOE_RH)
                for kc in range(MOE_NK):
                    xb_ref[kc, sl, :] = xf_ref[slot, sl, kc * MOE_KC:(kc + 1) * MOE_KC].astype(BF16)
                ab_ref[sl, :] = jnp.zeros((MOE_RH, 2 * EXPERT_FF), F32)

    @pl.when((n > 0) & (j == 1) & (i + 1 < n_items))
    def _():
        nxt = jnp.minimum(i + 1, n_items - 1)
        for_rows(in_ref[nxt], lambda *row: row_in(nxt, slot, *row).start())

    @pl.when((n > 0) & (j < MOE_NK))
    def _():
        kc = jnp.minimum(j, MOE_NK - 1)
        ring = up_ring(i, kc)

        def cat_cols(c0, c1):
            pieces = []
            if c0 < EXPERT_FF:
                pieces.append(w1r[ring, :, c0:min(c1, EXPERT_FF)].astype(BF16))
            if c1 > EXPERT_FF:
                pieces.append(w3r[ring, :, max(c0, EXPERT_FF) - EXPERT_FF:c1 - EXPERT_FF].astype(BF16))
            return pieces[0] if len(pieces) == 1 else jnp.concatenate(pieces, axis=-1)

        def up(off, size):
            xs = xb_ref[kc, pl.ds(off, size), :]
            for c0, c1 in MOE_UP_PARTS:
                ab_ref[pl.ds(off, size), c0:c1] += _dot(xs, cat_cols(c0, c1))
        for_row_tiles(up)

    @pl.when((n > 0) & (j == MOE_NK))
    def _():
        def act(off, size):
            a = ab_ref[pl.ds(off, size), :EXPERT_FF]
            b = ab_ref[pl.ds(off, size), EXPERT_FF:]
            half_a = 0.5 * a
            h_ref[pl.ds(off, size), :] = ((half_a + half_a * jnp.tanh(half_a)) * b).astype(BF16)
        for_row_tiles(act)

        def down(off, size):
            hs = h_ref[pl.ds(off, size), :MOE_FB]
            for c0, c1 in MOE_DOWN_PARTS:
                acc_ref[pl.ds(off, size), c0:c1] = _dot(hs, w2r[0, :, c0:c1].astype(BF16))
        for_row_tiles(down)

    @pl.when((n > 0) & (j == MOE_NK + 1))
    def _():
        def down(off, size):
            hs = h_ref[pl.ds(off, size), MOE_FB:]
            for c0, c1 in MOE_DOWN_PARTS:
                acc_ref[pl.ds(off, size), c0:c1] += _dot(hs, w2r[MOE_NB - 1, :MOE_FB_LAST, c0:c1].astype(BF16))
        for_row_tiles(down)
        for_rows(n, lambda *row: row_out(i, *row).start())
        nxt = jnp.minimum(i + 1, n_items - 1)

        @pl.when((i == n_items - 1) | (in_ref[nxt] == 0))
        def _():
            wait_rows(n, rows_out)


def _moe_experts(xn, w1, w3, w2, item_e, item_start, item_n, order):
    S = xn.shape[0]
    n_items = item_e.shape[0]

    hbm = pl.BlockSpec(memory_space=pl.ANY)
    grid_spec = pltpu.PrefetchScalarGridSpec(
        num_scalar_prefetch=4,
        grid=(n_items, MOE_STEPS),
        in_specs=[hbm, hbm, hbm, hbm],
        out_specs=hbm,
        scratch_shapes=[
            pltpu.VMEM((1, MOE_CAP, D_MODEL), F32),
            pltpu.VMEM((MOE_NK, MOE_CAP, MOE_KC), BF16),
            pltpu.VMEM((MOE_CAP, 2 * EXPERT_FF), F32),
            pltpu.VMEM((MOE_CAP, EXPERT_FF), BF16),
            pltpu.VMEM((MOE_CAP, D_MODEL), F32),
            pltpu.VMEM((MOE_UP_RING, MOE_KC, EXPERT_FF), F32),
            pltpu.VMEM((MOE_UP_RING, MOE_KC, EXPERT_FF), F32),
            pltpu.VMEM((MOE_NB, MOE_FB, D_MODEL), F32),
            pltpu.SemaphoreType.DMA((1,)),
            pltpu.SemaphoreType.DMA(()),
            pltpu.SemaphoreType.DMA((MOE_UP_RING,)),
            pltpu.SemaphoreType.DMA((MOE_NB,)),
        ],
    )
    return pl.pallas_call(
        _moe_kernel,
        grid_spec=grid_spec,
        out_shape=jax.ShapeDtypeStruct((TOP_K * S, D_MODEL), F32),
        compiler_params=_cparams(("arbitrary", "arbitrary")),
        name="moe_experts",
    )(item_e, item_start, item_n, order, xn, w1, w3, w2)


def _moe_items(eid, cap):
    A = eid.size
    flat_e = eid.reshape(A)
    order = jnp.argsort(flat_e, stable=True).astype(jnp.int32)
    experts = jnp.arange(N_EXPERTS + 1, dtype=jnp.int32)
    starts = jnp.sum((flat_e[None, :] < experts[:, None]).astype(jnp.int32), axis=1)
    counts = starts[1:] - starts[:-1]
    per_e = (counts + cap - 1) // cap
    item_end = jnp.cumsum(per_e)
    total = item_end[-1]
    n_items = N_EXPERTS + A // cap
    idx = jnp.arange(n_items, dtype=jnp.int32)
    clamped = jnp.minimum(idx, total - 1)
    first_after = jnp.sum((item_end[None, :] <= clamped[:, None]).astype(jnp.int32), axis=1)
    e = jnp.minimum(first_after, N_EXPERTS - 1)
    local = clamped - (item_end[e] - per_e[e])
    used = idx < total
    item_start = jnp.where(used, starts[e] + local * cap, 0).astype(jnp.int32)
    item_n = jnp.where(used, jnp.clip(counts[e] - local * cap, 0, cap), 0).astype(jnp.int32)
    return e, item_start, item_n, order


def _combine_kernel(x1_ref, y0_ref, y1_ref, g_ref, o_ref):
    g = g_ref[...]
    o_ref[...] = x1_ref[...] + (g[:, 0:1] * y0_ref[...] + g[:, 1:2] * y1_ref[...])


def _combine(x1, y, gates, tm=512):
    S = x1.shape[0]
    nb = S // tm
    return pl.pallas_call(
        _combine_kernel,
        grid=(nb,),
        in_specs=[
            pl.BlockSpec((tm, D_MODEL), lambda i: (i, 0)),
            pl.BlockSpec((tm, D_MODEL), lambda i: (i, 0)),
            pl.BlockSpec((tm, D_MODEL), lambda i: (nb + i, 0)),
            pl.BlockSpec((tm, LANES), lambda i: (i, 0)),
        ],
        out_specs=pl.BlockSpec((tm, D_MODEL), lambda i: (i, 0)),
        out_shape=jax.ShapeDtypeStruct((S, D_MODEL), F32),
        compiler_params=_cparams(("parallel",)),
        name="moe_combine",
    )(x1, y, y, gates)


def _rot_half_cols(w, half):
    return jnp.concatenate([-w[..., half:], w[..., :half]], axis=-1)


def _rope_tables(S):
    pos = np.arange(S, dtype=np.float32)[:, None]

    def cs(half):
        inv = (np.float32(ROPE_THETA) ** (-np.arange(half, dtype=np.float32) / np.float32(half))).astype(np.float32)
        ang = (pos * inv[None, :]).astype(np.float64)
        return np.cos(ang).astype(np.float32), np.sin(ang).astype(np.float32)

    c32, s32 = cs(MLA_ROPE_DIM // 2)
    t64 = np.concatenate([c32, c32, s32, s32], axis=-1)
    c64, s64 = cs(DIL_HEAD_DIM // 2)
    cos128 = np.concatenate([c64, c64], axis=-1)
    sin128 = np.concatenate([-s64, s64], axis=-1)
    return jnp.asarray(t64), jnp.asarray(cos128), jnp.asarray(sin128)


def _residue_perm(dil):
    n = PERM_TILE // dil
    rows = np.arange(PERM_TILE)
    src = (rows % n) * dil + rows // n
    return jnp.asarray(src[:, None] == np.arange(PERM_TILE)[None, :], dtype=BF16)


def kernel(x, norm1_g, w_in, g_cq, g_ckv, w_uq, w_ukv, mla_q_norm_g, mla_k_norm_g, dil_q_norm_g,
           dil_k_norm_g, w_out, norm2_g, w_group, b_group, w_expert, b_expert, w1, w3, w2):
    B, S, D = x.shape
    assert B == 1 and D == D_MODEL and norm1_g.shape[0] == 1
    x2 = x.reshape(S, D)
    half_r = MLA_ROPE_DIM // 2
    o1, o2 = Q_LORA_RANK, Q_LORA_RANK + KV_LORA_RANK
    o3 = o2 + MLA_ROPE_DIM

    w_z = _regroup_w_in(w_in)

    wq = w_uq[0].reshape(Q_LORA_RANK, N_MLA_HEADS, MLA_QK_DIM)
    wq_pe = wq[..., MLA_NOPE_DIM:]
    wq_ext = jnp.concatenate([wq, _rot_half_cols(wq_pe, half_r)], axis=-1)
    wq_ext = wq_ext.reshape(Q_LORA_RANK, N_MLA_HEADS * MLA_PAD).astype(BF16)
    wkv = w_ukv[0].astype(BF16)

    def ext_gain(g):
        pe = g[MLA_NOPE_DIM:]
        return jnp.concatenate([g, pe[half_r:], pe[:half_r]])[None, :]

    gq_ext = ext_gain(mla_q_norm_g[0])
    gk_ext = ext_gain(mla_k_norm_g[0])
    t64, cos128, sin128 = _rope_tables(S)
    p4, p16 = _residue_perm(4), _residue_perm(16)

    z = _inproj(x2, norm1_g, w_z)
    q, k, vt, dq, dk, dq4, dk4, dv4, dq16, dk16, dv16 = _prep(
        z, g_cq, g_ckv, wq_ext, wkv, gq_ext, gk_ext[:, :LANES], gk_ext[:, LANES:],
        dil_q_norm_g, dil_k_norm_g, t64, cos128, sin128, p4, p16)
    mla_o = _mla_attn(q, k, vt)

    (w_1, d_1), (w_4, d_4), (w_16, d_16) = DIL_PATTERNS
    dil1 = _dil_pattern(dq[None], dk[None], z[None], Z_DV, w_1, d_1, hb=2, gb=4)
    dil4 = _dil_pattern(dq4, dk4, dv4, 0, w_4, d_4, hb=8, gb=1)
    dil16 = _dil_pattern(dq16, dk16, dv16, 0, w_16, d_16, hb=8, gb=1)

    pad = LANES - N_EXPERT_GROUPS - N_EXPERTS
    w_router = jnp.concatenate([w_group[0], w_expert[0], jnp.zeros((D, pad), F32)], axis=-1)
    b_router = jnp.concatenate([b_group[0], b_expert[0], jnp.zeros((pad,), F32)])[None, :]
    wr_hi = w_router.astype(BF16)
    wr_lo = (w_router - wr_hi.astype(F32)).astype(BF16)
    x1, xn, route_i, route_f = _outproj_router(
        mla_o, dil1, dil4, dil16, p4.T, p16.T, x2, w_out[0].astype(BF16), norm2_g,
        jnp.concatenate([wr_hi, wr_lo], axis=-1), wr_hi, b_router)

    item_e, item_start, item_n, order = _moe_items(route_i[:, :TOP_K], MOE_CAP)
    y = _moe_experts(xn, w1[0], w3[0], w2[0], item_e, item_start, item_n, order)
    out = _combine(x1, y, route_f)
    return out.reshape(B, S, D)
```

```python
import functools
import math

import jax
import jax.numpy as jnp
import numpy as np
from jax import lax
from jax.experimental import pallas as pl
from jax.experimental.pallas import tpu as pltpu

D_MODEL = 2048
N_MLA_HEADS = 8
MLA_NOPE_DIM = 128
MLA_ROPE_DIM = 64
MLA_QK_DIM = MLA_NOPE_DIM + MLA_ROPE_DIM
MLA_V_DIM = 128
Q_LORA_RANK = 512
KV_LORA_RANK = 512
N_DIL_HEADS = 8
DIL_HEAD_DIM = 128
DIL_PATTERNS = ((128, 1), (512, 4), (2048, 16))
ROPE_THETA = 10000.0
NORM_EPS = 1e-6
NEG_INF = -1e30
N_EXPERT_GROUPS = 8
EXPERTS_PER_GROUP = 8
N_EXPERTS = N_EXPERT_GROUPS * EXPERTS_PER_GROUP
TOP_K = 2
EXPERT_FF = 1408

LANES = 128
BF16_ROWS = 16
MXU_DIM = 256
HD = N_DIL_HEADS * DIL_HEAD_DIM
MLA_PAD = MXU_DIM
Z_DQ, Z_DK, Z_DV = 0, HD, 2 * HD
Z_CQ = 3 * HD
Z_CKV = Z_CQ + Q_LORA_RANK
Z_KR = Z_CKV + KV_LORA_RANK
Z_WIDTH = Z_KR + 2 * MLA_ROPE_DIM

ATT_VC = 512
ATT_TK = 512
ATT_TQ = 1024
ATT_UNROLL = 8
VT_ROWS = MLA_V_DIM + BF16_ROWS
PERM_TILE = 256
LOG2E = math.log2(math.e)

VMEM_LIMIT = 56 * 1024 * 1024

F32 = jnp.float32
BF16 = jnp.bfloat16


def _cparams(sem, vmem=VMEM_LIMIT):
    return pltpu.CompilerParams(dimension_semantics=sem, vmem_limit_bytes=vmem)


def _dot(a, b):
    return jnp.dot(a, b, preferred_element_type=F32)


def _dot_nt(a, b):
    return lax.dot_general(a, b, (((1,), (1,)), ((), ())), preferred_element_type=F32)


REGROUP_ROWS = MLA_ROPE_DIM


REGROUP_PER_STEP = 6


def _regroup_kernel(*refs):
    o_ref = refs[-1]
    rb = REGROUP_ROWS
    half = MLA_ROPE_DIM // 2
    last_step = pl.program_id(0) == pl.num_programs(0) - 1
    for u, w_ref in enumerate(refs[:-1]):
        x = w_ref[...]
        if u == REGROUP_PER_STEP - 1:
            rot = jnp.concatenate([-x[half:], x[:half]], axis=0)
            x = jnp.where(last_step, rot, x)
        o_ref[u * rb:(u + 1) * rb, :] = x.astype(BF16)


def _regroup_w_in(w_in):
    w_t = jnp.swapaxes(w_in[0], 0, 1)
    d = w_t.shape[1]
    rb = REGROUP_ROWS
    n_tail = (Z_CQ - Z_DQ) // rb
    n_lat = (Z_KR - Z_CQ) // rb
    first_tail = n_lat + 1

    def src(u, step):
        i = step * REGROUP_PER_STEP + u
        return jnp.where(i < n_tail, i + first_tail, jnp.where(i < n_tail + n_lat, i - n_tail, n_lat)), 0

    per = REGROUP_PER_STEP
    assert Z_WIDTH % (per * rb) == 0
    return pl.pallas_call(
        _regroup_kernel,
        grid=(Z_WIDTH // (per * rb),),
        in_specs=[pl.BlockSpec((rb, d), functools.partial(src, u)) for u in range(per)],
        out_specs=pl.BlockSpec((per * rb, d), lambda s: (s, 0)),
        out_shape=jax.ShapeDtypeStruct((Z_WIDTH, d), BF16),
        compiler_params=_cparams(("parallel",)),
        name="regroup_w_in",
    )(*([w_t] * per))


def _inproj_kernel(x_ref, g_ref, w_ref, z_ref):
    x = x_ref[...]
    r = lax.rsqrt(jnp.mean(x * x, axis=-1, keepdims=True) + NORM_EPS)
    h = (x * r * g_ref[...]).astype(BF16)
    z_ref[...] = _dot_nt(h, w_ref[...]).astype(BF16)


def _inproj(x2, g1, w_z, tm=256):
    S = x2.shape[0]
    return pl.pallas_call(
        _inproj_kernel,
        grid=(S // tm,),
        in_specs=[
            pl.BlockSpec((tm, D_MODEL), lambda i: (i, 0)),
            pl.BlockSpec((1, D_MODEL), lambda i: (0, 0)),
            pl.BlockSpec((Z_WIDTH, D_MODEL), lambda i: (0, 0)),
        ],
        out_specs=pl.BlockSpec((tm, Z_WIDTH), lambda i: (i, 0)),
        out_shape=jax.ShapeDtypeStruct((S, Z_WIDTH), BF16),
        compiler_params=_cparams(("parallel",)),
        name="inproj",
    )(x2, g1, w_z)


def _prep_kernel(zdq_ref, zdk_ref, zdv_ref, zcq_ref, zckv_ref, zkr_ref,
                 gcq_ref, gckv_ref, wq_ref, wkv_ref, gq_ref, gkn_ref, gkp_ref,
                 gdq_ref, gdk_ref, t64_ref, cos_ref, sin_ref, p4_ref, p16_ref,
                 q_ref, k_ref, vt_ref, dq_ref, dk_ref,
                 dq4_ref, dk4_ref, dv4_ref, dq16_ref, dk16_ref, dv16_ref):
    tm = zcq_ref.shape[0]
    lane = lax.broadcasted_iota(jnp.int32, (1, LANES), 1)
    first_half = lane < MLA_ROPE_DIM

    def rms_rows(c, g):
        c = c.astype(F32)
        r = lax.rsqrt(jnp.mean(c * c, axis=-1, keepdims=True) + NORM_EPS)
        return (c * r * g).astype(BF16)

    cq = rms_rows(zcq_ref[...], gcq_ref[...])
    ckv = rms_rows(zckv_ref[...], gckv_ref[...])
    qe = _dot(cq, wq_ref[...])
    kv = _dot(ckv, wkv_ref[...])
    kr = zkr_ref[...].astype(F32)
    kr_ss = jnp.sum(jnp.where(first_half, kr * kr, 0.0), axis=-1, keepdims=True)
    t64 = t64_ref[...]
    q_scale = MLA_QK_DIM ** -0.5 * LOG2E

    def rope64(ext):
        t = ext * t64
        return jnp.where(first_half, t + pltpu.roll(t, MLA_ROPE_DIM, 1), 0.0)

    ones_rows = (lax.broadcasted_iota(jnp.int32, (BF16_ROWS, tm), 0) == 0).astype(BF16)
    eye = (lax.broadcasted_iota(jnp.int32, (MLA_V_DIM, MLA_V_DIM), 0)
           == lax.broadcasted_iota(jnp.int32, (MLA_V_DIM, MLA_V_DIM), 1)).astype(BF16)
    for h in range(N_MLA_HEADS):
        base = h * MLA_PAD
        qn = qe[:, base:base + LANES]
        qp = qe[:, base + LANES:base + 2 * LANES]
        ss = (jnp.sum(qn * qn, axis=-1, keepdims=True)
              + jnp.sum(jnp.where(first_half, qp * qp, 0.0), axis=-1, keepdims=True))
        r = lax.rsqrt(ss * (1.0 / MLA_QK_DIM) + NORM_EPS) * q_scale
        q_ref[:, base:base + LANES] = (qn * r * gq_ref[:, :LANES]).astype(BF16)
        q_ref[:, base + LANES:base + MLA_PAD] = rope64(qp * r * gq_ref[:, LANES:]).astype(BF16)

        kn = kv[:, base:base + LANES]
        ss = jnp.sum(kn * kn, axis=-1, keepdims=True) + kr_ss
        r = lax.rsqrt(ss * (1.0 / MLA_QK_DIM) + NORM_EPS)
        k_ref[:, base:base + LANES] = (kn * r * gkn_ref[...]).astype(BF16)
        k_ref[:, base + LANES:base + MLA_PAD] = rope64(kr * r * gkp_ref[...]).astype(BF16)
        v_h = kv[:, base + LANES:base + 2 * LANES].astype(BF16)
        vt_ref[h, :MLA_V_DIM, :] = _dot_nt(eye, v_h).astype(BF16)
        vt_ref[h, MLA_V_DIM:, :] = ones_rows

    cos = cos_ref[...]
    sin = sin_ref[...]
    d_scale = DIL_HEAD_DIM ** -0.5

    def dil_head(x, g, scale):
        x = x.astype(F32)
        r = lax.rsqrt(jnp.mean(x * x, axis=-1, keepdims=True) + NORM_EPS)
        y = x * r * g
        return ((y * cos + pltpu.roll(y, DIL_HEAD_DIM // 2, 1) * sin) * scale).astype(BF16)

    for h in range(N_DIL_HEADS):
        sl = slice(h * DIL_HEAD_DIM, (h + 1) * DIL_HEAD_DIM)
        dq_ref[:, sl] = dil_head(zdq_ref[:, sl], gdq_ref[...], d_scale)
        dk_ref[:, sl] = dil_head(zdk_ref[:, sl], gdk_ref[...], 1.0)

    for src, d4, d16 in ((dq_ref, dq4_ref, dq16_ref), (dk_ref, dk4_ref, dk16_ref), (zdv_ref, dv4_ref, dv16_ref)):
        for sub in range(tm // PERM_TILE):
            xs = src[sub * PERM_TILE:(sub + 1) * PERM_TILE, :]
            for dil, p_ref, dst in ((4, p4_ref, d4), (16, p16_ref, d16)):
                n = PERM_TILE // dil
                xp = _dot(p_ref[...], xs).astype(BF16)
                for r in range(dil):
                    dst[r, sub * n:(sub + 1) * n, :] = xp[r * n:(r + 1) * n, :]


def _prep(z, gcq, gckv, wq_ext, wkv, gq_ext, gk_nope, gk_pe, gdq, gdk, t64, cos128, sin128, p4, p16):
    S = z.shape[0]
    tm = ATT_VC
    row = lambda w, j: pl.BlockSpec((tm, w), lambda i, j=j: (i, j))
    full = lambda a: pl.BlockSpec(a.shape, lambda i: (0, 0))
    res = lambda dil: pl.BlockSpec((dil, tm // dil, HD), lambda i: (0, i, 0))
    qk_w = N_MLA_HEADS * MLA_PAD
    res_shape = lambda dil: jax.ShapeDtypeStruct((dil, S // dil, HD), BF16)
    return pl.pallas_call(
        _prep_kernel,
        grid=(S // tm,),
        in_specs=[
            row(HD, Z_DQ // HD), row(HD, Z_DK // HD), row(HD, Z_DV // HD),
            row(Q_LORA_RANK, Z_CQ // Q_LORA_RANK), row(KV_LORA_RANK, Z_CKV // KV_LORA_RANK),
            row(LANES, Z_KR // LANES),
            full(gcq), full(gckv), full(wq_ext), full(wkv), full(gq_ext), full(gk_nope), full(gk_pe),
            full(gdq), full(gdk),
            row(LANES, 0), row(LANES, 0), row(LANES, 0), full(p4), full(p16),
        ],
        out_specs=[row(qk_w, 0), row(qk_w, 0),
                   pl.BlockSpec((N_MLA_HEADS, None, VT_ROWS, tm), lambda i: (0, i, 0, 0)),
                   row(HD, 0), row(HD, 0),
                   res(4), res(4), res(4), res(16), res(16), res(16)],
        out_shape=[
            jax.ShapeDtypeStruct((S, qk_w), BF16), jax.ShapeDtypeStruct((S, qk_w), BF16),
            jax.ShapeDtypeStruct((N_MLA_HEADS, S // tm, VT_ROWS, tm), BF16),
            jax.ShapeDtypeStruct((S, HD), BF16), jax.ShapeDtypeStruct((S, HD), BF16),
            res_shape(4), res_shape(4), res_shape(4), res_shape(16), res_shape(16), res_shape(16),
        ],
        compiler_params=_cparams(("parallel",)),
        name="qkv_prep",
    )(z, z, z, z, z, z, gcq, gckv, wq_ext, wkv, gq_ext, gk_nope, gk_pe, gdq, gdk, t64, cos128, sin128, p4, p16)


def _mla_attn_kernel(q_ref, k_ref, vt_ref, o_ref, m_ref, acc_ref, s_ref):
    per_chunk = ATT_TK // ATT_VC
    n_chunks = vt_ref.shape[0] // per_chunk
    q = q_ref[...]

    def scores(c):
        off = pl.multiple_of(c * ATT_TK, ATT_TK)
        return _dot_nt(k_ref[pl.ds(off, ATT_TK), :], q)

    m_ref[...] = jnp.full(m_ref.shape, -jnp.inf, F32)
    acc_ref[...] = jnp.zeros(acc_ref.shape, F32)

    def fold(c, slot):
        s = s_ref[slot]
        m_old = m_ref[...]
        m_new = jnp.maximum(m_old, jnp.max(s, axis=0, keepdims=True))
        alpha = jnp.exp2(m_old - m_new)
        p = jnp.exp2(s - m_new).astype(BF16)
        pv = _dot(vt_ref[c * per_chunk], p[:ATT_VC])
        for u in range(1, per_chunk):
            pv += _dot(vt_ref[c * per_chunk + u], p[u * ATT_VC:(u + 1) * ATT_VC])
        acc_ref[...] = alpha * acc_ref[...] + pv
        m_ref[...] = m_new

    s_ref[0] = scores(0)

    def body(t, carry):
        for u in range(ATT_UNROLL):
            c = ATT_UNROLL * t + u
            s_ref[(u + 1) % 2] = scores(jnp.minimum(c + 1, n_chunks - 1))
            fold(c, u % 2)
        return carry

    lax.fori_loop(0, n_chunks // ATT_UNROLL, body, 0)
    acc = acc_ref[...]
    o_t = acc[:MLA_V_DIM, :] / acc[MLA_V_DIM:MLA_V_DIM + 1, :]
    o_ref[...] = o_t.T.astype(o_ref.dtype)


def _mla_attn(q, k, vt):
    S = q.shape[0]
    n_chunks = vt.shape[1]
    return pl.pallas_call(
        _mla_attn_kernel,
        grid=(N_MLA_HEADS, S // ATT_TQ),
        in_specs=[
            pl.BlockSpec((ATT_TQ, MLA_PAD), lambda h, i: (i, h)),
            pl.BlockSpec((S, MLA_PAD), lambda h, i: (0, h)),
            pl.BlockSpec((None, n_chunks, VT_ROWS, ATT_VC), lambda h, i: (h, 0, 0, 0)),
        ],
        out_specs=pl.BlockSpec((ATT_TQ, MLA_V_DIM), lambda h, i: (i, h)),
        out_shape=jax.ShapeDtypeStruct((S, N_MLA_HEADS * MLA_V_DIM), BF16),
        scratch_shapes=[pltpu.VMEM((1, ATT_TQ), F32), pltpu.VMEM((VT_ROWS, ATT_TQ), F32),
                        pltpu.VMEM((2, ATT_TK, ATT_TQ), F32)],
        compiler_params=_cparams(("parallel", "parallel")),
        name="mla_attn",
    )(q, k, vt)


DIL_QB = 128


def _dil_kernel(q_ref, k_ref, v_ref, o_ref, lse_ref, *, half_w, hb, gb):
    L = q_ref.shape[0]
    kw = DIL_QB + 2 * half_w
    g_id = pl.program_id(1)
    lane = lax.broadcasted_iota(jnp.int32, (DIL_QB, LANES), 1)

    @pl.when(g_id == 0)
    def _():
        lse_ref[...] = jnp.zeros(lse_ref.shape, F32)

    def body(step, carry):
        qs_l, ks_l, qb, kb, vb = [], [], [], [], []
        for j in range(gb):
            qs = pl.multiple_of((step * gb + j) * DIL_QB, DIL_QB)
            ks = pl.multiple_of(jnp.clip(qs - half_w, 0, L - kw), half_w)
            qs_l.append(qs)
            ks_l.append(ks)
            for h in range(hb):
                sl = slice(h * DIL_HEAD_DIM, (h + 1) * DIL_HEAD_DIM)
                qb.append(q_ref[pl.ds(qs, DIL_QB), sl])
                kb.append(k_ref[pl.ds(ks, kw), sl])
                vb.append(v_ref[pl.ds(ks, kw), sl])
        q = jnp.stack(qb)
        k = jnp.stack(kb)
        v = jnp.stack(vb)
        s = jnp.einsum("gqd,gkd->gqk", q, k, preferred_element_type=F32)
        rel = (lax.broadcasted_iota(jnp.int32, (DIL_QB, kw), 0)
               - lax.broadcasted_iota(jnp.int32, (DIL_QB, kw), 1))
        bias = []
        for j in range(gb):
            mask = jnp.abs(rel + (qs_l[j] - ks_l[j])) <= half_w
            bias += [jnp.where(mask, 0.0, NEG_INF)] * hb
        s = s + jnp.stack(bias)
        m = jnp.max(s, axis=-1, keepdims=True)
        p = jnp.exp(s - m)
        den = jnp.sum(p, axis=-1, keepdims=True)
        o = jnp.einsum("gqk,gkd->gqd", p.astype(BF16), v, preferred_element_type=F32) / den
        lse = m + jnp.log(den)
        for j in range(gb):
            tile = lse_ref[pl.ds(qs_l[j], DIL_QB), :]
            for h in range(hb):
                sl = slice(h * DIL_HEAD_DIM, (h + 1) * DIL_HEAD_DIM)
                o_ref[pl.ds(qs_l[j], DIL_QB), sl] = o[j * hb + h].astype(o_ref.dtype)
                tile = jnp.where(lane == g_id * hb + h, lse[j * hb + h], tile)
            lse_ref[pl.ds(qs_l[j], DIL_QB), :] = tile
        return carry

    lax.fori_loop(0, L // (DIL_QB * gb), body, 0)


def _dil_pattern(dq, dk, dv, v_col0, window, dil, hb, gb):
    L = dq.shape[1]
    half_w = window // (2 * dil)
    w = hb * DIL_HEAD_DIM
    spec = lambda c0: pl.BlockSpec((None, L, w), lambda r, g, c0=c0: (r, 0, c0 + g))
    return pl.pallas_call(
        functools.partial(_dil_kernel, half_w=half_w, hb=hb, gb=gb),
        grid=(dil, N_DIL_HEADS // hb),
        in_specs=[spec(0), spec(0), spec(v_col0 // w)],
        out_specs=[spec(0), pl.BlockSpec((None, L, LANES), lambda r, g: (r, 0, 0))],
        out_shape=[jax.ShapeDtypeStruct((dil, L, HD), BF16), jax.ShapeDtypeStruct((dil, L, LANES), F32)],
        compiler_params=_cparams(("parallel", "arbitrary")),
        name=f"dil_attn_d{dil}",
    )(dq, dk, dv)


def _outproj_kernel(a_ref, o1_ref, l1_ref, o4_ref, l4_ref, o16_ref, l16_ref, p4t_ref, p16t_ref,
                    x_ref, wa_ref, wb_ref, g_ref, wr_ref, wrh_ref, br_ref,
                    x1_ref, xn_ref, ri_ref, rf_ref):
    tm = x_ref.shape[0]

    def to_token_order(o_ref, l_ref, pt_ref):
        pt = pt_ref[...]
        o = _dot(pt, o_ref[...].reshape(tm, HD))
        lse = l_ref[...].reshape(tm, LANES)
        hi = lse.astype(BF16)
        rem = lse - hi.astype(F32)
        mid = rem.astype(BF16)
        lo = (rem - mid.astype(F32)).astype(BF16)
        return o, _dot(pt, hi) + _dot(pt, mid) + _dot(pt, lo)

    o1 = o1_ref[...].astype(F32)
    l1 = l1_ref[...]
    o4, l4 = to_token_order(o4_ref, l4_ref, p4t_ref)
    o16, l16 = to_token_order(o16_ref, l16_ref, p16t_ref)
    big = jnp.maximum(jnp.maximum(l1, l4), l16)
    e1 = jnp.exp(l1 - big)
    e4 = jnp.exp(l4 - big)
    e16 = jnp.exp(l16 - big)
    inv = 1.0 / (e1 + e4 + e16)
    w1, w4, w16 = e1 * inv, e4 * inv, e16 * inv
    slabs = []
    for h in range(N_DIL_HEADS):
        sl = slice(h * DIL_HEAD_DIM, (h + 1) * DIL_HEAD_DIM)
        slabs.append(w1[:, h:h + 1] * o1[:, sl] + w4[:, h:h + 1] * o4[:, sl] + w16[:, h:h + 1] * o16[:, sl])
    dil_o = jnp.concatenate(slabs, axis=-1).astype(BF16)

    x1 = x_ref[...] + _dot(a_ref[...], wa_ref[...]) + _dot(dil_o, wb_ref[...])
    x1_ref[...] = x1
    r = lax.rsqrt(jnp.mean(x1 * x1, axis=-1, keepdims=True) + NORM_EPS)
    xn = x1 * r * g_ref[...]
    xn_ref[...] = xn
    xh = xn.astype(BF16)
    xl = (xn - xh.astype(F32)).astype(BF16)
    two = _dot(xh, wr_ref[...])
    logits = two[:, :LANES] + two[:, LANES:] + _dot(xl, wrh_ref[...]) + br_ref[...]
    lane = lax.broadcasted_iota(jnp.int32, logits.shape, 1)
    ninf = -jnp.inf

    def first_argmax(vals, vmax):
        return jnp.min(jnp.where(vals == vmax, lane, LANES), axis=-1, keepdims=True)

    coarse = jnp.where(lane < N_EXPERT_GROUPS, logits, ninf)
    cmax = jnp.max(coarse, axis=-1, keepdims=True)
    g = first_argmax(coarse, cmax)
    p_g = 1.0 / jnp.sum(jnp.exp(coarse - cmax), axis=-1, keepdims=True)
    lo_lane = N_EXPERT_GROUPS + g * EXPERTS_PER_GROUP
    fine = jnp.where((lane >= lo_lane) & (lane < lo_lane + EXPERTS_PER_GROUP), logits, ninf)
    v1 = jnp.max(fine, axis=-1, keepdims=True)
    j1 = first_argmax(fine, v1)
    fine2 = jnp.where(lane == j1, ninf, fine)
    v2 = jnp.max(fine2, axis=-1, keepdims=True)
    j2 = first_argmax(fine2, v2)
    e2 = jnp.exp(v2 - v1)
    g1 = 1.0 / (1.0 + e2)
    g2 = e2 / (1.0 + e2)
    ri_ref[...] = jnp.where(lane == 0, j1 - N_EXPERT_GROUPS, jnp.where(lane == 1, j2 - N_EXPERT_GROUPS, 0))
    rf_ref[...] = jnp.where(lane == 0, p_g * g1, jnp.where(lane == 1, p_g * g2, 0.0))


def _outproj_router(mla_o, dil1, dil4, dil16, p4t, p16t, x2, w_out_bf, g2, wr_two, wr_hi, b_router):
    S = x2.shape[0]
    tm = PERM_TILE
    half = N_MLA_HEADS * MLA_V_DIM
    row = lambda w: pl.BlockSpec((tm, w), lambda i: (i, 0))
    res = lambda dil, w: pl.BlockSpec((dil, tm // dil, w), lambda i: (0, i, 0))
    const = lambda a: pl.BlockSpec(a.shape, lambda i: (0, 0))
    return pl.pallas_call(
        _outproj_kernel,
        grid=(S // tm,),
        in_specs=[
            row(half),
            pl.BlockSpec((None, tm, HD), lambda i: (0, i, 0)), pl.BlockSpec((None, tm, LANES), lambda i: (0, i, 0)),
            res(4, HD), res(4, LANES), res(16, HD), res(16, LANES), const(p4t), const(p16t),
            row(D_MODEL),
            pl.BlockSpec((half, D_MODEL), lambda i: (0, 0)),
            pl.BlockSpec((HD, D_MODEL), lambda i: (1, 0)),
            const(g2), const(wr_two), const(wr_hi), const(b_router),
        ],
        out_specs=[row(D_MODEL), row(D_MODEL), row(LANES), row(LANES)],
        out_shape=[
            jax.ShapeDtypeStruct((S, D_MODEL), F32), jax.ShapeDtypeStruct((S, D_MODEL), F32),
            jax.ShapeDtypeStruct((S, LANES), jnp.int32), jax.ShapeDtypeStruct((S, LANES), F32),
        ],
        compiler_params=_cparams(("parallel",)),
        name="outproj_router",
    )(mla_o, dil1[0], dil1[1], dil4[0], dil4[1], dil16[0], dil16[1], p4t, p16t,
      x2, w_out_bf, w_out_bf, g2, wr_two, wr_hi, b_router)


MOE_CAP = 512
MOE_RH = 128
MOE_KC = 512
MOE_NK = D_MODEL // MOE_KC
MOE_FB = 768
MOE_NB = 2
MOE_FB_LAST = EXPERT_FF - (MOE_NB - 1) * MOE_FB
MOE_STEPS = MOE_NK + MOE_NB
MOE_LOOKAHEAD = 3
MOE_UP_RING = MOE_LOOKAHEAD + 1
MOE_UP_PARTS = ((0, 1024), (1024, 2048), (2048, 2 * EXPERT_FF))
MOE_DOWN_PARTS = ((0, 1024), (1024, D_MODEL))
DMA_UNROLL = 8
assert 0 < MOE_FB_LAST <= MOE_FB and MOE_FB % LANES == 0 and MOE_FB_LAST % LANES == 0


def _moe_kernel(ie_ref, is_ref, in_ref, ord_ref,
                xn_hbm, w1_hbm, w3_hbm, w2_hbm,
                y_hbm, xf_ref, xb_ref, ab_ref, h_ref, acc_ref, w1r, w3r, w2r,
                sem_in, sem_out, sem_up, sem_dn):
    i = pl.program_id(0)
    j = pl.program_id(1)
    n_items = pl.num_programs(0)
    n_tokens = xn_hbm.shape[0]
    n = in_ref[i]
    slot = 0

    def up_ring(item, kc):
        return lax.rem(item * MOE_NK + kc, MOE_UP_RING)

    def block_copies(item, step):
        e = ie_ref[item]
        if step < MOE_NK:
            ring = up_ring(item, step)
            rows = pl.ds(step * MOE_KC, MOE_KC)
            return (pltpu.make_async_copy(w1_hbm.at[e, rows, :], w1r.at[ring], sem_up.at[ring]),
                    pltpu.make_async_copy(w3_hbm.at[e, rows, :], w3r.at[ring], sem_up.at[ring]))
        blk = step - MOE_NK
        size = MOE_FB if blk < MOE_NB - 1 else MOE_FB_LAST
        return (pltpu.make_async_copy(w2_hbm.at[e, pl.ds(blk * MOE_FB, size), :],
                                      w2r.at[blk, pl.ds(0, size), :], sem_dn.at[blk]),)

    def start_block(item, step):
        for c in block_copies(item, step):
            c.start()

    for js in range(MOE_STEPS):
        @pl.when((n > 0) & (j == js))
        def _():
            if js == 0:
                @pl.when(i == 0)
                def _():
                    for s0 in range(MOE_LOOKAHEAD):
                        start_block(0, s0)
            for c in block_copies(i, js):
                c.wait()
            tgt = js + MOE_LOOKAHEAD
            if tgt < MOE_STEPS:
                start_block(i, tgt)
            else:
                nxt = jnp.minimum(i + 1, n_items - 1)

                @pl.when((i + 1 < n_items) & (in_ref[nxt] > 0))
                def _():
                    start_block(nxt, tgt - MOE_STEPS)

    def row_in(item, sl, base, u, width):
        tok = lax.shift_right_logical(ord_ref[is_ref[item] + base + u], 1)
        dst = xf_ref.at[sl, pl.ds(base, width), :].at[pl.ds(u, 1), :]
        return pltpu.make_async_copy(xn_hbm.at[pl.ds(tok, 1), :], dst, sem_in.at[sl])

    def row_out(item, base, u, width):
        a = ord_ref[is_ref[item] + base + u]
        dst = (a & 1) * n_tokens + lax.shift_right_logical(a, 1)
        src = acc_ref.at[pl.ds(base, width), :].at[pl.ds(u, 1), :]
        return pltpu.make_async_copy(src, y_hbm.at[pl.ds(dst, 1), :], sem_out)

    def for_rows(count, fn):
        groups = lax.div(count, DMA_UNROLL)

        def group(g, c):
            base = pl.multiple_of(g * DMA_UNROLL, DMA_UNROLL)
            for u in range(DMA_UNROLL):
                fn(base, u, DMA_UNROLL)
            return c
        lax.fori_loop(0, groups, group, 0)

        def single(r, c):
            fn(r, 0, 1)
            return c
        lax.fori_loop(groups * DMA_UNROLL, count, single, 0)

    def wait_rows(count, rows_desc):
        groups = lax.div(count, DMA_UNROLL)

        def group(g, c):
            rows_desc(pl.multiple_of(g * DMA_UNROLL, DMA_UNROLL), DMA_UNROLL).wait()
            return c
        lax.fori_loop(0, groups, group, 0)

        def single(r, c):
            rows_desc(r, 1).wait()
            return c
        lax.fori_loop(groups * DMA_UNROLL, count, single, 0)

    def rows_in(sl, r0, size):
        return pltpu.make_async_copy(xn_hbm.at[pl.ds(0, size), :], xf_ref.at[sl, pl.ds(r0, size), :], sem_in.at[sl])

    def rows_out(r0, size):
        return pltpu.make_async_copy(acc_ref.at[pl.ds(r0, size), :], y_hbm.at[pl.ds(0, size), :], sem_out)

    def for_row_tiles(fn):
        tiles = lax.div(n + (MOE_RH - 1), MOE_RH)
        for k in range(1, MOE_CAP // MOE_RH + 1):
            @pl.when(tiles == k)
            def _():
                fn(0, k * MOE_RH)

    @pl.when((i == 0) & (j == 0))
    def _():
        xf_ref[...] = jnp.zeros(xf_ref.shape, F32)
        for_rows(n, lambda *row: row_in(0, 0, *row).start())

    @pl.when((n > 0) & (j == 0))
    def _():
        wait_rows(n, functools.partial(rows_in, slot))

        @pl.when(i > 0)
        def _():
            wait_rows(in_ref[i - 1], rows_out)

        for t in range(MOE_CAP // MOE_RH):
            @pl.when(t * MOE_RH < n)
            def _():
                sl = pl.ds(t * MOE_RH, MOE_RH)
                for kc in range(MOE_NK):
                    xb_ref[kc, sl, :] = xf_ref[slot, sl, kc * MOE_KC:(kc + 1) * MOE_KC].astype(BF16)
                ab_ref[sl, :] = jnp.zeros((MOE_RH, 2 * EXPERT_FF), F32)

    @pl.when((n > 0) & (j == 1) & (i + 1 < n_items))
    def _():
        nxt = jnp.minimum(i + 1, n_items - 1)
        for_rows(in_ref[nxt], lambda *row: row_in(nxt, slot, *row).start())

    @pl.when((n > 0) & (j < MOE_NK))
    def _():
        kc = jnp.minimum(j, MOE_NK - 1)
        ring = up_ring(i, kc)

        def cat_cols(c0, c1):
            pieces = []
            if c0 < EXPERT_FF:
                pieces.append(w1r[ring, :, c0:min(c1, EXPERT_FF)].astype(BF16))
            if c1 > EXPERT_FF:
                pieces.append(w3r[ring, :, max(c0, EXPERT_FF) - EXPERT_FF:c1 - EXPERT_FF].astype(BF16))
            return pieces[0] if len(pieces) == 1 else jnp.concatenate(pieces, axis=-1)

        def up(off, size):
            xs = xb_ref[kc, pl.ds(off, size), :]
            for c0, c1 in MOE_UP_PARTS:
                ab_ref[pl.ds(off, size), c0:c1] += _dot(xs, cat_cols(c0, c1))
        for_row_tiles(up)

    @pl.when((n > 0) & (j == MOE_NK))
    def _():
        def act(off, size):
            a = ab_ref[pl.ds(off, size), :EXPERT_FF]
            b = ab_ref[pl.ds(off, size), EXPERT_FF:]
            half_a = 0.5 * a
            h_ref[pl.ds(off, size), :] = ((half_a + half_a * jnp.tanh(half_a)) * b).astype(BF16)
        for_row_tiles(act)

        def down(off, size):
            hs = h_ref[pl.ds(off, size), :MOE_FB]
            for c0, c1 in MOE_DOWN_PARTS:
                acc_ref[pl.ds(off, size), c0:c1] = _dot(hs, w2r[0, :, c0:c1].astype(BF16))
        for_row_tiles(down)

    @pl.when((n > 0) & (j == MOE_NK + 1))
    def _():
        def down(off, size):
            hs = h_ref[pl.ds(off, size), MOE_FB:]
            for c0, c1 in MOE_DOWN_PARTS:
                acc_ref[pl.ds(off, size), c0:c1] += _dot(hs, w2r[MOE_NB - 1, :MOE_FB_LAST, c0:c1].astype(BF16))
        for_row_tiles(down)
        for_rows(n, lambda *row: row_out(i, *row).start())
        nxt = jnp.minimum(i + 1, n_items - 1)

        @pl.when((i == n_items - 1) | (in_ref[nxt] == 0))
        def _():
            wait_rows(n, rows_out)


def _moe_experts(xn, w1, w3, w2, item_e, item_start, item_n, order):
    S = xn.shape[0]
    n_items = item_e.shape[0]

    hbm = pl.BlockSpec(memory_space=pl.ANY)
    grid_spec = pltpu.PrefetchScalarGridSpec(
        num_scalar_prefetch=4,
        grid=(n_items, MOE_STEPS),
        in_specs=[hbm, hbm, hbm, hbm],
        out_specs=hbm,
        scratch_shapes=[
            pltpu.VMEM((1, MOE_CAP, D_MODEL), F32),
            pltpu.VMEM((MOE_NK, MOE_CAP, MOE_KC), BF16),
            pltpu.VMEM((MOE_CAP, 2 * EXPERT_FF), F32),
            pltpu.VMEM((MOE_CAP, EXPERT_FF), BF16),
            pltpu.VMEM((MOE_CAP, D_MODEL), F32),
            pltpu.VMEM((MOE_UP_RING, MOE_KC, EXPERT_FF), F32),
            pltpu.VMEM((MOE_UP_RING, MOE_KC, EXPERT_FF), F32),
            pltpu.VMEM((MOE_NB, MOE_FB, D_MODEL), F32),
            pltpu.SemaphoreType.DMA((1,)),
            pltpu.SemaphoreType.DMA(()),
            pltpu.SemaphoreType.DMA((MOE_UP_RING,)),
            pltpu.SemaphoreType.DMA((MOE_NB,)),
        ],
    )
    return pl.pallas_call(
        _moe_kernel,
        grid_spec=grid_spec,
        out_shape=jax.ShapeDtypeStruct((TOP_K * S, D_MODEL), F32),
        compiler_params=_cparams(("arbitrary", "arbitrary")),
        name="moe_experts",
    )(item_e, item_start, item_n, order, xn, w1, w3, w2)


def _moe_items(eid, cap):
    A = eid.size
    flat_e = eid.reshape(A)
    order = jnp.argsort(flat_e, stable=True).astype(jnp.int32)
    experts = jnp.arange(N_EXPERTS + 1, dtype=jnp.int32)
    starts = jnp.sum((flat_e[None, :] < experts[:, None]).astype(jnp.int32), axis=1)
    counts = starts[1:] - starts[:-1]
    per_e = (counts + cap - 1) // cap
    item_end = jnp.cumsum(per_e)
    total = item_end[-1]
    n_items = N_EXPERTS + A // cap
    idx = jnp.arange(n_items, dtype=jnp.int32)
    clamped = jnp.minimum(idx, total - 1)
    first_after = jnp.sum((item_end[None, :] <= clamped[:, None]).astype(jnp.int32), axis=1)
    e = jnp.minimum(first_after, N_EXPERTS - 1)
    local = clamped - (item_end[e] - per_e[e])
    used = idx < total
    item_start = jnp.where(used, starts[e] + local * cap, 0).astype(jnp.int32)
    item_n = jnp.where(used, jnp.clip(counts[e] - local * cap, 0, cap), 0).astype(jnp.int32)
    return e, item_start, item_n, order


def _combine_kernel(x1_ref, y0_ref, y1_ref, g_ref, o_ref):
    g = g_ref[...]
    o_ref[...] = x1_ref[...] + (g[:, 0:1] * y0_ref[...] + g[:, 1:2] * y1_ref[...])


def _combine(x1, y, gates, tm=512):
    S = x1.shape[0]
    nb = S // tm
    return pl.pallas_call(
        _combine_kernel,
        grid=(nb,),
        in_specs=[
            pl.BlockSpec((tm, D_MODEL), lambda i: (i, 0)),
            pl.BlockSpec((tm, D_MODEL), lambda i: (i, 0)),
            pl.BlockSpec((tm, D_MODEL), lambda i: (nb + i, 0)),
            pl.BlockSpec((tm, LANES), lambda i: (i, 0)),
        ],
        out_specs=pl.BlockSpec((tm, D_MODEL), lambda i: (i, 0)),
        out_shape=jax.ShapeDtypeStruct((S, D_MODEL), F32),
        compiler_params=_cparams(("parallel",)),
        name="moe_combine",
    )(x1, y, y, gates)


def _rot_half_cols(w, half):
    return jnp.concatenate([-w[..., half:], w[..., :half]], axis=-1)


def _rope_tables(S):
    pos = np.arange(S, dtype=np.float32)[:, None]

    def cs(half):
        inv = (np.float32(ROPE_THETA) ** (-np.arange(half, dtype=np.float32) / np.float32(half))).astype(np.float32)
        ang = (pos * inv[None, :]).astype(np.float64)
        return np.cos(ang).astype(np.float32), np.sin(ang).astype(np.float32)

    c32, s32 = cs(MLA_ROPE_DIM // 2)
    t64 = np.concatenate([c32, c32, s32, s32], axis=-1)
    c64, s64 = cs(DIL_HEAD_DIM // 2)
    cos128 = np.concatenate([c64, c64], axis=-1)
    sin128 = np.concatenate([-s64, s64], axis=-1)
    return jnp.asarray(t64), jnp.asarray(cos128), jnp.asarray(sin128)


def _residue_perm(dil):
    n = PERM_TILE // dil
    rows = np.arange(PERM_TILE)
    src = (rows % n) * dil + rows // n
    return jnp.asarray(src[:, None] == np.arange(PERM_TILE)[None, :], dtype=BF16)


def kernel(x, norm1_g, w_in, g_cq, g_ckv, w_uq, w_ukv, mla_q_norm_g, mla_k_norm_g, dil_q_norm_g,
           dil_k_norm_g, w_out, norm2_g, w_group, b_group, w_expert, b_expert, w1, w3, w2):
    B, S, D = x.shape
    assert B == 1 and D == D_MODEL and norm1_g.shape[0] == 1
    x2 = x.reshape(S, D)
    half_r = MLA_ROPE_DIM // 2
    o1, o2 = Q_LORA_RANK, Q_LORA_RANK + KV_LORA_RANK
    o3 = o2 + MLA_ROPE_DIM

    w_z = _regroup_w_in(w_in)

    wq = w_uq[0].reshape(Q_LORA_RANK, N_MLA_HEADS, MLA_QK_DIM)
    wq_pe = wq[..., MLA_NOPE_DIM:]
    wq_ext = jnp.concatenate([wq, _rot_half_cols(wq_pe, half_r)], axis=-1)
    wq_ext = wq_ext.reshape(Q_LORA_RANK, N_MLA_HEADS * MLA_PAD).astype(BF16)
    wkv = w_ukv[0].astype(BF16)

    def ext_gain(g):
        pe = g[MLA_NOPE_DIM:]
        return jnp.concatenate([g, pe[half_r:], pe[:half_r]])[None, :]

    gq_ext = ext_gain(mla_q_norm_g[0])
    gk_ext = ext_gain(mla_k_norm_g[0])
    t64, cos128, sin128 = _rope_tables(S)
    p4, p16 = _residue_perm(4), _residue_perm(16)

    z = _inproj(x2, norm1_g, w_z)
    q, k, vt, dq, dk, dq4, dk4, dv4, dq16, dk16, dv16 = _prep(
        z, g_cq, g_ckv, wq_ext, wkv, gq_ext, gk_ext[:, :LANES], gk_ext[:, LANES:],
        dil_q_norm_g, dil_k_norm_g, t64, cos128, sin128, p4, p16)
    mla_o = _mla_attn(q, k, vt)

    (w_1, d_1), (w_4, d_4), (w_16, d_16) = DIL_PATTERNS
    dil1 = _dil_pattern(dq[None], dk[None], z[None], Z_DV, w_1, d_1, hb=2, gb=4)
    dil4 = _dil_pattern(dq4, dk4, dv4, 0, w_4, d_4, hb=8, gb=1)
    dil16 = _dil_pattern(dq16, dk16, dv16, 0, w_16, d_16, hb=8, gb=1)

    pad = LANES - N_EXPERT_GROUPS - N_EXPERTS
    w_router = jnp.concatenate([w_group[0], w_expert[0], jnp.zeros((D, pad), F32)], axis=-1)
    b_router = jnp.concatenate([b_group[0], b_expert[0], jnp.zeros((pad,), F32)])[None, :]
    wr_hi = w_router.astype(BF16)
    wr_lo = (w_router - wr_hi.astype(F32)).astype(BF16)
    x1, xn, route_i, route_f = _outproj_router(
        mla_o, dil1, dil4, dil16, p4.T, p16.T, x2, w_out[0].astype(BF16), norm2_g,
        jnp.concatenate([wr_hi, wr_lo], axis=-1), wr_hi, b_router)

    item_e, item_start, item_n, order = _moe_items(route_i[:, :TOP_K], MOE_CAP)
    y = _moe_experts(xn, w1[0], w3[0], w2[0], item_e, item_start, item_n, order)
    out = _combine(x1, y, route_f)
    return out.reshape(B, S, D)
```

```python
import functools
import math

import jax
import jax.numpy as jnp
import numpy as np
from jax import lax
from jax.experimental import pallas as pl
from jax.experimental.pallas import tpu as pltpu

D_MODEL = 2048
N_MLA_HEADS = 8
MLA_NOPE_DIM = 128
MLA_ROPE_DIM = 64
MLA_QK_DIM = MLA_NOPE_DIM + MLA_ROPE_DIM
MLA_V_DIM = 128
Q_LORA_RANK = 512
KV_LORA_RANK = 512
N_DIL_HEADS = 8
DIL_HEAD_DIM = 128
DIL_PATTERNS = ((128, 1), (512, 4), (2048, 16))
ROPE_THETA = 10000.0
NORM_EPS = 1e-6
NEG_INF = -1e30
N_EXPERT_GROUPS = 8
EXPERTS_PER_GROUP = 8
N_EXPERTS = N_EXPERT_GROUPS * EXPERTS_PER_GROUP
TOP_K = 2
EXPERT_FF = 1408

LANES = 128
BF16_ROWS = 16
MXU_DIM = 256
HD = N_DIL_HEADS * DIL_HEAD_DIM
MLA_PAD = MXU_DIM
Z_DQ, Z_DK, Z_DV = 0, HD, 2 * HD
Z_CQ = 3 * HD
Z_CKV = Z_CQ + Q_LORA_RANK
Z_KR = Z_CKV + KV_LORA_RANK
Z_WIDTH = Z_KR + 2 * MLA_ROPE_DIM

ATT_VC = 512
ATT_TK = 512
ATT_TQ = 1024
ATT_UNROLL = 8
VT_ROWS = MLA_V_DIM + BF16_ROWS
PERM_TILE = 256
OUTPROJ_TM = 2 * PERM_TILE
LOG2E = math.log2(math.e)

VMEM_LIMIT = 56 * 1024 * 1024

F32 = jnp.float32
BF16 = jnp.bfloat16


def _cparams(sem, vmem=VMEM_LIMIT):
    return pltpu.CompilerParams(dimension_semantics=sem, vmem_limit_bytes=vmem)


def _dot(a, b):
    return jnp.dot(a, b, preferred_element_type=F32)


def _dot_nt(a, b):
    return lax.dot_general(a, b, (((1,), (1,)), ((), ())), preferred_element_type=F32)


REGROUP_ROWS = MLA_ROPE_DIM


REGROUP_PER_STEP = 6


def _regroup_kernel(*refs):
    o_ref = refs[-1]
    rb = REGROUP_ROWS
    half = MLA_ROPE_DIM // 2
    last_step = pl.program_id(0) == pl.num_programs(0) - 1
    for u, w_ref in enumerate(refs[:-1]):
        x = w_ref[...]
        if u == REGROUP_PER_STEP - 1:
            rot = jnp.concatenate([-x[half:], x[:half]], axis=0)
            x = jnp.where(last_step, rot, x)
        o_ref[u * rb:(u + 1) * rb, :] = x.astype(BF16)


def _regroup_w_in(w_in):
    w_t = jnp.swapaxes(w_in[0], 0, 1)
    d = w_t.shape[1]
    rb = REGROUP_ROWS
    n_tail = (Z_CQ - Z_DQ) // rb
    n_lat = (Z_KR - Z_CQ) // rb
    first_tail = n_lat + 1

    def src(u, step):
        i = step * REGROUP_PER_STEP + u
        return jnp.where(i < n_tail, i + first_tail, jnp.where(i < n_tail + n_lat, i - n_tail, n_lat)), 0

    per = REGROUP_PER_STEP
    assert Z_WIDTH % (per * rb) == 0
    return pl.pallas_call(
        _regroup_kernel,
        grid=(Z_WIDTH // (per * rb),),
        in_specs=[pl.BlockSpec((rb, d), functools.partial(src, u)) for u in range(per)],
        out_specs=pl.BlockSpec((per * rb, d), lambda s: (s, 0)),
        out_shape=jax.ShapeDtypeStruct((Z_WIDTH, d), BF16),
        compiler_params=_cparams(("parallel",)),
        name="regroup_w_in",
    )(*([w_t] * per))


def _inproj_kernel(x_ref, g_ref, w_ref, z_ref):
    x = x_ref[...]
    r = lax.rsqrt(jnp.mean(x * x, axis=-1, keepdims=True) + NORM_EPS)
    h = (x * r * g_ref[...]).astype(BF16)
    z_ref[...] = _dot_nt(h, w_ref[...]).astype(BF16)


def _inproj(x2, g1, w_z, tm=256):
    S = x2.shape[0]
    return pl.pallas_call(
        _inproj_kernel,
        grid=(S // tm,),
        in_specs=[
            pl.BlockSpec((tm, D_MODEL), lambda i: (i, 0)),
            pl.BlockSpec((1, D_MODEL), lambda i: (0, 0)),
            pl.BlockSpec((Z_WIDTH, D_MODEL), lambda i: (0, 0)),
        ],
        out_specs=pl.BlockSpec((tm, Z_WIDTH), lambda i: (i, 0)),
        out_shape=jax.ShapeDtypeStruct((S, Z_WIDTH), BF16),
        compiler_params=_cparams(("parallel",)),
        name="inproj",
    )(x2, g1, w_z)


def _prep_kernel(zdq_ref, zdk_ref, zdv_ref, zcq_ref, zckv_ref, zkr_ref,
                 gcq_ref, gckv_ref, wq_ref, wkv_ref, gq_ref, gkn_ref, gkp_ref,
                 gdq_ref, gdk_ref, t64_ref, cos_ref, sin_ref, p4_ref, p16_ref,
                 q_ref, k_ref, vt_ref, dq_ref, dk_ref,
                 dq4_ref, dk4_ref, dv4_ref, dq16_ref, dk16_ref, dv16_ref):
    tm = zcq_ref.shape[0]
    lane = lax.broadcasted_iota(jnp.int32, (1, LANES), 1)
    first_half = lane < MLA_ROPE_DIM

    def rms_rows(c, g):
        c = c.astype(F32)
        r = lax.rsqrt(jnp.mean(c * c, axis=-1, keepdims=True) + NORM_EPS)
        return (c * r * g).astype(BF16)

    cq = rms_rows(zcq_ref[...], gcq_ref[...])
    ckv = rms_rows(zckv_ref[...], gckv_ref[...])
    qe = _dot(cq, wq_ref[...])
    kv = _dot(ckv, wkv_ref[...])
    kr = zkr_ref[...].astype(F32)
    kr_ss = jnp.sum(jnp.where(first_half, kr * kr, 0.0), axis=-1, keepdims=True)
    t64 = t64_ref[...]
    q_scale = MLA_QK_DIM ** -0.5 * LOG2E

    def rope64(ext):
        t = ext * t64
        return jnp.where(first_half, t + pltpu.roll(t, MLA_ROPE_DIM, 1), 0.0)

    ones_rows = (lax.broadcasted_iota(jnp.int32, (BF16_ROWS, tm), 0) == 0).astype(BF16)
    eye = (lax.broadcasted_iota(jnp.int32, (MLA_V_DIM, MLA_V_DIM), 0)
           == lax.broadcasted_iota(jnp.int32, (MLA_V_DIM, MLA_V_DIM), 1)).astype(BF16)
    for h in range(N_MLA_HEADS):
        base = h * MLA_PAD
        qn = qe[:, base:base + LANES]
        qp = qe[:, base + LANES:base + 2 * LANES]
        ss = (jnp.sum(qn * qn, axis=-1, keepdims=True)
              + jnp.sum(jnp.where(first_half, qp * qp, 0.0), axis=-1, keepdims=True))
        r = lax.rsqrt(ss * (1.0 / MLA_QK_DIM) + NORM_EPS) * q_scale
        q_ref[:, base:base + LANES] = (qn * r * gq_ref[:, :LANES]).astype(BF16)
        q_ref[:, base + LANES:base + MLA_PAD] = rope64(qp * r * gq_ref[:, LANES:]).astype(BF16)

        kn = kv[:, base:base + LANES]
        ss = jnp.sum(kn * kn, axis=-1, keepdims=True) + kr_ss
        r = lax.rsqrt(ss * (1.0 / MLA_QK_DIM) + NORM_EPS)
        k_ref[:, base:base + LANES] = (kn * r * gkn_ref[...]).astype(BF16)
        k_ref[:, base + LANES:base + MLA_PAD] = rope64(kr * r * gkp_ref[...]).astype(BF16)
        v_h = kv[:, base + LANES:base + 2 * LANES].astype(BF16)
        vt_ref[h, :MLA_V_DIM, :] = _dot_nt(eye, v_h).astype(BF16)
        vt_ref[h, MLA_V_DIM:, :] = ones_rows

    cos = cos_ref[...]
    sin = sin_ref[...]
    d_scale = DIL_HEAD_DIM ** -0.5

    def dil_head(x, g, scale):
        x = x.astype(F32)
        r = lax.rsqrt(jnp.mean(x * x, axis=-1, keepdims=True) + NORM_EPS)
        y = x * r * g
        return ((y * cos + pltpu.roll(y, DIL_HEAD_DIM // 2, 1) * sin) * scale).astype(BF16)

    for h in range(N_DIL_HEADS):
        sl = slice(h * DIL_HEAD_DIM, (h + 1) * DIL_HEAD_DIM)
        dq_ref[:, sl] = dil_head(zdq_ref[:, sl], gdq_ref[...], d_scale)
        dk_ref[:, sl] = dil_head(zdk_ref[:, sl], gdk_ref[...], 1.0)

    for src, d4, d16 in ((dq_ref, dq4_ref, dq16_ref), (dk_ref, dk4_ref, dk16_ref), (zdv_ref, dv4_ref, dv16_ref)):
        for sub in range(tm // PERM_TILE):
            xs = src[sub * PERM_TILE:(sub + 1) * PERM_TILE, :]
            for dil, p_ref, dst in ((4, p4_ref, d4), (16, p16_ref, d16)):
                n = PERM_TILE // dil
                xp = _dot(p_ref[...], xs).astype(BF16)
                for r in range(dil):
                    dst[r, sub * n:(sub + 1) * n, :] = xp[r * n:(r + 1) * n, :]


def _prep(z, gcq, gckv, wq_ext, wkv, gq_ext, gk_nope, gk_pe, gdq, gdk, t64, cos128, sin128, p4, p16):
    S = z.shape[0]
    tm = ATT_VC
    row = lambda w, j: pl.BlockSpec((tm, w), lambda i, j=j: (i, j))
    full = lambda a: pl.BlockSpec(a.shape, lambda i: (0, 0))
    res = lambda dil: pl.BlockSpec((dil, tm // dil, HD), lambda i: (0, i, 0))
    qk_w = N_MLA_HEADS * MLA_PAD
    res_shape = lambda dil: jax.ShapeDtypeStruct((dil, S // dil, HD), BF16)
    return pl.pallas_call(
        _prep_kernel,
        grid=(S // tm,),
        in_specs=[
            row(HD, Z_DQ // HD), row(HD, Z_DK // HD), row(HD, Z_DV // HD),
            row(Q_LORA_RANK, Z_CQ // Q_LORA_RANK), row(KV_LORA_RANK, Z_CKV // KV_LORA_RANK),
            row(LANES, Z_KR // LANES),
            full(gcq), full(gckv), full(wq_ext), full(wkv), full(gq_ext), full(gk_nope), full(gk_pe),
            full(gdq), full(gdk),
            row(LANES, 0), row(LANES, 0), row(LANES, 0), full(p4), full(p16),
        ],
        out_specs=[row(qk_w, 0), row(qk_w, 0),
                   pl.BlockSpec((N_MLA_HEADS, None, VT_ROWS, tm), lambda i: (0, i, 0, 0)),
                   row(HD, 0), row(HD, 0),
                   res(4), res(4), res(4), res(16), res(16), res(16)],
        out_shape=[
            jax.ShapeDtypeStruct((S, qk_w), BF16), jax.ShapeDtypeStruct((S, qk_w), BF16),
            jax.ShapeDtypeStruct((N_MLA_HEADS, S // tm, VT_ROWS, tm), BF16),
            jax.ShapeDtypeStruct((S, HD), BF16), jax.ShapeDtypeStruct((S, HD), BF16),
            res_shape(4), res_shape(4), res_shape(4), res_shape(16), res_shape(16), res_shape(16),
        ],
        compiler_params=_cparams(("parallel",)),
        name="qkv_prep",
    )(z, z, z, z, z, z, gcq, gckv, wq_ext, wkv, gq_ext, gk_nope, gk_pe, gdq, gdk, t64, cos128, sin128, p4, p16)


def _mla_attn_kernel(q_ref, k_ref, vt_ref, o_ref, m_ref, acc_ref, s_ref):
    per_chunk = ATT_TK // ATT_VC
    n_chunks = vt_ref.shape[0] // per_chunk
    q = q_ref[...]

    def scores(c):
        off = pl.multiple_of(c * ATT_TK, ATT_TK)
        return _dot_nt(k_ref[pl.ds(off, ATT_TK), :], q)

    m_ref[...] = jnp.full(m_ref.shape, -jnp.inf, F32)
    acc_ref[...] = jnp.zeros(acc_ref.shape, F32)

    def fold(c, slot):
        s = s_ref[slot]
        m_old = m_ref[...]
        m_new = jnp.maximum(m_old, jnp.max(s, axis=0, keepdims=True))
        alpha = jnp.exp2(m_old - m_new)
        p = jnp.exp2(s - m_new).astype(BF16)
        pv = _dot(vt_ref[c * per_chunk], p[:ATT_VC])
        for u in range(1, per_chunk):
            pv += _dot(vt_ref[c * per_chunk + u], p[u * ATT_VC:(u + 1) * ATT_VC])
        acc_ref[...] = alpha * acc_ref[...] + pv
        m_ref[...] = m_new

    s_ref[0] = scores(0)

    def body(t, carry):
        for u in range(ATT_UNROLL):
            c = ATT_UNROLL * t + u
            s_ref[(u + 1) % 2] = scores(jnp.minimum(c + 1, n_chunks - 1))
            fold(c, u % 2)
        return carry

    lax.fori_loop(0, n_chunks // ATT_UNROLL, body, 0)
    acc = acc_ref[...]
    o_t = acc[:MLA_V_DIM, :] / acc[MLA_V_DIM:MLA_V_DIM + 1, :]
    o_ref[...] = o_t.T.astype(o_ref.dtype)


def _mla_attn(q, k, vt):
    S = q.shape[0]
    n_chunks = vt.shape[1]
    return pl.pallas_call(
        _mla_attn_kernel,
        grid=(N_MLA_HEADS, S // ATT_TQ),
        in_specs=[
            pl.BlockSpec((ATT_TQ, MLA_PAD), lambda h, i: (i, h)),
            pl.BlockSpec((S, MLA_PAD), lambda h, i: (0, h)),
            pl.BlockSpec((None, n_chunks, VT_ROWS, ATT_VC), lambda h, i: (h, 0, 0, 0)),
        ],
        out_specs=pl.BlockSpec((ATT_TQ, MLA_V_DIM), lambda h, i: (i, h)),
        out_shape=jax.ShapeDtypeStruct((S, N_MLA_HEADS * MLA_V_DIM), BF16),
        scratch_shapes=[pltpu.VMEM((1, ATT_TQ), F32), pltpu.VMEM((VT_ROWS, ATT_TQ), F32),
                        pltpu.VMEM((2, ATT_TK, ATT_TQ), F32)],
        compiler_params=_cparams(("parallel", "parallel")),
        name="mla_attn",
    )(q, k, vt)


DIL_QB = 128


def _dil_kernel(q_ref, k_ref, v_ref, o_ref, lse_ref, *, half_w, hb, gb):
    L = q_ref.shape[0]
    kw = DIL_QB + 2 * half_w
    g_id = pl.program_id(1)
    lane = lax.broadcasted_iota(jnp.int32, (DIL_QB, LANES), 1)

    @pl.when(g_id == 0)
    def _():
        lse_ref[...] = jnp.zeros(lse_ref.shape, F32)

    def body(step, carry):
        qs_l, ks_l, qb, kb, vb = [], [], [], [], []
        for j in range(gb):
            qs = pl.multiple_of((step * gb + j) * DIL_QB, DIL_QB)
            ks = pl.multiple_of(jnp.clip(qs - half_w, 0, L - kw), half_w)
            qs_l.append(qs)
            ks_l.append(ks)
            for h in range(hb):
                sl = slice(h * DIL_HEAD_DIM, (h + 1) * DIL_HEAD_DIM)
                qb.append(q_ref[pl.ds(qs, DIL_QB), sl])
                kb.append(k_ref[pl.ds(ks, kw), sl])
                vb.append(v_ref[pl.ds(ks, kw), sl])
        q = jnp.stack(qb)
        k = jnp.stack(kb)
        v = jnp.stack(vb)
        s = jnp.einsum("gqd,gkd->gqk", q, k, preferred_element_type=F32)
        rel = (lax.broadcasted_iota(jnp.int32, (DIL_QB, kw), 0)
               - lax.broadcasted_iota(jnp.int32, (DIL_QB, kw), 1))
        bias = []
        for j in range(gb):
            mask = jnp.abs(rel + (qs_l[j] - ks_l[j])) <= half_w
            bias += [jnp.where(mask, 0.0, NEG_INF)] * hb
        s = s + jnp.stack(bias)
        m = jnp.max(s, axis=-1, keepdims=True)
        p = jnp.exp(s - m)
        den = jnp.sum(p, axis=-1, keepdims=True)
        o = jnp.einsum("gqk,gkd->gqd", p.astype(BF16), v, preferred_element_type=F32) / den
        lse = m + jnp.log(den)
        for j in range(gb):
            tile = lse_ref[pl.ds(qs_l[j], DIL_QB), :]
            for h in range(hb):
                sl = slice(h * DIL_HEAD_DIM, (h + 1) * DIL_HEAD_DIM)
                o_ref[pl.ds(qs_l[j], DIL_QB), sl] = o[j * hb + h].astype(o_ref.dtype)
                tile = jnp.where(lane == g_id * hb + h, lse[j * hb + h], tile)
            lse_ref[pl.ds(qs_l[j], DIL_QB), :] = tile
        return carry

    lax.fori_loop(0, L // (DIL_QB * gb), body, 0)


def _dil_pattern(dq, dk, dv, v_col0, window, dil, hb, gb):
    L = dq.shape[1]
    half_w = window // (2 * dil)
    w = hb * DIL_HEAD_DIM
    spec = lambda c0: pl.BlockSpec((None, L, w), lambda r, g, c0=c0: (r, 0, c0 + g))
    return pl.pallas_call(
        functools.partial(_dil_kernel, half_w=half_w, hb=hb, gb=gb),
        grid=(dil, N_DIL_HEADS // hb),
        in_specs=[spec(0), spec(0), spec(v_col0 // w)],
        out_specs=[spec(0), pl.BlockSpec((None, L, LANES), lambda r, g: (r, 0, 0))],
        out_shape=[jax.ShapeDtypeStruct((dil, L, HD), BF16), jax.ShapeDtypeStruct((dil, L, LANES), F32)],
        compiler_params=_cparams(("parallel", "arbitrary")),
        name=f"dil_attn_d{dil}",
    )(dq, dk, dv)


def _outproj_kernel(a_ref, o1_ref, l1_ref, o4_ref, l4_ref, o16_ref, l16_ref, p4t_ref, p16t_ref,
                    x_ref, wa_ref, wb_ref, g_ref, wr_ref, wrh_ref, br_ref,
                    x1_ref, xn_ref, ri_ref, rf_ref):
    for s in range(x_ref.shape[0] // PERM_TILE):
        _outproj_tile(s, a_ref, o1_ref, l1_ref, o4_ref, l4_ref, o16_ref, l16_ref, p4t_ref, p16t_ref,
                      x_ref, wa_ref, wb_ref, g_ref, wr_ref, wrh_ref, br_ref, x1_ref, xn_ref, ri_ref, rf_ref)


def _outproj_tile(s, a_ref, o1_ref, l1_ref, o4_ref, l4_ref, o16_ref, l16_ref, p4t_ref, p16t_ref,
                  x_ref, wa_ref, wb_ref, g_ref, wr_ref, wrh_ref, br_ref, x1_ref, xn_ref, ri_ref, rf_ref):
    tm = PERM_TILE
    rows = slice(s * tm, (s + 1) * tm)

    def to_token_order(o_ref, l_ref, pt_ref):
        pt = pt_ref[...]
        n = tm // o_ref.shape[0]
        res_rows = slice(s * n, (s + 1) * n)
        o = _dot(pt, o_ref[:, res_rows, :].reshape(tm, HD))
        lse = l_ref[:, res_rows, :].reshape(tm, LANES)
        hi = lse.astype(BF16)
        rem = lse - hi.astype(F32)
        mid = rem.astype(BF16)
        lo = (rem - mid.astype(F32)).astype(BF16)
        return o, _dot(pt, hi) + _dot(pt, mid) + _dot(pt, lo)

    o1 = o1_ref[rows, :].astype(F32)
    l1 = l1_ref[rows, :]
    o4, l4 = to_token_order(o4_ref, l4_ref, p4t_ref)
    o16, l16 = to_token_order(o16_ref, l16_ref, p16t_ref)
    big = jnp.maximum(jnp.maximum(l1, l4), l16)
    e1 = jnp.exp(l1 - big)
    e4 = jnp.exp(l4 - big)
    e16 = jnp.exp(l16 - big)
    inv = 1.0 / (e1 + e4 + e16)
    w1, w4, w16 = e1 * inv, e4 * inv, e16 * inv
    slabs = []
    for h in range(N_DIL_HEADS):
        sl = slice(h * DIL_HEAD_DIM, (h + 1) * DIL_HEAD_DIM)
        slabs.append(w1[:, h:h + 1] * o1[:, sl] + w4[:, h:h + 1] * o4[:, sl] + w16[:, h:h + 1] * o16[:, sl])
    dil_o = jnp.concatenate(slabs, axis=-1).astype(BF16)

    x1 = x_ref[rows, :] + _dot(a_ref[rows, :], wa_ref[...]) + _dot(dil_o, wb_ref[...])
    x1_ref[rows, :] = x1
    r = lax.rsqrt(jnp.mean(x1 * x1, axis=-1, keepdims=True) + NORM_EPS)
    xn = x1 * r * g_ref[...]
    xn_ref[rows, :] = xn
    xh = xn.astype(BF16)
    xl = (xn - xh.astype(F32)).astype(BF16)
    two = _dot(xh, wr_ref[...])
    logits = two[:, :LANES] + two[:, LANES:] + _dot(xl, wrh_ref[...]) + br_ref[...]
    lane = lax.broadcasted_iota(jnp.int32, logits.shape, 1)
    ninf = -jnp.inf

    def first_argmax(vals, vmax):
        return jnp.min(jnp.where(vals == vmax, lane, LANES), axis=-1, keepdims=True)

    coarse = jnp.where(lane < N_EXPERT_GROUPS, logits, ninf)
    cmax = jnp.max(coarse, axis=-1, keepdims=True)
    g = first_argmax(coarse, cmax)
    p_g = 1.0 / jnp.sum(jnp.exp(coarse - cmax), axis=-1, keepdims=True)
    lo_lane = N_EXPERT_GROUPS + g * EXPERTS_PER_GROUP
    fine = jnp.where((lane >= lo_lane) & (lane < lo_lane + EXPERTS_PER_GROUP), logits, ninf)
    v1 = jnp.max(fine, axis=-1, keepdims=True)
    j1 = first_argmax(fine, v1)
    fine2 = jnp.where(lane == j1, ninf, fine)
    v2 = jnp.max(fine2, axis=-1, keepdims=True)
    j2 = first_argmax(fine2, v2)
    e2 = jnp.exp(v2 - v1)
    g1 = 1.0 / (1.0 + e2)
    g2 = e2 / (1.0 + e2)
    ri_ref[rows, :] = jnp.where(lane == 0, j1 - N_EXPERT_GROUPS, jnp.where(lane == 1, j2 - N_EXPERT_GROUPS, 0))
    rf_ref[rows, :] = jnp.where(lane == 0, p_g * g1, jnp.where(lane == 1, p_g * g2, 0.0))


def _outproj_router(mla_o, dil1, dil4, dil16, p4t, p16t, x2, w_out_bf, g2, wr_two, wr_hi, b_router):
    S = x2.shape[0]
    tm = OUTPROJ_TM
    half = N_MLA_HEADS * MLA_V_DIM
    row = lambda w: pl.BlockSpec((tm, w), lambda i: (i, 0))
    res = lambda dil, w: pl.BlockSpec((dil, tm // dil, w), lambda i: (0, i, 0))
    const = lambda a: pl.BlockSpec(a.shape, lambda i: (0, 0))
    return pl.pallas_call(
        _outproj_kernel,
        grid=(S // tm,),
        in_specs=[
            row(half),
            pl.BlockSpec((None, tm, HD), lambda i: (0, i, 0)), pl.BlockSpec((None, tm, LANES), lambda i: (0, i, 0)),
            res(4, HD), res(4, LANES), res(16, HD), res(16, LANES), const(p4t), const(p16t),
            row(D_MODEL),
            pl.BlockSpec((half, D_MODEL), lambda i: (0, 0)),
            pl.BlockSpec((HD, D_MODEL), lambda i: (1, 0)),
            const(g2), const(wr_two), const(wr_hi), const(b_router),
        ],
        out_specs=[row(D_MODEL), row(D_MODEL), row(LANES), row(LANES)],
        out_shape=[
            jax.ShapeDtypeStruct((S, D_MODEL), F32), jax.ShapeDtypeStruct((S, D_MODEL), F32),
            jax.ShapeDtypeStruct((S, LANES), jnp.int32), jax.ShapeDtypeStruct((S, LANES), F32),
        ],
        compiler_params=_cparams(("parallel",)),
        name="outproj_router",
    )(mla_o, dil1[0], dil1[1], dil4[0], dil4[1], dil16[0], dil16[1], p4t, p16t,
      x2, w_out_bf, w_out_bf, g2, wr_two, wr_hi, b_router)


MOE_CAP = 512
MOE_RH = 128
MOE_KC = 512
MOE_NK = D_MODEL // MOE_KC
MOE_FB = 768
MOE_NB = 2
MOE_FB_LAST = EXPERT_FF - (MOE_NB - 1) * MOE_FB
MOE_STEPS = MOE_NK + MOE_NB
MOE_LOOKAHEAD = 3
MOE_UP_RING = MOE_LOOKAHEAD + 1
MOE_UP_PARTS = ((0, 1024), (1024, 2048), (2048, 2 * EXPERT_FF))
MOE_DOWN_PARTS = ((0, 1024), (1024, D_MODEL))
DMA_UNROLL = 8
assert 0 < MOE_FB_LAST <= MOE_FB and MOE_FB % LANES == 0 and MOE_FB_LAST % LANES == 0


def _moe_kernel(ie_ref, is_ref, in_ref, ord_ref,
                xn_hbm, w1_hbm, w3_hbm, w2_hbm,
                y_hbm, xf_ref, xb_ref, ab_ref, h_ref, acc_ref, w1r, w3r, w2r,
                sem_in, sem_out, sem_up, sem_dn):
    i = pl.program_id(0)
    j = pl.program_id(1)
    n_items = pl.num_programs(0)
    n_tokens = xn_hbm.shape[0]
    n = in_ref[i]
    slot = 0

    def up_ring(item, kc):
        return lax.rem(item * MOE_NK + kc, MOE_UP_RING)

    def block_copies(item, step):
        e = ie_ref[item]
        if step < MOE_NK:
            ring = up_ring(item, step)
            rows = pl.ds(step * MOE_KC, MOE_KC)
            return (pltpu.make_async_copy(w1_hbm.at[e, rows, :], w1r.at[ring], sem_up.at[ring]),
                    pltpu.make_async_copy(w3_hbm.at[e, rows, :], w3r.at[ring], sem_up.at[ring]))
        blk = step - MOE_NK
        size = MOE_FB if blk < MOE_NB - 1 else MOE_FB_LAST
        return (pltpu.make_async_copy(w2_hbm.at[e, pl.ds(blk * MOE_FB, size), :],
                                      w2r.at[blk, pl.ds(0, size), :], sem_dn.at[blk]),)

    def start_block(item, step):
        for c in block_copies(item, step):
            c.start()

    for js in range(MOE_STEPS):
        @pl.when((n > 0) & (j == js))
        def _():
            if js == 0:
                @pl.when(i == 0)
                def _():
                    for s0 in range(MOE_LOOKAHEAD):
                        start_block(0, s0)
            for c in block_copies(i, js):
                c.wait()
            tgt = js + MOE_LOOKAHEAD
            if tgt < MOE_STEPS:
                start_block(i, tgt)
            else:
                nxt = jnp.minimum(i + 1, n_items - 1)

                @pl.when((i + 1 < n_items) & (in_ref[nxt] > 0))
                def _():
                    start_block(nxt, tgt - MOE_STEPS)

    def row_in(item, sl, base, u, width):
        tok = lax.shift_right_logical(ord_ref[is_ref[item] + base + u], 1)
        dst = xf_ref.at[sl, pl.ds(base, width), :].at[pl.ds(u, 1), :]
        return pltpu.make_async_copy(xn_hbm.at[pl.ds(tok, 1), :], dst, sem_in.at[sl])

    def row_out(item, base, u, width):
        a = ord_ref[is_ref[item] + base + u]
        dst = (a & 1) * n_tokens + lax.shift_right_logical(a, 1)
        src = acc_ref.at[pl.ds(base, width), :].at[pl.ds(u, 1), :]
        return pltpu.make_async_copy(src, y_hbm.at[pl.ds(dst, 1), :], sem_out)

    def for_rows(count, fn):
        groups = lax.div(count, DMA_UNROLL)

        def group(g, c):
            base = pl.multiple_of(g * DMA_UNROLL, DMA_UNROLL)
            for u in range(DMA_UNROLL):
                fn(base, u, DMA_UNROLL)
            return c
        lax.fori_loop(0, groups, group, 0)

        def single(r, c):
            fn(r, 0, 1)
            return c
        lax.fori_loop(groups * DMA_UNROLL, count, single, 0)

    def wait_rows(count, rows_desc):
        groups = lax.div(count, DMA_UNROLL)

        def group(g, c):
            rows_desc(pl.multiple_of(g * DMA_UNROLL, DMA_UNROLL), DMA_UNROLL).wait()
            return c
        lax.fori_loop(0, groups, group, 0)

        def single(r, c):
            rows_desc(r, 1).wait()
            return c
        lax.fori_loop(groups * DMA_UNROLL, count, single, 0)

    def rows_in(sl, r0, size):
        return pltpu.make_async_copy(xn_hbm.at[pl.ds(0, size), :], xf_ref.at[sl, pl.ds(r0, size), :], sem_in.at[sl])

    def rows_out(r0, size):
        return pltpu.make_async_copy(acc_ref.at[pl.ds(r0, size), :], y_hbm.at[pl.ds(0, size), :], sem_out)

    def for_row_tiles(fn):
        tiles = lax.div(n + (MOE_RH - 1), MOE_RH)
        for k in range(1, MOE_CAP // MOE_RH + 1):
            @pl.when(tiles == k)
            def _():
                fn(0, k * MOE_RH)

    @pl.when((i == 0) & (j == 0))
    def _():
        xf_ref[...] = jnp.zeros(xf_ref.shape, F32)
        for_rows(n, lambda *row: row_in(0, 0, *row).start())

    @pl.when((n > 0) & (j == 0))
    def _():
        wait_rows(n, functools.partial(rows_in, slot))

        @pl.when(i > 0)
        def _():
            wait_rows(in_ref[i - 1], rows_out)

        for t in range(MOE_CAP // MOE_RH):
            @pl.when(t * MOE_RH < n)
            def _():
                sl = pl.ds(t * MOE_RH, MOE_RH)
                for kc in range(MOE_NK):
                    xb_ref[kc, sl, :] = xf_ref[slot, sl, kc * MOE_KC:(kc + 1) * MOE_KC].astype(BF16)
                ab_ref[sl, :] = jnp.zeros((MOE_RH, 2 * EXPERT_FF), F32)

    @pl.when((n > 0) & (j == 1) & (i + 1 < n_items))
    def _():
        nxt = jnp.minimum(i + 1, n_items - 1)
        for_rows(in_ref[nxt], lambda *row: row_in(nxt, slot, *row).start())

    @pl.when((n > 0) & (j < MOE_NK))
    def _():
        kc = jnp.minimum(j, MOE_NK - 1)
        ring = up_ring(i, kc)

        def cat_cols(c0, c1):
            pieces = []
            if c0 < EXPERT_FF:
                pieces.append(w1r[ring, :, c0:min(c1, EXPERT_FF)].astype(BF16))
            if c1 > EXPERT_FF:
                pieces.append(w3r[ring, :, max(c0, EXPERT_FF) - EXPERT_FF:c1 - EXPERT_FF].astype(BF16))
            return pieces[0] if len(pieces) == 1 else jnp.concatenate(pieces, axis=-1)

        def up(off, size):
            xs = xb_ref[kc, pl.ds(off, size), :]
            for c0, c1 in MOE_UP_PARTS:
                ab_ref[pl.ds(off, size), c0:c1] += _dot(xs, cat_cols(c0, c1))
        for_row_tiles(up)

    @pl.when((n > 0) & (j == MOE_NK))
    def _():
        def act(off, size):
            a = ab_ref[pl.ds(off, size), :EXPERT_FF]
            b = ab_ref[pl.ds(off, size), EXPERT_FF:]
            half_a = 0.5 * a
            h_ref[pl.ds(off, size), :] = ((half_a + half_a * jnp.tanh(half_a)) * b).astype(BF16)
        for_row_tiles(act)

        def down(off, size):
            hs = h_ref[pl.ds(off, size), :MOE_FB]
            for c0, c1 in MOE_DOWN_PARTS:
                acc_ref[pl.ds(off, size), c0:c1] = _dot(hs, w2r[0, :, c0:c1].astype(BF16))
        for_row_tiles(down)

    @pl.when((n > 0) & (j == MOE_NK + 1))
    def _():
        def down(off, size):
            hs = h_ref[pl.ds(off, size), MOE_FB:]
            for c0, c1 in MOE_DOWN_PARTS:
                acc_ref[pl.ds(off, size), c0:c1] += _dot(hs, w2r[MOE_NB - 1, :MOE_FB_LAST, c0:c1].astype(BF16))
        for_row_tiles(down)
        for_rows(n, lambda *row: row_out(i, *row).start())
        nxt = jnp.minimum(i + 1, n_items - 1)

        @pl.when((i == n_items - 1) | (in_ref[nxt] == 0))
        def _():
            wait_rows(n, rows_out)


def _moe_experts(xn, w1, w3, w2, item_e, item_start, item_n, order):
    S = xn.shape[0]
    n_items = item_e.shape[0]

    hbm = pl.BlockSpec(memory_space=pl.ANY)
    grid_spec = pltpu.PrefetchScalarGridSpec(
        num_scalar_prefetch=4,
        grid=(n_items, MOE_STEPS),
        in_specs=[hbm, hbm, hbm, hbm],
        out_specs=hbm,
        scratch_shapes=[
            pltpu.VMEM((1, MOE_CAP, D_MODEL), F32),
            pltpu.VMEM((MOE_NK, MOE_CAP, MOE_KC), BF16),
            pltpu.VMEM((MOE_CAP, 2 * EXPERT_FF), F32),
            pltpu.VMEM((MOE_CAP, EXPERT_FF), BF16),
            pltpu.VMEM((MOE_CAP, D_MODEL), F32),
            pltpu.VMEM((MOE_UP_RING, MOE_KC, EXPERT_FF), F32),
            pltpu.VMEM((MOE_UP_RING, MOE_KC, EXPERT_FF), F32),
            pltpu.VMEM((MOE_NB, MOE_FB, D_MODEL), F32),
            pltpu.SemaphoreType.DMA((1,)),
            pltpu.SemaphoreType.DMA(()),
            pltpu.SemaphoreType.DMA((MOE_UP_RING,)),
            pltpu.SemaphoreType.DMA((MOE_NB,)),
        ],
    )
    return pl.pallas_call(
        _moe_kernel,
        grid_spec=grid_spec,
        out_shape=jax.ShapeDtypeStruct((TOP_K * S, D_MODEL), F32),
        compiler_params=_cparams(("arbitrary", "arbitrary")),
        name="moe_experts",
    )(item_e, item_start, item_n, order, xn, w1, w3, w2)


def _moe_items(eid, cap):
    A = eid.size
    flat_e = eid.reshape(A)
    order = jnp.argsort(flat_e, stable=True).astype(jnp.int32)
    experts = jnp.arange(N_EXPERTS + 1, dtype=jnp.int32)
    starts = jnp.sum((flat_e[None, :] < experts[:, None]).astype(jnp.int32), axis=1)
    counts = starts[1:] - starts[:-1]
    per_e = (counts + cap - 1) // cap
    item_end = jnp.cumsum(per_e)
    total = item_end[-1]
    n_items = N_EXPERTS + A // cap
    idx = jnp.arange(n_items, dtype=jnp.int32)
    clamped = jnp.minimum(idx, total - 1)
    first_after = jnp.sum((item_end[None, :] <= clamped[:, None]).astype(jnp.int32), axis=1)
    e = jnp.minimum(first_after, N_EXPERTS - 1)
    local = clamped - (item_end[e] - per_e[e])
    used = idx < total
    item_start = jnp.where(used, starts[e] + local * cap, 0).astype(jnp.int32)
    item_n = jnp.where(used, jnp.clip(counts[e] - local * cap, 0, cap), 0).astype(jnp.int32)
    return e, item_start, item_n, order


def _combine_kernel(x1_ref, y0_ref, y1_ref, g_ref, o_ref):
    g = g_ref[...]
    o_ref[...] = x1_ref[...] + (g[:, 0:1] * y0_ref[...] + g[:, 1:2] * y1_ref[...])


def _combine(x1, y, gates, tm=512):
    S = x1.shape[0]
    nb = S // tm
    return pl.pallas_call(
        _combine_kernel,
        grid=(nb,),
        in_specs=[
            pl.BlockSpec((tm, D_MODEL), lambda i: (i, 0)),
            pl.BlockSpec((tm, D_MODEL), lambda i: (i, 0)),
            pl.BlockSpec((tm, D_MODEL), lambda i: (nb + i, 0)),
            pl.BlockSpec((tm, LANES), lambda i: (i, 0)),
        ],
        out_specs=pl.BlockSpec((tm, D_MODEL), lambda i: (i, 0)),
        out_shape=jax.ShapeDtypeStruct((S, D_MODEL), F32),
        compiler_params=_cparams(("parallel",)),
        name="moe_combine",
    )(x1, y, y, gates)


def _rot_half_cols(w, half):
    return jnp.concatenate([-w[..., half:], w[..., :half]], axis=-1)


def _rope_tables(S):
    pos = np.arange(S, dtype=np.float32)[:, None]

    def cs(half):
        inv = (np.float32(ROPE_THETA) ** (-np.arange(half, dtype=np.float32) / np.float32(half))).astype(np.float32)
        ang = (pos * inv[None, :]).astype(np.float64)
        return np.cos(ang).astype(np.float32), np.sin(ang).astype(np.float32)

    c32, s32 = cs(MLA_ROPE_DIM // 2)
    t64 = np.concatenate([c32, c32, s32, s32], axis=-1)
    c64, s64 = cs(DIL_HEAD_DIM // 2)
    cos128 = np.concatenate([c64, c64], axis=-1)
    sin128 = np.concatenate([-s64, s64], axis=-1)
    return jnp.asarray(t64), jnp.asarray(cos128), jnp.asarray(sin128)


def _residue_perm(dil):
    n = PERM_TILE // dil
    rows = np.arange(PERM_TILE)
    src = (rows % n) * dil + rows // n
    return jnp.asarray(src[:, None] == np.arange(PERM_TILE)[None, :], dtype=BF16)


def kernel(x, norm1_g, w_in, g_cq, g_ckv, w_uq, w_ukv, mla_q_norm_g, mla_k_norm_g, dil_q_norm_g,
           dil_k_norm_g, w_out, norm2_g, w_group, b_group, w_expert, b_expert, w1, w3, w2):
    B, S, D = x.shape
    assert B == 1 and D == D_MODEL and norm1_g.shape[0] == 1
    x2 = x.reshape(S, D)
    half_r = MLA_ROPE_DIM // 2
    o1, o2 = Q_LORA_RANK, Q_LORA_RANK + KV_LORA_RANK
    o3 = o2 + MLA_ROPE_DIM

    w_z = _regroup_w_in(w_in)

    wq = w_uq[0].reshape(Q_LORA_RANK, N_MLA_HEADS, MLA_QK_DIM)
    wq_pe = wq[..., MLA_NOPE_DIM:]
    wq_ext = jnp.concatenate([wq, _rot_half_cols(wq_pe, half_r)], axis=-1)
    wq_ext = wq_ext.reshape(Q_LORA_RANK, N_MLA_HEADS * MLA_PAD).astype(BF16)
    wkv = w_ukv[0].astype(BF16)

    def ext_gain(g):
        pe = g[MLA_NOPE_DIM:]
        return jnp.concatenate([g, pe[half_r:], pe[:half_r]])[None, :]

    gq_ext = ext_gain(mla_q_norm_g[0])
    gk_ext = ext_gain(mla_k_norm_g[0])
    t64, cos128, sin128 = _rope_tables(S)
    p4, p16 = _residue_perm(4), _residue_perm(16)

    z = _inproj(x2, norm1_g, w_z)
    q, k, vt, dq, dk, dq4, dk4, dv4, dq16, dk16, dv16 = _prep(
        z, g_cq, g_ckv, wq_ext, wkv, gq_ext, gk_ext[:, :LANES], gk_ext[:, LANES:],
        dil_q_norm_g, dil_k_norm_g, t64, cos128, sin128, p4, p16)
    mla_o = _mla_attn(q, k, vt)

    (w_1, d_1), (w_4, d_4), (w_16, d_16) = DIL_PATTERNS
    dil1 = _dil_pattern(dq[None], dk[None], z[None], Z_DV, w_1, d_1, hb=2, gb=4)
    dil4 = _dil_pattern(dq4, dk4, dv4, 0, w_4, d_4, hb=8, gb=1)
    dil16 = _dil_pattern(dq16, dk16, dv16, 0, w_16, d_16, hb=8, gb=1)

    pad = LANES - N_EXPERT_GROUPS - N_EXPERTS
    w_router = jnp.concatenate([w_group[0], w_expert[0], jnp.zeros((D, pad), F32)], axis=-1)
    b_router = jnp.concatenate([b_group[0], b_expert[0], jnp.zeros((pad,), F32)])[None, :]
    wr_hi = w_router.astype(BF16)
    wr_lo = (w_router - wr_hi.astype(F32)).astype(BF16)
    x1, xn, route_i, route_f = _outproj_router(
        mla_o, dil1, dil4, dil16, p4.T, p16.T, x2, w_out[0].astype(BF16), norm2_g,
        jnp.concatenate([wr_hi, wr_lo], axis=-1), wr_hi, b_router)

    item_e, item_start, item_n, order = _moe_items(route_i[:, :TOP_K], MOE_CAP)
    y = _moe_experts(xn, w1[0], w3[0], w2[0], item_e, item_start, item_n, order)
    out = _combine(x1, y, route_f)
    return out.reshape(B, S, D)
```

```python
import functools
import math

import jax
import jax.numpy as jnp
import numpy as np
from jax import lax
from jax.experimental import pallas as pl
from jax.experimental.pallas import tpu as pltpu

D_MODEL = 2048
N_MLA_HEADS = 8
MLA_NOPE_DIM = 128
MLA_ROPE_DIM = 64
MLA_QK_DIM = MLA_NOPE_DIM + MLA_ROPE_DIM
MLA_V_DIM = 128
Q_LORA_RANK = 512
KV_LORA_RANK = 512
N_DIL_HEADS = 8
DIL_HEAD_DIM = 128
DIL_PATTERNS = ((128, 1), (512, 4), (2048, 16))
ROPE_THETA = 10000.0
NORM_EPS = 1e-6
NEG_INF = -1e30
N_EXPERT_GROUPS = 8
EXPERTS_PER_GROUP = 8
N_EXPERTS = N_EXPERT_GROUPS * EXPERTS_PER_GROUP
TOP_K = 2
EXPERT_FF = 1408

LANES = 128
BF16_ROWS = 16
MXU_DIM = 256
HD = N_DIL_HEADS * DIL_HEAD_DIM
MLA_PAD = MXU_DIM
Z_DQ, Z_DK, Z_DV = 0, HD, 2 * HD
Z_CQ = 3 * HD
Z_CKV = Z_CQ + Q_LORA_RANK
Z_KR = Z_CKV + KV_LORA_RANK
Z_WIDTH = Z_KR + 2 * MLA_ROPE_DIM

ATT_VC = 512
ATT_TK = 512
ATT_TQ = 1024
ATT_UNROLL = 8
VT_ROWS = MLA_V_DIM + BF16_ROWS
PERM_TILE = 256
LOG2E = math.log2(math.e)

VMEM_LIMIT = 56 * 1024 * 1024

F32 = jnp.float32
BF16 = jnp.bfloat16


def _cparams(sem, vmem=VMEM_LIMIT):
    return pltpu.CompilerParams(dimension_semantics=sem, vmem_limit_bytes=vmem)


def _dot(a, b):
    return jnp.dot(a, b, preferred_element_type=F32)


def _dot_nt(a, b):
    return lax.dot_general(a, b, (((1,), (1,)), ((), ())), preferred_element_type=F32)


REGROUP_ROWS = MLA_ROPE_DIM


REGROUP_PER_STEP = 6


def _regroup_kernel(*refs):
    o_ref = refs[-1]
    rb = REGROUP_ROWS
    half = MLA_ROPE_DIM // 2
    last_step = pl.program_id(0) == pl.num_programs(0) - 1
    for u, w_ref in enumerate(refs[:-1]):
        x = w_ref[...]
        if u == REGROUP_PER_STEP - 1:
            rot = jnp.concatenate([-x[half:], x[:half]], axis=0)
            x = jnp.where(last_step, rot, x)
        o_ref[u * rb:(u + 1) * rb, :] = x.astype(BF16)


def _regroup_w_in(w_in):
    w_t = jnp.swapaxes(w_in[0], 0, 1)
    d = w_t.shape[1]
    rb = REGROUP_ROWS
    n_tail = (Z_CQ - Z_DQ) // rb
    n_lat = (Z_KR - Z_CQ) // rb
    first_tail = n_lat + 1

    def src(u, step):
        i = step * REGROUP_PER_STEP + u
        return jnp.where(i < n_tail, i + first_tail, jnp.where(i < n_tail + n_lat, i - n_tail, n_lat)), 0

    per = REGROUP_PER_STEP
    assert Z_WIDTH % (per * rb) == 0
    return pl.pallas_call(
        _regroup_kernel,
        grid=(Z_WIDTH // (per * rb),),
        in_specs=[pl.BlockSpec((rb, d), functools.partial(src, u)) for u in range(per)],
        out_specs=pl.BlockSpec((per * rb, d), lambda s: (s, 0)),
        out_shape=jax.ShapeDtypeStruct((Z_WIDTH, d), BF16),
        compiler_params=_cparams(("parallel",)),
        name="regroup_w_in",
    )(*([w_t] * per))


def _inproj_kernel(x_ref, g_ref, w_ref, z_ref):
    x = x_ref[...]
    r = lax.rsqrt(jnp.mean(x * x, axis=-1, keepdims=True) + NORM_EPS)
    h = (x * r * g_ref[...]).astype(BF16)
    z_ref[...] = _dot_nt(h, w_ref[...]).astype(BF16)


def _inproj(x2, g1, w_z, tm=256):
    S = x2.shape[0]
    return pl.pallas_call(
        _inproj_kernel,
        grid=(S // tm,),
        in_specs=[
            pl.BlockSpec((tm, D_MODEL), lambda i: (i, 0)),
            pl.BlockSpec((1, D_MODEL), lambda i: (0, 0)),
            pl.BlockSpec((Z_WIDTH, D_MODEL), lambda i: (0, 0)),
        ],
        out_specs=pl.BlockSpec((tm, Z_WIDTH), lambda i: (i, 0)),
        out_shape=jax.ShapeDtypeStruct((S, Z_WIDTH), BF16),
        compiler_params=_cparams(("parallel",)),
        name="inproj",
    )(x2, g1, w_z)


def _prep_kernel(zdq_ref, zdk_ref, zdv_ref, zcq_ref, zckv_ref, zkr_ref,
                 gcq_ref, gckv_ref, wq_ref, wkv_ref, gq_ref, gkn_ref, gkp_ref,
                 gdq_ref, gdk_ref, t64_ref, cos_ref, sin_ref, p4_ref, p16_ref,
                 q_ref, k_ref, vt_ref, dq_ref, dk_ref,
                 dq4_ref, dk4_ref, dv4_ref, dq16_ref, dk16_ref, dv16_ref):
    tm = zcq_ref.shape[0]
    lane = lax.broadcasted_iota(jnp.int32, (1, LANES), 1)
    first_half = lane < MLA_ROPE_DIM

    def rms_rows(c, g):
        c = c.astype(F32)
        r = lax.rsqrt(jnp.mean(c * c, axis=-1, keepdims=True) + NORM_EPS)
        return (c * r * g).astype(BF16)

    cq = rms_rows(zcq_ref[...], gcq_ref[...])
    ckv = rms_rows(zckv_ref[...], gckv_ref[...])
    qe = _dot(cq, wq_ref[...])
    kv = _dot(ckv, wkv_ref[...])
    kr = zkr_ref[...].astype(F32)
    kr_ss = jnp.sum(jnp.where(first_half, kr * kr, 0.0), axis=-1, keepdims=True)
    t64 = t64_ref[...]
    q_scale = MLA_QK_DIM ** -0.5 * LOG2E

    def rope64(ext):
        t = ext * t64
        return jnp.where(first_half, t + pltpu.roll(t, MLA_ROPE_DIM, 1), 0.0)

    ones_rows = (lax.broadcasted_iota(jnp.int32, (BF16_ROWS, tm), 0) == 0).astype(BF16)
    eye = (lax.broadcasted_iota(jnp.int32, (MLA_V_DIM, MLA_V_DIM), 0)
           == lax.broadcasted_iota(jnp.int32, (MLA_V_DIM, MLA_V_DIM), 1)).astype(BF16)
    for h in range(N_MLA_HEADS):
        base = h * MLA_PAD
        qn = qe[:, base:base + LANES]
        qp = qe[:, base + LANES:base + 2 * LANES]
        ss = (jnp.sum(qn * qn, axis=-1, keepdims=True)
              + jnp.sum(jnp.where(first_half, qp * qp, 0.0), axis=-1, keepdims=True))
        r = lax.rsqrt(ss * (1.0 / MLA_QK_DIM) + NORM_EPS) * q_scale
        q_ref[:, base:base + LANES] = (qn * r * gq_ref[:, :LANES]).astype(BF16)
        q_ref[:, base + LANES:base + MLA_PAD] = rope64(qp * r * gq_ref[:, LANES:]).astype(BF16)

        kn = kv[:, base:base + LANES]
        ss = jnp.sum(kn * kn, axis=-1, keepdims=True) + kr_ss
        r = lax.rsqrt(ss * (1.0 / MLA_QK_DIM) + NORM_EPS)
        k_ref[:, base:base + LANES] = (kn * r * gkn_ref[...]).astype(BF16)
        k_ref[:, base + LANES:base + MLA_PAD] = rope64(kr * r * gkp_ref[...]).astype(BF16)
        v_h = kv[:, base + LANES:base + 2 * LANES].astype(BF16)
        vt_ref[h, :MLA_V_DIM, :] = _dot_nt(eye, v_h).astype(BF16)
        vt_ref[h, MLA_V_DIM:, :] = ones_rows

    cos = cos_ref[...]
    sin = sin_ref[...]
    d_scale = DIL_HEAD_DIM ** -0.5

    def dil_head(x, g, scale):
        x = x.astype(F32)
        r = lax.rsqrt(jnp.mean(x * x, axis=-1, keepdims=True) + NORM_EPS)
        y = x * r * g
        return ((y * cos + pltpu.roll(y, DIL_HEAD_DIM // 2, 1) * sin) * scale).astype(BF16)

    for h in range(N_DIL_HEADS):
        sl = slice(h * DIL_HEAD_DIM, (h + 1) * DIL_HEAD_DIM)
        dq_ref[:, sl] = dil_head(zdq_ref[:, sl], gdq_ref[...], d_scale)
        dk_ref[:, sl] = dil_head(zdk_ref[:, sl], gdk_ref[...], 1.0)

    for src, d4, d16 in ((dq_ref, dq4_ref, dq16_ref), (dk_ref, dk4_ref, dk16_ref), (zdv_ref, dv4_ref, dv16_ref)):
        for sub in range(tm // PERM_TILE):
            xs = src[sub * PERM_TILE:(sub + 1) * PERM_TILE, :]
            for dil, p_ref, dst in ((4, p4_ref, d4), (16, p16_ref, d16)):
                n = PERM_TILE // dil
                xp = _dot(p_ref[...], xs).astype(BF16)
                for r in range(dil):
                    dst[r, sub * n:(sub + 1) * n, :] = xp[r * n:(r + 1) * n, :]


def _prep(z, gcq, gckv, wq_ext, wkv, gq_ext, gk_nope, gk_pe, gdq, gdk, t64, cos128, sin128, p4, p16):
    S = z.shape[0]
    tm = ATT_VC
    row = lambda w, j: pl.BlockSpec((tm, w), lambda i, j=j: (i, j))
    full = lambda a: pl.BlockSpec(a.shape, lambda i: (0, 0))
    res = lambda dil: pl.BlockSpec((dil, tm // dil, HD), lambda i: (0, i, 0))
    qk_w = N_MLA_HEADS * MLA_PAD
    res_shape = lambda dil: jax.ShapeDtypeStruct((dil, S // dil, HD), BF16)
    return pl.pallas_call(
        _prep_kernel,
        grid=(S // tm,),
        in_specs=[
            row(HD, Z_DQ // HD), row(HD, Z_DK // HD), row(HD, Z_DV // HD),
            row(Q_LORA_RANK, Z_CQ // Q_LORA_RANK), row(KV_LORA_RANK, Z_CKV // KV_LORA_RANK),
            row(LANES, Z_KR // LANES),
            full(gcq), full(gckv), full(wq_ext), full(wkv), full(gq_ext), full(gk_nope), full(gk_pe),
            full(gdq), full(gdk),
            row(LANES, 0), row(LANES, 0), row(LANES, 0), full(p4), full(p16),
        ],
        out_specs=[row(qk_w, 0), row(qk_w, 0),
                   pl.BlockSpec((N_MLA_HEADS, None, VT_ROWS, tm), lambda i: (0, i, 0, 0)),
                   row(HD, 0), row(HD, 0),
                   res(4), res(4), res(4), res(16), res(16), res(16)],
        out_shape=[
            jax.ShapeDtypeStruct((S, qk_w), BF16), jax.ShapeDtypeStruct((S, qk_w), BF16),
            jax.ShapeDtypeStruct((N_MLA_HEADS, S // tm, VT_ROWS, tm), BF16),
            jax.ShapeDtypeStruct((S, HD), BF16), jax.ShapeDtypeStruct((S, HD), BF16),
            res_shape(4), res_shape(4), res_shape(4), res_shape(16), res_shape(16), res_shape(16),
        ],
        compiler_params=_cparams(("parallel",)),
        name="qkv_prep",
    )(z, z, z, z, z, z, gcq, gckv, wq_ext, wkv, gq_ext, gk_nope, gk_pe, gdq, gdk, t64, cos128, sin128, p4, p16)


def _mla_attn_kernel(q_ref, k_ref, vt_ref, o_ref, m_ref, acc_ref, s_ref):
    per_chunk = ATT_TK // ATT_VC
    n_chunks = vt_ref.shape[0] // per_chunk
    q = q_ref[...]

    def scores(c):
        off = pl.multiple_of(c * ATT_TK, ATT_TK)
        return _dot_nt(k_ref[pl.ds(off, ATT_TK), :], q)

    m_ref[...] = jnp.full(m_ref.shape, -jnp.inf, F32)
    acc_ref[...] = jnp.zeros(acc_ref.shape, F32)

    def fold(c, slot):
        s = s_ref[slot]
        m_old = m_ref[...]
        m_new = jnp.maximum(m_old, jnp.max(s, axis=0, keepdims=True))
        alpha = jnp.exp2(m_old - m_new)
        p = jnp.exp2(s - m_new).astype(BF16)
        pv = _dot(vt_ref[c * per_chunk], p[:ATT_VC])
        for u in range(1, per_chunk):
            pv += _dot(vt_ref[c * per_chunk + u], p[u * ATT_VC:(u + 1) * ATT_VC])
        acc_ref[...] = alpha * acc_ref[...] + pv
        m_ref[...] = m_new

    s_ref[0] = scores(0)

    def body(t, carry):
        for u in range(ATT_UNROLL):
            c = ATT_UNROLL * t + u
            s_ref[(u + 1) % 2] = scores(jnp.minimum(c + 1, n_chunks - 1))
            fold(c, u % 2)
        return carry

    lax.fori_loop(0, n_chunks // ATT_UNROLL, body, 0)
    acc = acc_ref[...]
    o_t = acc[:MLA_V_DIM, :] / acc[MLA_V_DIM:MLA_V_DIM + 1, :]
    o_ref[...] = o_t.T.astype(o_ref.dtype)


def _mla_attn(q, k, vt):
    S = q.shape[0]
    n_chunks = vt.shape[1]
    return pl.pallas_call(
        _mla_attn_kernel,
        grid=(N_MLA_HEADS, S // ATT_TQ),
        in_specs=[
            pl.BlockSpec((ATT_TQ, MLA_PAD), lambda h, i: (i, h)),
            pl.BlockSpec((S, MLA_PAD), lambda h, i: (0, h)),
            pl.BlockSpec((None, n_chunks, VT_ROWS, ATT_VC), lambda h, i: (h, 0, 0, 0)),
        ],
        out_specs=pl.BlockSpec((ATT_TQ, MLA_V_DIM), lambda h, i: (i, h)),
        out_shape=jax.ShapeDtypeStruct((S, N_MLA_HEADS * MLA_V_DIM), BF16),
        scratch_shapes=[pltpu.VMEM((1, ATT_TQ), F32), pltpu.VMEM((VT_ROWS, ATT_TQ), F32),
                        pltpu.VMEM((2, ATT_TK, ATT_TQ), F32)],
        compiler_params=_cparams(("parallel", "parallel")),
        name="mla_attn",
    )(q, k, vt)


DIL_QB = 128


def _dil_kernel(q_ref, k_ref, v_ref, o_ref, lse_ref, *, half_w, hb, gb):
    L = q_ref.shape[0]
    kw = DIL_QB + 2 * half_w
    g_id = pl.program_id(1)
    lane = lax.broadcasted_iota(jnp.int32, (DIL_QB, LANES), 1)

    @pl.when(g_id == 0)
    def _():
        lse_ref[...] = jnp.zeros(lse_ref.shape, F32)

    def body(step, carry):
        qs_l, ks_l, qb, kb, vb = [], [], [], [], []
        for j in range(gb):
            qs = pl.multiple_of((step * gb + j) * DIL_QB, DIL_QB)
            ks = pl.multiple_of(jnp.clip(qs - half_w, 0, L - kw), half_w)
            qs_l.append(qs)
            ks_l.append(ks)
            for h in range(hb):
                sl = slice(h * DIL_HEAD_DIM, (h + 1) * DIL_HEAD_DIM)
                qb.append(q_ref[pl.ds(qs, DIL_QB), sl])
                kb.append(k_ref[pl.ds(ks, kw), sl])
                vb.append(v_ref[pl.ds(ks, kw), sl])
        q = jnp.stack(qb)
        k = jnp.stack(kb)
        v = jnp.stack(vb)
        s = jnp.einsum("gqd,gkd->gqk", q, k, preferred_element_type=F32)
        rel = (lax.broadcasted_iota(jnp.int32, (DIL_QB, kw), 0)
               - lax.broadcasted_iota(jnp.int32, (DIL_QB, kw), 1))
        bias = []
        for j in range(gb):
            mask = jnp.abs(rel + (qs_l[j] - ks_l[j])) <= half_w
            bias += [jnp.where(mask, 0.0, NEG_INF)] * hb
        s = s + jnp.stack(bias)
        m = jnp.max(s, axis=-1, keepdims=True)
        p = jnp.exp(s - m)
        den = jnp.sum(p, axis=-1, keepdims=True)
        o = jnp.einsum("gqk,gkd->gqd", p.astype(BF16), v, preferred_element_type=F32) / den
        lse = m + jnp.log(den)
        for j in range(gb):
            tile = lse_ref[pl.ds(qs_l[j], DIL_QB), :]
            for h in range(hb):
                sl = slice(h * DIL_HEAD_DIM, (h + 1) * DIL_HEAD_DIM)
                o_ref[pl.ds(qs_l[j], DIL_QB), sl] = o[j * hb + h].astype(o_ref.dtype)
                tile = jnp.where(lane == g_id * hb + h, lse[j * hb + h], tile)
            lse_ref[pl.ds(qs_l[j], DIL_QB), :] = tile
        return carry

    lax.fori_loop(0, L // (DIL_QB * gb), body, 0)


def _dil_pattern(dq, dk, dv, v_col0, window, dil, hb, gb):
    L = dq.shape[1]
    half_w = window // (2 * dil)
    w = hb * DIL_HEAD_DIM
    spec = lambda c0: pl.BlockSpec((None, L, w), lambda r, g, c0=c0: (r, 0, c0 + g))
    return pl.pallas_call(
        functools.partial(_dil_kernel, half_w=half_w, hb=hb, gb=gb),
        grid=(dil, N_DIL_HEADS // hb),
        in_specs=[spec(0), spec(0), spec(v_col0 // w)],
        out_specs=[spec(0), pl.BlockSpec((None, L, LANES), lambda r, g: (r, 0, 0))],
        out_shape=[jax.ShapeDtypeStruct((dil, L, HD), BF16), jax.ShapeDtypeStruct((dil, L, LANES), F32)],
        compiler_params=_cparams(("parallel", "arbitrary")),
        name=f"dil_attn_d{dil}",
    )(dq, dk, dv)


def _outproj_kernel(a_ref, o1_ref, l1_ref, o4_ref, l4_ref, o16_ref, l16_ref, p4t_ref, p16t_ref,
                    x_ref, wa_ref, wb_ref, g_ref, wr_ref, wrh_ref, br_ref,
                    x1_ref, xn_ref, ri_ref, rf_ref):
    tm = x_ref.shape[0]

    def to_token_order(o_ref, l_ref, pt_ref):
        pt = pt_ref[...]
        o = _dot(pt, o_ref[...].reshape(tm, HD))
        lse = l_ref[...].reshape(tm, LANES)
        hi = lse.astype(BF16)
        rem = lse - hi.astype(F32)
        mid = rem.astype(BF16)
        lo = (rem - mid.astype(F32)).astype(BF16)
        return o, _dot(pt, hi) + _dot(pt, mid) + _dot(pt, lo)

    o1 = o1_ref[...].astype(F32)
    l1 = l1_ref[...]
    o4, l4 = to_token_order(o4_ref, l4_ref, p4t_ref)
    o16, l16 = to_token_order(o16_ref, l16_ref, p16t_ref)
    big = jnp.maximum(jnp.maximum(l1, l4), l16)
    e1 = jnp.exp(l1 - big)
    e4 = jnp.exp(l4 - big)
    e16 = jnp.exp(l16 - big)
    inv = 1.0 / (e1 + e4 + e16)
    w1, w4, w16 = e1 * inv, e4 * inv, e16 * inv
    slabs = []
    for h in range(N_DIL_HEADS):
        sl = slice(h * DIL_HEAD_DIM, (h + 1) * DIL_HEAD_DIM)
        slabs.append(w1[:, h:h + 1] * o1[:, sl] + w4[:, h:h + 1] * o4[:, sl] + w16[:, h:h + 1] * o16[:, sl])
    dil_o = jnp.concatenate(slabs, axis=-1).astype(BF16)

    x1 = x_ref[...] + _dot(a_ref[...], wa_ref[...]) + _dot(dil_o, wb_ref[...])
    x1_ref[...] = x1
    r = lax.rsqrt(jnp.mean(x1 * x1, axis=-1, keepdims=True) + NORM_EPS)
    xn = x1 * r * g_ref[...]
    xn_ref[...] = xn
    xh = xn.astype(BF16)
    xl = (xn - xh.astype(F32)).astype(BF16)
    two = _dot(xh, wr_ref[...])
    logits = two[:, :LANES] + two[:, LANES:] + _dot(xl, wrh_ref[...]) + br_ref[...]
    lane = lax.broadcasted_iota(jnp.int32, logits.shape, 1)
    ninf = -jnp.inf

    def first_argmax(vals, vmax):
        return jnp.min(jnp.where(vals == vmax, lane, LANES), axis=-1, keepdims=True)

    coarse = jnp.where(lane < N_EXPERT_GROUPS, logits, ninf)
    cmax = jnp.max(coarse, axis=-1, keepdims=True)
    g = first_argmax(coarse, cmax)
    p_g = 1.0 / jnp.sum(jnp.exp(coarse - cmax), axis=-1, keepdims=True)
    lo_lane = N_EXPERT_GROUPS + g * EXPERTS_PER_GROUP
    fine = jnp.where((lane >= lo_lane) & (lane < lo_lane + EXPERTS_PER_GROUP), logits, ninf)
    v1 = jnp.max(fine, axis=-1, keepdims=True)
    j1 = first_argmax(fine, v1)
    fine2 = jnp.where(lane == j1, ninf, fine)
    v2 = jnp.max(fine2, axis=-1, keepdims=True)
    j2 = first_argmax(fine2, v2)
    e2 = jnp.exp(v2 - v1)
    g1 = 1.0 / (1.0 + e2)
    g2 = e2 / (1.0 + e2)
    ri_ref[...] = jnp.where(lane == 0, j1 - N_EXPERT_GROUPS, jnp.where(lane == 1, j2 - N_EXPERT_GROUPS, 0))
    rf_ref[...] = jnp.where(lane == 0, p_g * g1, jnp.where(lane == 1, p_g * g2, 0.0))


def _outproj_router(mla_o, dil1, dil4, dil16, p4t, p16t, x2, w_out_bf, g2, wr_two, wr_hi, b_router):
    S = x2.shape[0]
    tm = PERM_TILE
    half = N_MLA_HEADS * MLA_V_DIM
    row = lambda w: pl.BlockSpec((tm, w), lambda i: (i, 0))
    res = lambda dil, w: pl.BlockSpec((dil, tm // dil, w), lambda i: (0, i, 0))
    const = lambda a: pl.BlockSpec(a.shape, lambda i: (0, 0))
    return pl.pallas_call(
        _outproj_kernel,
        grid=(S // tm,),
        in_specs=[
            row(half),
            pl.BlockSpec((None, tm, HD), lambda i: (0, i, 0)), pl.BlockSpec((None, tm, LANES), lambda i: (0, i, 0)),
            res(4, HD), res(4, LANES), res(16, HD), res(16, LANES), const(p4t), const(p16t),
            row(D_MODEL),
            pl.BlockSpec((half, D_MODEL), lambda i: (0, 0)),
            pl.BlockSpec((HD, D_MODEL), lambda i: (1, 0)),
            const(g2), const(wr_two), const(wr_hi), const(b_router),
        ],
        out_specs=[row(D_MODEL), row(D_MODEL), row(LANES), row(LANES)],
        out_shape=[
            jax.ShapeDtypeStruct((S, D_MODEL), F32), jax.ShapeDtypeStruct((S, D_MODEL), F32),
            jax.ShapeDtypeStruct((S, LANES), jnp.int32), jax.ShapeDtypeStruct((S, LANES), F32),
        ],
        compiler_params=_cparams(("parallel",)),
        name="outproj_router",
    )(mla_o, dil1[0], dil1[1], dil4[0], dil4[1], dil16[0], dil16[1], p4t, p16t,
      x2, w_out_bf, w_out_bf, g2, wr_two, wr_hi, b_router)


MOE_CAP = 512
MOE_RH = 128
MOE_KC = 512
MOE_NK = D_MODEL // MOE_KC
MOE_FB = 768
MOE_NB = 2
MOE_FB_LAST = EXPERT_FF - (MOE_NB - 1) * MOE_FB
MOE_STEPS = MOE_NK + MOE_NB
MOE_LOOKAHEAD = 3
MOE_UP_RING = MOE_LOOKAHEAD + 1
MOE_UP_PARTS = ((0, 1024), (1024, 2048), (2048, 2 * EXPERT_FF))
MOE_DOWN_PARTS = ((0, 1024), (1024, D_MODEL))
DMA_UNROLL = 8
assert 0 < MOE_FB_LAST <= MOE_FB and MOE_FB % LANES == 0 and MOE_FB_LAST % LANES == 0


def _moe_kernel(ie_ref, is_ref, in_ref, ord_ref,
                xn_hbm, w1_hbm, w3_hbm, w2_hbm,
                y_hbm, xf_ref, xb_ref, ab_ref, h_ref, acc_ref, w1r, w3r, w2r,
                sem_in, sem_out, sem_up, sem_dn):
    i = pl.program_id(0)
    j = pl.program_id(1)
    n_items = pl.num_programs(0)
    n_tokens = xn_hbm.shape[0]
    n = in_ref[i]
    slot = 0

    def up_ring(item, kc):
        return lax.rem(item * MOE_NK + kc, MOE_UP_RING)

    def block_copies(item, step):
        e = ie_ref[item]
        if step < MOE_NK:
            ring = up_ring(item, step)
            rows = pl.ds(step * MOE_KC, MOE_KC)
            return (pltpu.make_async_copy(w1_hbm.at[e, rows, :], w1r.at[ring], sem_up.at[ring]),
                    pltpu.make_async_copy(w3_hbm.at[e, rows, :], w3r.at[ring], sem_up.at[ring]))
        blk = step - MOE_NK
        size = MOE_FB if blk < MOE_NB - 1 else MOE_FB_LAST
        return (pltpu.make_async_copy(w2_hbm.at[e, pl.ds(blk * MOE_FB, size), :],
                                      w2r.at[blk, pl.ds(0, size), :], sem_dn.at[blk]),)

    def start_block(item, step):
        for c in block_copies(item, step):
            c.start(priority=1)

    for js in range(MOE_STEPS):
        @pl.when((n > 0) & (j == js))
        def _():
            if js == 0:
                @pl.when(i == 0)
                def _():
                    for s0 in range(MOE_LOOKAHEAD):
                        start_block(0, s0)
            for c in block_copies(i, js):
                c.wait()
            tgt = js + MOE_LOOKAHEAD
            if tgt < MOE_STEPS:
                start_block(i, tgt)
            else:
                nxt = jnp.minimum(i + 1, n_items - 1)

                @pl.when((i + 1 < n_items) & (in_ref[nxt] > 0))
                def _():
                    start_block(nxt, tgt - MOE_STEPS)

    def row_in(item, sl, base, u, width):
        tok = lax.shift_right_logical(ord_ref[is_ref[item] + base + u], 1)
        dst = xf_ref.at[sl, pl.ds(base, width), :].at[pl.ds(u, 1), :]
        return pltpu.make_async_copy(xn_hbm.at[pl.ds(tok, 1), :], dst, sem_in.at[sl])

    def row_out(item, base, u, width):
        a = ord_ref[is_ref[item] + base + u]
        dst = (a & 1) * n_tokens + lax.shift_right_logical(a, 1)
        src = acc_ref.at[pl.ds(base, width), :].at[pl.ds(u, 1), :]
        return pltpu.make_async_copy(src, y_hbm.at[pl.ds(dst, 1), :], sem_out)

    def for_rows(count, fn):
        groups = lax.div(count, DMA_UNROLL)

        def group(g, c):
            base = pl.multiple_of(g * DMA_UNROLL, DMA_UNROLL)
            for u in range(DMA_UNROLL):
                fn(base, u, DMA_UNROLL)
            return c
        lax.fori_loop(0, groups, group, 0)

        def single(r, c):
            fn(r, 0, 1)
            return c
        lax.fori_loop(groups * DMA_UNROLL, count, single, 0)

    def wait_rows(count, rows_desc):
        groups = lax.div(count, DMA_UNROLL)

        def group(g, c):
            rows_desc(pl.multiple_of(g * DMA_UNROLL, DMA_UNROLL), DMA_UNROLL).wait()
            return c
        lax.fori_loop(0, groups, group, 0)

        def single(r, c):
            rows_desc(r, 1).wait()
            return c
        lax.fori_loop(groups * DMA_UNROLL, count, single, 0)

    def rows_in(sl, r0, size):
        return pltpu.make_async_copy(xn_hbm.at[pl.ds(0, size), :], xf_ref.at[sl, pl.ds(r0, size), :], sem_in.at[sl])

    def rows_out(r0, size):
        return pltpu.make_async_copy(acc_ref.at[pl.ds(r0, size), :], y_hbm.at[pl.ds(0, size), :], sem_out)

    def for_row_tiles(fn):
        tiles = lax.div(n + (MOE_RH - 1), MOE_RH)
        for k in range(1, MOE_CAP // MOE_RH + 1):
            @pl.when(tiles == k)
            def _():
                fn(0, k * MOE_RH)

    @pl.when((i == 0) & (j == 0))
    def _():
        xf_ref[...] = jnp.zeros(xf_ref.shape, F32)
        for_rows(n, lambda *row: row_in(0, 0, *row).start())

    @pl.when((n > 0) & (j == 0))
    def _():
        wait_rows(n, functools.partial(rows_in, slot))

        @pl.when(i > 0)
        def _():
            wait_rows(in_ref[i - 1], rows_out)

        for t in range(MOE_CAP // MOE_RH):
            @pl.when(t * MOE_RH < n)
            def _():
                sl = pl.ds(t * MOE_RH, MOE_RH)
                for kc in range(MOE_NK):
                    xb_ref[kc, sl, :] = xf_ref[slot, sl, kc * MOE_KC:(kc + 1) * MOE_KC].astype(BF16)
                ab_ref[sl, :] = jnp.zeros((MOE_RH, 2 * EXPERT_FF), F32)

    @pl.when((n > 0) & (j == 1) & (i + 1 < n_items))
    def _():
        nxt = jnp.minimum(i + 1, n_items - 1)
        for_rows(in_ref[nxt], lambda *row: row_in(nxt, slot, *row).start())

    @pl.when((n > 0) & (j < MOE_NK))
    def _():
        kc = jnp.minimum(j, MOE_NK - 1)
        ring = up_ring(i, kc)

        def cat_cols(c0, c1):
            pieces = []
            if c0 < EXPERT_FF:
                pieces.append(w1r[ring, :, c0:min(c1, EXPERT_FF)].astype(BF16))
            if c1 > EXPERT_FF:
                pieces.append(w3r[ring, :, max(c0, EXPERT_FF) - EXPERT_FF:c1 - EXPERT_FF].astype(BF16))
            return pieces[0] if len(pieces) == 1 else jnp.concatenate(pieces, axis=-1)

        def up(off, size):
            xs = xb_ref[kc, pl.ds(off, size), :]
            for c0, c1 in MOE_UP_PARTS:
                ab_ref[pl.ds(off, size), c0:c1] += _dot(xs, cat_cols(c0, c1))
        for_row_tiles(up)

    @pl.when((n > 0) & (j == MOE_NK))
    def _():
        def act(off, size):
            a = ab_ref[pl.ds(off, size), :EXPERT_FF]
            b = ab_ref[pl.ds(off, size), EXPERT_FF:]
            half_a = 0.5 * a
            h_ref[pl.ds(off, size), :] = ((half_a + half_a * jnp.tanh(half_a)) * b).astype(BF16)
        for_row_tiles(act)

        def down(off, size):
            hs = h_ref[pl.ds(off, size), :MOE_FB]
            for c0, c1 in MOE_DOWN_PARTS:
                acc_ref[pl.ds(off, size), c0:c1] = _dot(hs, w2r[0, :, c0:c1].astype(BF16))
        for_row_tiles(down)

    @pl.when((n > 0) & (j == MOE_NK + 1))
    def _():
        def down(off, size):
            hs = h_ref[pl.ds(off, size), MOE_FB:]
            for c0, c1 in MOE_DOWN_PARTS:
                acc_ref[pl.ds(off, size), c0:c1] += _dot(hs, w2r[MOE_NB - 1, :MOE_FB_LAST, c0:c1].astype(BF16))
        for_row_tiles(down)
        for_rows(n, lambda *row: row_out(i, *row).start(priority=row[1] % 2))
        nxt = jnp.minimum(i + 1, n_items - 1)

        @pl.when((i == n_items - 1) | (in_ref[nxt] == 0))
        def _():
            wait_rows(n, rows_out)


def _moe_experts(xn, w1, w3, w2, item_e, item_start, item_n, order):
    S = xn.shape[0]
    n_items = item_e.shape[0]

    hbm = pl.BlockSpec(memory_space=pl.ANY)
    grid_spec = pltpu.PrefetchScalarGridSpec(
        num_scalar_prefetch=4,
        grid=(n_items, MOE_STEPS),
        in_specs=[hbm, hbm, hbm, hbm],
        out_specs=hbm,
        scratch_shapes=[
            pltpu.VMEM((1, MOE_CAP, D_MODEL), F32),
            pltpu.VMEM((MOE_NK, MOE_CAP, MOE_KC), BF16),
            pltpu.VMEM((MOE_CAP, 2 * EXPERT_FF), F32),
            pltpu.VMEM((MOE_CAP, EXPERT_FF), BF16),
            pltpu.VMEM((MOE_CAP, D_MODEL), F32),
            pltpu.VMEM((MOE_UP_RING, MOE_KC, EXPERT_FF), F32),
            pltpu.VMEM((MOE_UP_RING, MOE_KC, EXPERT_FF), F32),
            pltpu.VMEM((MOE_NB, MOE_FB, D_MODEL), F32),
            pltpu.SemaphoreType.DMA((1,)),
            pltpu.SemaphoreType.DMA(()),
            pltpu.SemaphoreType.DMA((MOE_UP_RING,)),
            pltpu.SemaphoreType.DMA((MOE_NB,)),
        ],
    )
    return pl.pallas_call(
        _moe_kernel,
        grid_spec=grid_spec,
        out_shape=jax.ShapeDtypeStruct((TOP_K * S, D_MODEL), F32),
        compiler_params=_cparams(("arbitrary", "arbitrary")),
        name="moe_experts",
    )(item_e, item_start, item_n, order, xn, w1, w3, w2)


def _moe_items(eid, cap):
    A = eid.size
    flat_e = eid.reshape(A)
    order = jnp.argsort(flat_e, stable=True).astype(jnp.int32)
    experts = jnp.arange(N_EXPERTS + 1, dtype=jnp.int32)
    starts = jnp.sum((flat_e[None, :] < experts[:, None]).astype(jnp.int32), axis=1)
    counts = starts[1:] - starts[:-1]
    per_e = (counts + cap - 1) // cap
    item_end = jnp.cumsum(per_e)
    total = item_end[-1]
    n_items = N_EXPERTS + A // cap
    idx = jnp.arange(n_items, dtype=jnp.int32)
    clamped = jnp.minimum(idx, total - 1)
    first_after = jnp.sum((item_end[None, :] <= clamped[:, None]).astype(jnp.int32), axis=1)
    e = jnp.minimum(first_after, N_EXPERTS - 1)
    local = clamped - (item_end[e] - per_e[e])
    used = idx < total
    item_start = jnp.where(used, starts[e] + local * cap, 0).astype(jnp.int32)
    item_n = jnp.where(used, jnp.clip(counts[e] - local * cap, 0, cap), 0).astype(jnp.int32)
    return e, item_start, item_n, order


def _combine_kernel(x1_ref, y0_ref, y1_ref, g_ref, o_ref):
    g = g_ref[...]
    o_ref[...] = x1_ref[...] + (g[:, 0:1] * y0_ref[...] + g[:, 1:2] * y1_ref[...])


def _combine(x1, y, gates, tm=512):
    S = x1.shape[0]
    nb = S // tm
    return pl.pallas_call(
        _combine_kernel,
        grid=(nb,),
        in_specs=[
            pl.BlockSpec((tm, D_MODEL), lambda i: (i, 0)),
            pl.BlockSpec((tm, D_MODEL), lambda i: (i, 0)),
            pl.BlockSpec((tm, D_MODEL), lambda i: (nb + i, 0)),
            pl.BlockSpec((tm, LANES), lambda i: (i, 0)),
        ],
        out_specs=pl.BlockSpec((tm, D_MODEL), lambda i: (i, 0)),
        out_shape=jax.ShapeDtypeStruct((S, D_MODEL), F32),
        compiler_params=_cparams(("parallel",)),
        name="moe_combine",
    )(x1, y, y, gates)


def _rot_half_cols(w, half):
    return jnp.concatenate([-w[..., half:], w[..., :half]], axis=-1)


def _rope_tables(S):
    pos = np.arange(S, dtype=np.float32)[:, None]

    def cs(half):
        inv = (np.float32(ROPE_THETA) ** (-np.arange(half, dtype=np.float32) / np.float32(half))).astype(np.float32)
        ang = (pos * inv[None, :]).astype(np.float64)
        return np.cos(ang).astype(np.float32), np.sin(ang).astype(np.float32)

    c32, s32 = cs(MLA_ROPE_DIM // 2)
    t64 = np.concatenate([c32, c32, s32, s32], axis=-1)
    c64, s64 = cs(DIL_HEAD_DIM // 2)
    cos128 = np.concatenate([c64, c64], axis=-1)
    sin128 = np.concatenate([-s64, s64], axis=-1)
    return jnp.asarray(t64), jnp.asarray(cos128), jnp.asarray(sin128)


def _residue_perm(dil):
    n = PERM_TILE // dil
    rows = np.arange(PERM_TILE)
    src = (rows % n) * dil + rows // n
    return jnp.asarray(src[:, None] == np.arange(PERM_TILE)[None, :], dtype=BF16)


def kernel(x, norm1_g, w_in, g_cq, g_ckv, w_uq, w_ukv, mla_q_norm_g, mla_k_norm_g, dil_q_norm_g,
           dil_k_norm_g, w_out, norm2_g, w_group, b_group, w_expert, b_expert, w1, w3, w2):
    B, S, D = x.shape
    assert B == 1 and D == D_MODEL and norm1_g.shape[0] == 1
    x2 = x.reshape(S, D)
    half_r = MLA_ROPE_DIM // 2
    o1, o2 = Q_LORA_RANK, Q_LORA_RANK + KV_LORA_RANK
    o3 = o2 + MLA_ROPE_DIM

    w_z = _regroup_w_in(w_in)

    wq = w_uq[0].reshape(Q_LORA_RANK, N_MLA_HEADS, MLA_QK_DIM)
    wq_pe = wq[..., MLA_NOPE_DIM:]
    wq_ext = jnp.concatenate([wq, _rot_half_cols(wq_pe, half_r)], axis=-1)
    wq_ext = wq_ext.reshape(Q_LORA_RANK, N_MLA_HEADS * MLA_PAD).astype(BF16)
    wkv = w_ukv[0].astype(BF16)

    def ext_gain(g):
        pe = g[MLA_NOPE_DIM:]
        return jnp.concatenate([g, pe[half_r:], pe[:half_r]])[None, :]

    gq_ext = ext_gain(mla_q_norm_g[0])
    gk_ext = ext_gain(mla_k_norm_g[0])
    t64, cos128, sin128 = _rope_tables(S)
    p4, p16 = _residue_perm(4), _residue_perm(16)

    z = _inproj(x2, norm1_g, w_z)
    q, k, vt, dq, dk, dq4, dk4, dv4, dq16, dk16, dv16 = _prep(
        z, g_cq, g_ckv, wq_ext, wkv, gq_ext, gk_ext[:, :LANES], gk_ext[:, LANES:],
        dil_q_norm_g, dil_k_norm_g, t64, cos128, sin128, p4, p16)
    mla_o = _mla_attn(q, k, vt)

    (w_1, d_1), (w_4, d_4), (w_16, d_16) = DIL_PATTERNS
    dil1 = _dil_pattern(dq[None], dk[None], z[None], Z_DV, w_1, d_1, hb=2, gb=4)
    dil4 = _dil_pattern(dq4, dk4, dv4, 0, w_4, d_4, hb=8, gb=1)
    dil16 = _dil_pattern(dq16, dk16, dv16, 0, w_16, d_16, hb=8, gb=1)

    pad = LANES - N_EXPERT_GROUPS - N_EXPERTS
    w_router = jnp.concatenate([w_group[0], w_expert[0], jnp.zeros((D, pad), F32)], axis=-1)
    b_router = jnp.concatenate([b_group[0], b_expert[0], jnp.zeros((pad,), F32)])[None, :]
    wr_hi = w_router.astype(BF16)
    wr_lo = (w_router - wr_hi.astype(F32)).astype(BF16)
    x1, xn, route_i, route_f = _outproj_router(
        mla_o, dil1, dil4, dil16, p4.T, p16.T, x2, w_out[0].astype(BF16), norm2_g,
        jnp.concatenate([wr_hi, wr_lo], axis=-1), wr_hi, b_router)

    item_e, item_start, item_n, order = _moe_items(route_i[:, :TOP_K], MOE_CAP)
    y = _moe_experts(xn, w1[0], w3[0], w2[0], item_e, item_start, item_n, order)
    out = _combine(x1, y, route_f)
    return out.reshape(B, S, D)
```

```python
import functools
import math

import jax
import jax.numpy as jnp
import numpy as np
from jax import lax
from jax.experimental import pallas as pl
from jax.experimental.pallas import tpu as pltpu

D_MODEL = 2048
N_MLA_HEADS = 8
MLA_NOPE_DIM = 128
MLA_ROPE_DIM = 64
MLA_QK_DIM = MLA_NOPE_DIM + MLA_ROPE_DIM
MLA_V_DIM = 128
Q_LORA_RANK = 512
KV_LORA_RANK = 512
N_DIL_HEADS = 8
DIL_HEAD_DIM = 128
DIL_PATTERNS = ((128, 1), (512, 4), (2048, 16))
ROPE_THETA = 10000.0
NORM_EPS = 1e-6
NEG_INF = -1e30
N_EXPERT_GROUPS = 8
EXPERTS_PER_GROUP = 8
N_EXPERTS = N_EXPERT_GROUPS * EXPERTS_PER_GROUP
TOP_K = 2
EXPERT_FF = 1408

LANES = 128
BF16_ROWS = 16
MXU_DIM = 256
HD = N_DIL_HEADS * DIL_HEAD_DIM
MLA_PAD = MXU_DIM
Z_DQ, Z_DK, Z_DV = 0, HD, 2 * HD
Z_CQ = 3 * HD
Z_CKV = Z_CQ + Q_LORA_RANK
Z_KR = Z_CKV + KV_LORA_RANK
Z_WIDTH = Z_KR + 2 * MLA_ROPE_DIM

ATT_VC = 512
ATT_TK = 512
ATT_TQ = 1024
ATT_UNROLL = 8
VT_ROWS = MLA_V_DIM + BF16_ROWS
PERM_TILE = 256
LOG2E = math.log2(math.e)

VMEM_LIMIT = 56 * 1024 * 1024

F32 = jnp.float32
BF16 = jnp.bfloat16


def _cparams(sem, vmem=VMEM_LIMIT):
    return pltpu.CompilerParams(dimension_semantics=sem, vmem_limit_bytes=vmem)


def _dot(a, b):
    return jnp.dot(a, b, preferred_element_type=F32)


def _dot_nt(a, b):
    return lax.dot_general(a, b, (((1,), (1,)), ((), ())), preferred_element_type=F32)


REGROUP_ROWS = MLA_ROPE_DIM


REGROUP_PER_STEP = 6


def _regroup_kernel(*refs):
    o_ref = refs[-1]
    rb = REGROUP_ROWS
    half = MLA_ROPE_DIM // 2
    last_step = pl.program_id(0) == pl.num_programs(0) - 1
    for u, w_ref in enumerate(refs[:-1]):
        x = w_ref[...]
        if u == REGROUP_PER_STEP - 1:
            rot = jnp.concatenate([-x[half:], x[:half]], axis=0)
            x = jnp.where(last_step, rot, x)
        o_ref[u * rb:(u + 1) * rb, :] = x.astype(BF16)


def _regroup_w_in(w_in):
    w_t = jnp.swapaxes(w_in[0], 0, 1)
    d = w_t.shape[1]
    rb = REGROUP_ROWS
    n_tail = (Z_CQ - Z_DQ) // rb
    n_lat = (Z_KR - Z_CQ) // rb
    first_tail = n_lat + 1

    def src(u, step):
        i = step * REGROUP_PER_STEP + u
        return jnp.where(i < n_tail, i + first_tail, jnp.where(i < n_tail + n_lat, i - n_tail, n_lat)), 0

    per = REGROUP_PER_STEP
    assert Z_WIDTH % (per * rb) == 0
    return pl.pallas_call(
        _regroup_kernel,
        grid=(Z_WIDTH // (per * rb),),
        in_specs=[pl.BlockSpec((rb, d), functools.partial(src, u)) for u in range(per)],
        out_specs=pl.BlockSpec((per * rb, d), lambda s: (s, 0)),
        out_shape=jax.ShapeDtypeStruct((Z_WIDTH, d), BF16),
        compiler_params=_cparams(("parallel",)),
        name="regroup_w_in",
    )(*([w_t] * per))


def _inproj_kernel(x_ref, g_ref, w_ref, z_ref):
    x = x_ref[...]
    r = lax.rsqrt(jnp.mean(x * x, axis=-1, keepdims=True) + NORM_EPS)
    h = (x * r * g_ref[...]).astype(BF16)
    z_ref[...] = _dot_nt(h, w_ref[...]).astype(BF16)


def _inproj(x2, g1, w_z, tm=256):
    S = x2.shape[0]
    return pl.pallas_call(
        _inproj_kernel,
        grid=(S // tm,),
        in_specs=[
            pl.BlockSpec((tm, D_MODEL), lambda i: (i, 0)),
            pl.BlockSpec((1, D_MODEL), lambda i: (0, 0)),
            pl.BlockSpec((Z_WIDTH, D_MODEL), lambda i: (0, 0)),
        ],
        out_specs=pl.BlockSpec((tm, Z_WIDTH), lambda i: (i, 0)),
        out_shape=jax.ShapeDtypeStruct((S, Z_WIDTH), BF16),
        compiler_params=_cparams(("parallel",)),
        name="inproj",
    )(x2, g1, w_z)


def _prep_kernel(zdq_ref, zdk_ref, zdv_ref, zcq_ref, zckv_ref, zkr_ref,
                 gcq_ref, gckv_ref, wq_ref, wkv_ref, gq_ref, gkn_ref, gkp_ref,
                 gdq_ref, gdk_ref, t64_ref, cos_ref, sin_ref, p4_ref, p16_ref,
                 q_ref, k_ref, vt_ref, dq_ref, dk_ref,
                 dq4_ref, dk4_ref, dv4_ref, dq16_ref, dk16_ref, dv16_ref):
    tm = zcq_ref.shape[0]
    lane = lax.broadcasted_iota(jnp.int32, (1, LANES), 1)
    first_half = lane < MLA_ROPE_DIM

    def rms_rows(c, g):
        c = c.astype(F32)
        r = lax.rsqrt(jnp.mean(c * c, axis=-1, keepdims=True) + NORM_EPS)
        return (c * r * g).astype(BF16)

    cq = rms_rows(zcq_ref[...], gcq_ref[...])
    ckv = rms_rows(zckv_ref[...], gckv_ref[...])
    qe = _dot(cq, wq_ref[...])
    kv = _dot(ckv, wkv_ref[...])
    kr = zkr_ref[...].astype(F32)
    kr_ss = jnp.sum(jnp.where(first_half, kr * kr, 0.0), axis=-1, keepdims=True)
    t64 = t64_ref[...]
    q_scale = MLA_QK_DIM ** -0.5 * LOG2E

    def rope64(ext):
        t = ext * t64
        return jnp.where(first_half, t + pltpu.roll(t, MLA_ROPE_DIM, 1), 0.0)

    ones_rows = (lax.broadcasted_iota(jnp.int32, (BF16_ROWS, tm), 0) == 0).astype(BF16)
    eye = (lax.broadcasted_iota(jnp.int32, (MLA_V_DIM, MLA_V_DIM), 0)
           == lax.broadcasted_iota(jnp.int32, (MLA_V_DIM, MLA_V_DIM), 1)).astype(BF16)
    for h in range(N_MLA_HEADS):
        base = h * MLA_PAD
        qn = qe[:, base:base + LANES]
        qp = qe[:, base + LANES:base + 2 * LANES]
        ss = (jnp.sum(qn * qn, axis=-1, keepdims=True)
              + jnp.sum(jnp.where(first_half, qp * qp, 0.0), axis=-1, keepdims=True))
        r = lax.rsqrt(ss * (1.0 / MLA_QK_DIM) + NORM_EPS) * q_scale
        q_ref[:, base:base + LANES] = (qn * r * gq_ref[:, :LANES]).astype(BF16)
        q_ref[:, base + LANES:base + MLA_PAD] = rope64(qp * r * gq_ref[:, LANES:]).astype(BF16)

        kn = kv[:, base:base + LANES]
        ss = jnp.sum(kn * kn, axis=-1, keepdims=True) + kr_ss
        r = lax.rsqrt(ss * (1.0 / MLA_QK_DIM) + NORM_EPS)
        k_ref[:, base:base + LANES] = (kn * r * gkn_ref[...]).astype(BF16)
        k_ref[:, base + LANES:base + MLA_PAD] = rope64(kr * r * gkp_ref[...]).astype(BF16)
        v_h = kv[:, base + LANES:base + 2 * LANES].astype(BF16)
        vt_ref[h, :MLA_V_DIM, :] = _dot_nt(eye, v_h).astype(BF16)
        vt_ref[h, MLA_V_DIM:, :] = ones_rows

    cos = cos_ref[...]
    sin = sin_ref[...]
    d_scale = DIL_HEAD_DIM ** -0.5

    def dil_head(x, g, scale):
        x = x.astype(F32)
        r = lax.rsqrt(jnp.mean(x * x, axis=-1, keepdims=True) + NORM_EPS)
        y = x * r * g
        return ((y * cos + pltpu.roll(y, DIL_HEAD_DIM // 2, 1) * sin) * scale).astype(BF16)

    for h in range(N_DIL_HEADS):
        sl = slice(h * DIL_HEAD_DIM, (h + 1) * DIL_HEAD_DIM)
        dq_ref[:, sl] = dil_head(zdq_ref[:, sl], gdq_ref[...], d_scale)
        dk_ref[:, sl] = dil_head(zdk_ref[:, sl], gdk_ref[...], 1.0)

    for src, d4, d16 in ((dq_ref, dq4_ref, dq16_ref), (dk_ref, dk4_ref, dk16_ref), (zdv_ref, dv4_ref, dv16_ref)):
        for sub in range(tm // PERM_TILE):
            xs = src[sub * PERM_TILE:(sub + 1) * PERM_TILE, :]
            for dil, p_ref, dst in ((4, p4_ref, d4), (16, p16_ref, d16)):
                n = PERM_TILE // dil
                xp = _dot(p_ref[...], xs).astype(BF16)
                for r in range(dil):
                    dst[r, sub * n:(sub + 1) * n, :] = xp[r * n:(r + 1) * n, :]


def _prep(z, gcq, gckv, wq_ext, wkv, gq_ext, gk_nope, gk_pe, gdq, gdk, t64, cos128, sin128, p4, p16):
    S = z.shape[0]
    tm = ATT_VC
    row = lambda w, j: pl.BlockSpec((tm, w), lambda i, j=j: (i, j))
    full = lambda a: pl.BlockSpec(a.shape, lambda i: (0, 0))
    res = lambda dil: pl.BlockSpec((dil, tm // dil, HD), lambda i: (0, i, 0))
    qk_w = N_MLA_HEADS * MLA_PAD
    res_shape = lambda dil: jax.ShapeDtypeStruct((dil, S // dil, HD), BF16)
    return pl.pallas_call(
        _prep_kernel,
        grid=(S // tm,),
        in_specs=[
            row(HD, Z_DQ // HD), row(HD, Z_DK // HD), row(HD, Z_DV // HD),
            row(Q_LORA_RANK, Z_CQ // Q_LORA_RANK), row(KV_LORA_RANK, Z_CKV // KV_LORA_RANK),
            row(LANES, Z_KR // LANES),
            full(gcq), full(gckv), full(wq_ext), full(wkv), full(gq_ext), full(gk_nope), full(gk_pe),
            full(gdq), full(gdk),
            row(LANES, 0), row(LANES, 0), row(LANES, 0), full(p4), full(p16),
        ],
        out_specs=[row(qk_w, 0), row(qk_w, 0),
                   pl.BlockSpec((N_MLA_HEADS, None, VT_ROWS, tm), lambda i: (0, i, 0, 0)),
                   row(HD, 0), row(HD, 0),
                   res(4), res(4), res(4), res(16), res(16), res(16)],
        out_shape=[
            jax.ShapeDtypeStruct((S, qk_w), BF16), jax.ShapeDtypeStruct((S, qk_w), BF16),
            jax.ShapeDtypeStruct((N_MLA_HEADS, S // tm, VT_ROWS, tm), BF16),
            jax.ShapeDtypeStruct((S, HD), BF16), jax.ShapeDtypeStruct((S, HD), BF16),
            res_shape(4), res_shape(4), res_shape(4), res_shape(16), res_shape(16), res_shape(16),
        ],
        compiler_params=_cparams(("parallel",)),
        name="qkv_prep",
    )(z, z, z, z, z, z, gcq, gckv, wq_ext, wkv, gq_ext, gk_nope, gk_pe, gdq, gdk, t64, cos128, sin128, p4, p16)


def _mla_attn_kernel(q_ref, k_ref, vt_ref, o_ref, m_ref, acc_ref, s_ref):
    per_chunk = ATT_TK // ATT_VC
    n_chunks = vt_ref.shape[0] // per_chunk
    q = q_ref[...]

    def scores(c):
        off = pl.multiple_of(c * ATT_TK, ATT_TK)
        return _dot_nt(k_ref[pl.ds(off, ATT_TK), :], q)

    m_ref[...] = jnp.full(m_ref.shape, -jnp.inf, F32)
    acc_ref[...] = jnp.zeros(acc_ref.shape, F32)

    def fold(c, slot):
        for cols in (slice(0, ATT_TQ // 2), slice(ATT_TQ // 2, ATT_TQ)):
            s = s_ref[slot, :, cols]
            m_old = m_ref[:, cols]
            m_new = jnp.maximum(m_old, jnp.max(s, axis=0, keepdims=True))
            alpha = jnp.exp2(m_old - m_new)
            p = jnp.exp2(s - m_new).astype(BF16)
            pv = _dot(vt_ref[c * per_chunk], p[:ATT_VC])
            for u in range(1, per_chunk):
                pv += _dot(vt_ref[c * per_chunk + u], p[u * ATT_VC:(u + 1) * ATT_VC])
            acc_ref[:, cols] = alpha * acc_ref[:, cols] + pv
            m_ref[:, cols] = m_new

    s_ref[0] = scores(0)

    def body(t, carry):
        for u in range(ATT_UNROLL):
            c = ATT_UNROLL * t + u
            s_ref[(u + 1) % 2] = scores(jnp.minimum(c + 1, n_chunks - 1))
            fold(c, u % 2)
        return carry

    lax.fori_loop(0, n_chunks // ATT_UNROLL, body, 0)
    acc = acc_ref[...]
    o_t = acc[:MLA_V_DIM, :] / acc[MLA_V_DIM:MLA_V_DIM + 1, :]
    o_ref[...] = o_t.T.astype(o_ref.dtype)


def _mla_attn(q, k, vt):
    S = q.shape[0]
    n_chunks = vt.shape[1]
    return pl.pallas_call(
        _mla_attn_kernel,
        grid=(N_MLA_HEADS, S // ATT_TQ),
        in_specs=[
            pl.BlockSpec((ATT_TQ, MLA_PAD), lambda h, i: (i, h)),
            pl.BlockSpec((S, MLA_PAD), lambda h, i: (0, h)),
            pl.BlockSpec((None, n_chunks, VT_ROWS, ATT_VC), lambda h, i: (h, 0, 0, 0)),
        ],
        out_specs=pl.BlockSpec((ATT_TQ, MLA_V_DIM), lambda h, i: (i, h)),
        out_shape=jax.ShapeDtypeStruct((S, N_MLA_HEADS * MLA_V_DIM), BF16),
        scratch_shapes=[pltpu.VMEM((1, ATT_TQ), F32), pltpu.VMEM((VT_ROWS, ATT_TQ), F32),
                        pltpu.VMEM((2, ATT_TK, ATT_TQ), F32)],
        compiler_params=_cparams(("parallel", "parallel")),
        name="mla_attn",
    )(q, k, vt)


DIL_QB = 128


def _dil_kernel(q_ref, k_ref, v_ref, o_ref, lse_ref, *, half_w, hb, gb):
    L = q_ref.shape[0]
    kw = DIL_QB + 2 * half_w
    g_id = pl.program_id(1)
    lane = lax.broadcasted_iota(jnp.int32, (DIL_QB, LANES), 1)

    @pl.when(g_id == 0)
    def _():
        lse_ref[...] = jnp.zeros(lse_ref.shape, F32)

    def body(step, carry):
        qs_l, ks_l, qb, kb, vb = [], [], [], [], []
        for j in range(gb):
            qs = pl.multiple_of((step * gb + j) * DIL_QB, DIL_QB)
            ks = pl.multiple_of(jnp.clip(qs - half_w, 0, L - kw), half_w)
            qs_l.append(qs)
            ks_l.append(ks)
            for h in range(hb):
                sl = slice(h * DIL_HEAD_DIM, (h + 1) * DIL_HEAD_DIM)
                qb.append(q_ref[pl.ds(qs, DIL_QB), sl])
                kb.append(k_ref[pl.ds(ks, kw), sl])
                vb.append(v_ref[pl.ds(ks, kw), sl])
        q = jnp.stack(qb)
        k = jnp.stack(kb)
        v = jnp.stack(vb)
        s = jnp.einsum("gqd,gkd->gqk", q, k, preferred_element_type=F32)
        rel = (lax.broadcasted_iota(jnp.int32, (DIL_QB, kw), 0)
               - lax.broadcasted_iota(jnp.int32, (DIL_QB, kw), 1))
        bias = []
        for j in range(gb):
            mask = jnp.abs(rel + (qs_l[j] - ks_l[j])) <= half_w
            bias += [jnp.where(mask, 0.0, NEG_INF)] * hb
        s = s + jnp.stack(bias)
        m = jnp.max(s, axis=-1, keepdims=True)
        p = jnp.exp(s - m)
        den = jnp.sum(p, axis=-1, keepdims=True)
        o = jnp.einsum("gqk,gkd->gqd", p.astype(BF16), v, preferred_element_type=F32) / den
        lse = m + jnp.log(den)
        for j in range(gb):
            tile = lse_ref[pl.ds(qs_l[j], DIL_QB), :]
            for h in range(hb):
                sl = slice(h * DIL_HEAD_DIM, (h + 1) * DIL_HEAD_DIM)
                o_ref[pl.ds(qs_l[j], DIL_QB), sl] = o[j * hb + h].astype(o_ref.dtype)
                tile = jnp.where(lane == g_id * hb + h, lse[j * hb + h], tile)
            lse_ref[pl.ds(qs_l[j], DIL_QB), :] = tile
        return carry

    lax.fori_loop(0, L // (DIL_QB * gb), body, 0)


def _dil_pattern(dq, dk, dv, v_col0, window, dil, hb, gb):
    L = dq.shape[1]
    half_w = window // (2 * dil)
    w = hb * DIL_HEAD_DIM
    spec = lambda c0: pl.BlockSpec((None, L, w), lambda r, g, c0=c0: (r, 0, c0 + g))
    return pl.pallas_call(
        functools.partial(_dil_kernel, half_w=half_w, hb=hb, gb=gb),
        grid=(dil, N_DIL_HEADS // hb),
        in_specs=[spec(0), spec(0), spec(v_col0 // w)],
        out_specs=[spec(0), pl.BlockSpec((None, L, LANES), lambda r, g: (r, 0, 0))],
        out_shape=[jax.ShapeDtypeStruct((dil, L, HD), BF16), jax.ShapeDtypeStruct((dil, L, LANES), F32)],
        compiler_params=_cparams(("parallel", "arbitrary")),
        name=f"dil_attn_d{dil}",
    )(dq, dk, dv)


def _outproj_kernel(a_ref, o1_ref, l1_ref, o4_ref, l4_ref, o16_ref, l16_ref, p4t_ref, p16t_ref,
                    x_ref, wa_ref, wb_ref, g_ref, wr_ref, wrh_ref, br_ref,
                    x1_ref, xn_ref, ri_ref, rf_ref):
    tm = x_ref.shape[0]

    def to_token_order(o_ref, l_ref, pt_ref):
        pt = pt_ref[...]
        o = _dot(pt, o_ref[...].reshape(tm, HD))
        lse = l_ref[...].reshape(tm, LANES)
        hi = lse.astype(BF16)
        rem = lse - hi.astype(F32)
        mid = rem.astype(BF16)
        lo = (rem - mid.astype(F32)).astype(BF16)
        return o, _dot(pt, hi) + _dot(pt, mid) + _dot(pt, lo)

    o1 = o1_ref[...].astype(F32)
    l1 = l1_ref[...]
    o4, l4 = to_token_order(o4_ref, l4_ref, p4t_ref)
    o16, l16 = to_token_order(o16_ref, l16_ref, p16t_ref)
    big = jnp.maximum(jnp.maximum(l1, l4), l16)
    e1 = jnp.exp(l1 - big)
    e4 = jnp.exp(l4 - big)
    e16 = jnp.exp(l16 - big)
    inv = 1.0 / (e1 + e4 + e16)
    w1, w4, w16 = e1 * inv, e4 * inv, e16 * inv
    slabs = []
    for h in range(N_DIL_HEADS):
        sl = slice(h * DIL_HEAD_DIM, (h + 1) * DIL_HEAD_DIM)
        slabs.append(w1[:, h:h + 1] * o1[:, sl] + w4[:, h:h + 1] * o4[:, sl] + w16[:, h:h + 1] * o16[:, sl])
    dil_o = jnp.concatenate(slabs, axis=-1).astype(BF16)

    x1 = x_ref[...] + _dot(a_ref[...], wa_ref[...]) + _dot(dil_o, wb_ref[...])
    x1_ref[...] = x1
    r = lax.rsqrt(jnp.mean(x1 * x1, axis=-1, keepdims=True) + NORM_EPS)
    xn = x1 * r * g_ref[...]
    xn_ref[...] = xn
    xh = xn.astype(BF16)
    xl = (xn - xh.astype(F32)).astype(BF16)
    two = _dot(xh, wr_ref[...])
    logits = two[:, :LANES] + two[:, LANES:] + _dot(xl, wrh_ref[...]) + br_ref[...]
    lane = lax.broadcasted_iota(jnp.int32, logits.shape, 1)
    ninf = -jnp.inf

    def first_argmax(vals, vmax):
        return jnp.min(jnp.where(vals == vmax, lane, LANES), axis=-1, keepdims=True)

    coarse = jnp.where(lane < N_EXPERT_GROUPS, logits, ninf)
    cmax = jnp.max(coarse, axis=-1, keepdims=True)
    g = first_argmax(coarse, cmax)
    p_g = 1.0 / jnp.sum(jnp.exp(coarse - cmax), axis=-1, keepdims=True)
    lo_lane = N_EXPERT_GROUPS + g * EXPERTS_PER_GROUP
    fine = jnp.where((lane >= lo_lane) & (lane < lo_lane + EXPERTS_PER_GROUP), logits, ninf)
    v1 = jnp.max(fine, axis=-1, keepdims=True)
    j1 = first_argmax(fine, v1)
    fine2 = jnp.where(lane == j1, ninf, fine)
    v2 = jnp.max(fine2, axis=-1, keepdims=True)
    j2 = first_argmax(fine2, v2)
    e2 = jnp.exp(v2 - v1)
    g1 = 1.0 / (1.0 + e2)
    g2 = e2 / (1.0 + e2)
    ri_ref[...] = jnp.where(lane == 0, j1 - N_EXPERT_GROUPS, jnp.where(lane == 1, j2 - N_EXPERT_GROUPS, 0))
    rf_ref[...] = jnp.where(lane == 0, p_g * g1, jnp.where(lane == 1, p_g * g2, 0.0))


def _outproj_router(mla_o, dil1, dil4, dil16, p4t, p16t, x2, w_out_bf, g2, wr_two, wr_hi, b_router):
    S = x2.shape[0]
    tm = PERM_TILE
    half = N_MLA_HEADS * MLA_V_DIM
    row = lambda w: pl.BlockSpec((tm, w), lambda i: (i, 0))
    res = lambda dil, w: pl.BlockSpec((dil, tm // dil, w), lambda i: (0, i, 0))
    const = lambda a: pl.BlockSpec(a.shape, lambda i: (0, 0))
    return pl.pallas_call(
        _outproj_kernel,
        grid=(S // tm,),
        in_specs=[
            row(half),
            pl.BlockSpec((None, tm, HD), lambda i: (0, i, 0)), pl.BlockSpec((None, tm, LANES), lambda i: (0, i, 0)),
            res(4, HD), res(4, LANES), res(16, HD), res(16, LANES), const(p4t), const(p16t),
            row(D_MODEL),
            pl.BlockSpec((half, D_MODEL), lambda i: (0, 0)),
            pl.BlockSpec((HD, D_MODEL), lambda i: (1, 0)),
            const(g2), const(wr_two), const(wr_hi), const(b_router),
        ],
        out_specs=[row(D_MODEL), row(D_MODEL), row(LANES), row(LANES)],
        out_shape=[
            jax.ShapeDtypeStruct((S, D_MODEL), F32), jax.ShapeDtypeStruct((S, D_MODEL), F32),
            jax.ShapeDtypeStruct((S, LANES), jnp.int32), jax.ShapeDtypeStruct((S, LANES), F32),
        ],
        compiler_params=_cparams(("parallel",)),
        name="outproj_router",
    )(mla_o, dil1[0], dil1[1], dil4[0], dil4[1], dil16[0], dil16[1], p4t, p16t,
      x2, w_out_bf, w_out_bf, g2, wr_two, wr_hi, b_router)


MOE_CAP = 512
MOE_RH = 128
MOE_KC = 512
MOE_NK = D_MODEL // MOE_KC
MOE_FB = 768
MOE_NB = 2
MOE_FB_LAST = EXPERT_FF - (MOE_NB - 1) * MOE_FB
MOE_STEPS = MOE_NK + MOE_NB
MOE_LOOKAHEAD = 3
MOE_UP_RING = MOE_LOOKAHEAD + 1
MOE_UP_PARTS = ((0, 1024), (1024, 2048), (2048, 2 * EXPERT_FF))
MOE_DOWN_PARTS = ((0, 1024), (1024, D_MODEL))
DMA_UNROLL = 8
assert 0 < MOE_FB_LAST <= MOE_FB and MOE_FB % LANES == 0 and MOE_FB_LAST % LANES == 0


def _moe_kernel(ie_ref, is_ref, in_ref, ord_ref,
                xn_hbm, w1_hbm, w3_hbm, w2_hbm,
                y_hbm, xf_ref, xb_ref, ab_ref, h_ref, acc_ref, w1r, w3r, w2r,
                sem_in, sem_out, sem_up, sem_dn):
    i = pl.program_id(0)
    j = pl.program_id(1)
    n_items = pl.num_programs(0)
    n_tokens = xn_hbm.shape[0]
    n = in_ref[i]
    slot = 0

    def up_ring(item, kc):
        return lax.rem(item * MOE_NK + kc, MOE_UP_RING)

    def block_copies(item, step):
        e = ie_ref[item]
        if step < MOE_NK:
            ring = up_ring(item, step)
            rows = pl.ds(step * MOE_KC, MOE_KC)
            return (pltpu.make_async_copy(w1_hbm.at[e, rows, :], w1r.at[ring], sem_up.at[ring]),
                    pltpu.make_async_copy(w3_hbm.at[e, rows, :], w3r.at[ring], sem_up.at[ring]))
        blk = step - MOE_NK
        size = MOE_FB if blk < MOE_NB - 1 else MOE_FB_LAST
        return (pltpu.make_async_copy(w2_hbm.at[e, pl.ds(blk * MOE_FB, size), :],
                                      w2r.at[blk, pl.ds(0, size), :], sem_dn.at[blk]),)

    def start_block(item, step):
        for c in block_copies(item, step):
            c.start()

    for js in range(MOE_STEPS):
        @pl.when((n > 0) & (j == js))
        def _():
            if js == 0:
                @pl.when(i == 0)
                def _():
                    for s0 in range(MOE_LOOKAHEAD):
                        start_block(0, s0)
            for c in block_copies(i, js):
                c.wait()
            tgt = js + MOE_LOOKAHEAD
            if tgt < MOE_STEPS:
                start_block(i, tgt)
            else:
                nxt = jnp.minimum(i + 1, n_items - 1)

                @pl.when((i + 1 < n_items) & (in_ref[nxt] > 0))
                def _():
                    start_block(nxt, tgt - MOE_STEPS)

    def row_in(item, sl, base, u, width):
        tok = lax.shift_right_logical(ord_ref[is_ref[item] + base + u], 1)
        dst = xf_ref.at[sl, pl.ds(base, width), :].at[pl.ds(u, 1), :]
        return pltpu.make_async_copy(xn_hbm.at[pl.ds(tok, 1), :], dst, sem_in.at[sl])

    def row_out(item, base, u, width):
        a = ord_ref[is_ref[item] + base + u]
        dst = (a & 1) * n_tokens + lax.shift_right_logical(a, 1)
        src = acc_ref.at[pl.ds(base, width), :].at[pl.ds(u, 1), :]
        return pltpu.make_async_copy(src, y_hbm.at[pl.ds(dst, 1), :], sem_out)

    def for_rows(count, fn):
        groups = lax.div(count, DMA_UNROLL)

        def group(g, c):
            base = pl.multiple_of(g * DMA_UNROLL, DMA_UNROLL)
            for u in range(DMA_UNROLL):
                fn(base, u, DMA_UNROLL)
            return c
        lax.fori_loop(0, groups, group, 0)

        def single(r, c):
            fn(r, 0, 1)
            return c
        lax.fori_loop(groups * DMA_UNROLL, count, single, 0)

    def wait_rows(count, rows_desc):
        groups = lax.div(count, DMA_UNROLL)

        def group(g, c):
            rows_desc(pl.multiple_of(g * DMA_UNROLL, DMA_UNROLL), DMA_UNROLL).wait()
            return c
        lax.fori_loop(0, groups, group, 0)

        def single(r, c):
            rows_desc(r, 1).wait()
            return c
        lax.fori_loop(groups * DMA_UNROLL, count, single, 0)

    def rows_in(sl, r0, size):
        return pltpu.make_async_copy(xn_hbm.at[pl.ds(0, size), :], xf_ref.at[sl, pl.ds(r0, size), :], sem_in.at[sl])

    def rows_out(r0, size):
        return pltpu.make_async_copy(acc_ref.at[pl.ds(r0, size), :], y_hbm.at[pl.ds(0, size), :], sem_out)

    def for_row_tiles(fn):
        tiles = lax.div(n + (MOE_RH - 1), MOE_RH)
        for k in range(1, MOE_CAP // MOE_RH + 1):
            @pl.when(tiles == k)
            def _():
                fn(0, k * MOE_RH)

    @pl.when((i == 0) & (j == 0))
    def _():
        xf_ref[...] = jnp.zeros(xf_ref.shape, F32)
        for_rows(n, lambda *row: row_in(0, 0, *row).start())

    @pl.when((n > 0) & (j == 0))
    def _():
        wait_rows(n, functools.partial(rows_in, slot))

        @pl.when(i > 0)
        def _():
            wait_rows(in_ref[i - 1], rows_out)

        for t in range(MOE_CAP // MOE_RH):
            @pl.when(t * MOE_RH < n)
            def _():
                sl = pl.ds(t * MOE_RH, MOE_RH)
                for kc in range(MOE_NK):
                    xb_ref[kc, sl, :] = xf_ref[slot, sl, kc * MOE_KC:(kc + 1) * MOE_KC].astype(BF16)
                ab_ref[sl, :] = jnp.zeros((MOE_RH, 2 * EXPERT_FF), F32)

    @pl.when((n > 0) & (j == 1) & (i + 1 < n_items))
    def _():
        nxt = jnp.minimum(i + 1, n_items - 1)
        for_rows(in_ref[nxt], lambda *row: row_in(nxt, slot, *row).start())

    @pl.when((n > 0) & (j < MOE_NK))
    def _():
        kc = jnp.minimum(j, MOE_NK - 1)
        ring = up_ring(i, kc)

        def cat_cols(c0, c1):
            pieces = []
            if c0 < EXPERT_FF:
                pieces.append(w1r[ring, :, c0:min(c1, EXPERT_FF)].astype(BF16))
            if c1 > EXPERT_FF:
                pieces.append(w3r[ring, :, max(c0, EXPERT_FF) - EXPERT_FF:c1 - EXPERT_FF].astype(BF16))
            return pieces[0] if len(pieces) == 1 else jnp.concatenate(pieces, axis=-1)

        def up(off, size):
            xs = xb_ref[kc, pl.ds(off, size), :]
            for c0, c1 in MOE_UP_PARTS:
                ab_ref[pl.ds(off, size), c0:c1] += _dot(xs, cat_cols(c0, c1))
        for_row_tiles(up)

    @pl.when((n > 0) & (j == MOE_NK))
    def _():
        def act(off, size):
            a = ab_ref[pl.ds(off, size), :EXPERT_FF]
            b = ab_ref[pl.ds(off, size), EXPERT_FF:]
            half_a = 0.5 * a
            h_ref[pl.ds(off, size), :] = ((half_a + half_a * jnp.tanh(half_a)) * b).astype(BF16)
        for_row_tiles(act)

        def down(off, size):
            hs = h_ref[pl.ds(off, size), :MOE_FB]
            for c0, c1 in MOE_DOWN_PARTS:
                acc_ref[pl.ds(off, size), c0:c1] = _dot(hs, w2r[0, :, c0:c1].astype(BF16))
        for_row_tiles(down)

    @pl.when((n > 0) & (j == MOE_NK + 1))
    def _():
        def down(off, size):
            hs = h_ref[pl.ds(off, size), MOE_FB:]
            for c0, c1 in MOE_DOWN_PARTS:
                acc_ref[pl.ds(off, size), c0:c1] += _dot(hs, w2r[MOE_NB - 1, :MOE_FB_LAST, c0:c1].astype(BF16))
        for_row_tiles(down)
        for_rows(n, lambda *row: row_out(i, *row).start())
        nxt = jnp.minimum(i + 1, n_items - 1)

        @pl.when((i == n_items - 1) | (in_ref[nxt] == 0))
        def _():
            wait_rows(n, rows_out)


def _moe_experts(xn, w1, w3, w2, item_e, item_start, item_n, order):
    S = xn.shape[0]
    n_items = item_e.shape[0]

    hbm = pl.BlockSpec(memory_space=pl.ANY)
    grid_spec = pltpu.PrefetchScalarGridSpec(
        num_scalar_prefetch=4,
        grid=(n_items, MOE_STEPS),
        in_specs=[hbm, hbm, hbm, hbm],
        out_specs=hbm,
        scratch_shapes=[
            pltpu.VMEM((1, MOE_CAP, D_MODEL), F32),
            pltpu.VMEM((MOE_NK, MOE_CAP, MOE_KC), BF16),
            pltpu.VMEM((MOE_CAP, 2 * EXPERT_FF), F32),
            pltpu.VMEM((MOE_CAP, EXPERT_FF), BF16),
            pltpu.VMEM((MOE_CAP, D_MODEL), F32),
            pltpu.VMEM((MOE_UP_RING, MOE_KC, EXPERT_FF), F32),
            pltpu.VMEM((MOE_UP_RING, MOE_KC, EXPERT_FF), F32),
            pltpu.VMEM((MOE_NB, MOE_FB, D_MODEL), F32),
            pltpu.SemaphoreType.DMA((1,)),
            pltpu.SemaphoreType.DMA(()),
            pltpu.SemaphoreType.DMA((MOE_UP_RING,)),
            pltpu.SemaphoreType.DMA((MOE_NB,)),
        ],
    )
    return pl.pallas_call(
        _moe_kernel,
        grid_spec=grid_spec,
        out_shape=jax.ShapeDtypeStruct((TOP_K * S, D_MODEL), F32),
        compiler_params=_cparams(("arbitrary", "arbitrary")),
        name="moe_experts",
    )(item_e, item_start, item_n, order, xn, w1, w3, w2)


def _moe_items(eid, cap):
    A = eid.size
    flat_e = eid.reshape(A)
    order = jnp.argsort(flat_e, stable=True).astype(jnp.int32)
    experts = jnp.arange(N_EXPERTS + 1, dtype=jnp.int32)
    starts = jnp.sum((flat_e[None, :] < experts[:, None]).astype(jnp.int32), axis=1)
    counts = starts[1:] - starts[:-1]
    per_e = (counts + cap - 1) // cap
    item_end = jnp.cumsum(per_e)
    total = item_end[-1]
    n_items = N_EXPERTS + A // cap
    idx = jnp.arange(n_items, dtype=jnp.int32)
    clamped = jnp.minimum(idx, total - 1)
    first_after = jnp.sum((item_end[None, :] <= clamped[:, None]).astype(jnp.int32), axis=1)
    e = jnp.minimum(first_after, N_EXPERTS - 1)
    local = clamped - (item_end[e] - per_e[e])
    used = idx < total
    item_start = jnp.where(used, starts[e] + local * cap, 0).astype(jnp.int32)
    item_n = jnp.where(used, jnp.clip(counts[e] - local * cap, 0, cap), 0).astype(jnp.int32)
    return e, item_start, item_n, order


def _combine_kernel(x1_ref, y0_ref, y1_ref, g_ref, o_ref):
    g = g_ref[...]
    o_ref[...] = x1_ref[...] + (g[:, 0:1] * y0_ref[...] + g[:, 1:2] * y1_ref[...])


def _combine(x1, y, gates, tm=512):
    S = x1.shape[0]
    nb = S // tm
    return pl.pallas_call(
        _combine_kernel,
        grid=(nb,),
        in_specs=[
            pl.BlockSpec((tm, D_MODEL), lambda i: (i, 0)),
            pl.BlockSpec((tm, D_MODEL), lambda i: (i, 0)),
            pl.BlockSpec((tm, D_MODEL), lambda i: (nb + i, 0)),
            pl.BlockSpec((tm, LANES), lambda i: (i, 0)),
        ],
        out_specs=pl.BlockSpec((tm, D_MODEL), lambda i: (i, 0)),
        out_shape=jax.ShapeDtypeStruct((S, D_MODEL), F32),
        compiler_params=_cparams(("parallel",)),
        name="moe_combine",
    )(x1, y, y, gates)


def _rot_half_cols(w, half):
    return jnp.concatenate([-w[..., half:], w[..., :half]], axis=-1)


def _rope_tables(S):
    pos = np.arange(S, dtype=np.float32)[:, None]

    def cs(half):
        inv = (np.float32(ROPE_THETA) ** (-np.arange(half, dtype=np.float32) / np.float32(half))).astype(np.float32)
        ang = (pos * inv[None, :]).astype(np.float64)
        return np.cos(ang).astype(np.float32), np.sin(ang).astype(np.float32)

    c32, s32 = cs(MLA_ROPE_DIM // 2)
    t64 = np.concatenate([c32, c32, s32, s32], axis=-1)
    c64, s64 = cs(DIL_HEAD_DIM // 2)
    cos128 = np.concatenate([c64, c64], axis=-1)
    sin128 = np.concatenate([-s64, s64], axis=-1)
    return jnp.asarray(t64), jnp.asarray(cos128), jnp.asarray(sin128)


def _residue_perm(dil):
    n = PERM_TILE // dil
    rows = np.arange(PERM_TILE)
    src = (rows % n) * dil + rows // n
    return jnp.asarray(src[:, None] == np.arange(PERM_TILE)[None, :], dtype=BF16)


def kernel(x, norm1_g, w_in, g_cq, g_ckv, w_uq, w_ukv, mla_q_norm_g, mla_k_norm_g, dil_q_norm_g,
           dil_k_norm_g, w_out, norm2_g, w_group, b_group, w_expert, b_expert, w1, w3, w2):
    B, S, D = x.shape
    assert B == 1 and D == D_MODEL and norm1_g.shape[0] == 1
    x2 = x.reshape(S, D)
    half_r = MLA_ROPE_DIM // 2
    o1, o2 = Q_LORA_RANK, Q_LORA_RANK + KV_LORA_RANK
    o3 = o2 + MLA_ROPE_DIM

    w_z = _regroup_w_in(w_in)

    wq = w_uq[0].reshape(Q_LORA_RANK, N_MLA_HEADS, MLA_QK_DIM)
    wq_pe = wq[..., MLA_NOPE_DIM:]
    wq_ext = jnp.concatenate([wq, _rot_half_cols(wq_pe, half_r)], axis=-1)
    wq_ext = wq_ext.reshape(Q_LORA_RANK, N_MLA_HEADS * MLA_PAD).astype(BF16)
    wkv = w_ukv[0].astype(BF16)

    def ext_gain(g):
        pe = g[MLA_NOPE_DIM:]
        return jnp.concatenate([g, pe[half_r:], pe[:half_r]])[None, :]

    gq_ext = ext_gain(mla_q_norm_g[0])
    gk_ext = ext_gain(mla_k_norm_g[0])
    t64, cos128, sin128 = _rope_tables(S)
    p4, p16 = _residue_perm(4), _residue_perm(16)

    z = _inproj(x2, norm1_g, w_z)
    q, k, vt, dq, dk, dq4, dk4, dv4, dq16, dk16, dv16 = _prep(
        z, g_cq, g_ckv, wq_ext, wkv, gq_ext, gk_ext[:, :LANES], gk_ext[:, LANES:],
        dil_q_norm_g, dil_k_norm_g, t64, cos128, sin128, p4, p16)
    mla_o = _mla_attn(q, k, vt)

    (w_1, d_1), (w_4, d_4), (w_16, d_16) = DIL_PATTERNS
    dil1 = _dil_pattern(dq[None], dk[None], z[None], Z_DV, w_1, d_1, hb=2, gb=4)
    dil4 = _dil_pattern(dq4, dk4, dv4, 0, w_4, d_4, hb=8, gb=1)
    dil16 = _dil_pattern(dq16, dk16, dv16, 0, w_16, d_16, hb=8, gb=1)

    pad = LANES - N_EXPERT_GROUPS - N_EXPERTS
    w_router = jnp.concatenate([w_group[0], w_expert[0], jnp.zeros((D, pad), F32)], axis=-1)
    b_router = jnp.concatenate([b_group[0], b_expert[0], jnp.zeros((pad,), F32)])[None, :]
    wr_hi = w_router.astype(BF16)
    wr_lo = (w_router - wr_hi.astype(F32)).astype(BF16)
    x1, xn, route_i, route_f = _outproj_router(
        mla_o, dil1, dil4, dil16, p4.T, p16.T, x2, w_out[0].astype(BF16), norm2_g,
        jnp.concatenate([wr_hi, wr_lo], axis=-1), wr_hi, b_router)

    item_e, item_start, item_n, order = _moe_items(route_i[:, :TOP_K], MOE_CAP)
    y = _moe_experts(xn, w1[0], w3[0], w2[0], item_e, item_start, item_n, order)
    out = _combine(x1, y, route_f)
    return out.reshape(B, S, D)
```
